```python
import math
import jax, jax.numpy as jnp
from jax import lax
import numpy as np

D_MODEL = 2048
BATCH = 2
SEQ = 16384
DEPTH = 1

ATTN_HEADS = 8
HEAD_DIM = 128
ATTN_WIDTH = ATTN_HEADS * HEAD_DIM
IDX_HEADS = 8
IDX_DIM = 64
DSA_TOPK = 256
Q_BLOCK = 128
GMLP_GROUPS = 8
GMLP_GROUP_CH = 128
GMLP_WIDTH = GMLP_GROUPS * GMLP_GROUP_CH
CHUNK = 128
N_BRANCH = 2
N_KEYS = 128
N_EXPERTS = N_KEYS * N_KEYS
PEER_HEADS = 8
PEER_QDIM = 256
PEER_TOPK = 16
PEER_TOK_BLOCK = 128
EPS = 1e-6

kernel_name = "hybrid_dsa_gmlp_peer_block"


def rms_norm(x, g):
    xf = x.astype(jnp.float32)
    y = xf * lax.rsqrt(jnp.mean(xf * xf, axis=-1, keepdims=True) + EPS)
    return (y * g.astype(jnp.float32)).astype(x.dtype)


def dsa_attention(q, k, v, q_idx, k_idx, w_idx):
    B, S = q.shape[0], q.shape[1]
    n_sel = min(DSA_TOPK, S // 4)
    nb = S // Q_BLOCK

    def to_blocks(a):
        return jnp.swapaxes(a.reshape((B, nb, Q_BLOCK) + a.shape[2:]), 0, 1)

    q_pos = jnp.arange(S, dtype=jnp.int32).reshape(nb, Q_BLOCK)
    key_pos = jnp.arange(S, dtype=jnp.int32)
    bidx = jnp.arange(B, dtype=jnp.int32)[:, None, None]
    idx_scale = IDX_DIM ** -0.5
    attn_scale = HEAD_DIM ** -0.5

    def block(args):
        qb, qib, wb, qpos = args
        dots = jnp.einsum('bqhd,bsd->bqhs', qib, k_idx).astype(jnp.float32) * idx_scale
        score = jnp.einsum('bqhs,bqh->bqs', jax.nn.relu(dots), wb.astype(jnp.float32))
        causal = key_pos[None, None, :] <= qpos[None, :, None]
        score = jnp.where(causal, score, -jnp.inf)
        _, sel = lax.top_k(score, n_sel)
        valid = sel <= qpos[None, :, None]
        k_sel = k[bidx, sel]
        v_sel = v[bidx, sel]
        logits = jnp.einsum('bqhd,bqkhd->bqhk', qb, k_sel).astype(jnp.float32) * attn_scale
        logits = jnp.where(valid[:, :, None, :], logits, -jnp.inf)
        p = jax.nn.softmax(logits, axis=-1).astype(v.dtype)
        return jnp.einsum('bqhk,bqkhd->bqhd', p, v_sel)

    out = lax.map(block, (to_blocks(q), to_blocks(q_idx), to_blocks(w_idx), q_pos))
    return jnp.swapaxes(out, 0, 1).reshape(B, S, ATTN_WIDTH)


def chunked_spatial_gating(u, v, norm_g, w_s, b_s):
    B, S = u.shape[0], u.shape[1]
    nc = S // CHUNK
    u = jax.nn.gelu(u)
    v = rms_norm(jax.nn.gelu(v), norm_g)
    v = v.reshape(B, nc, CHUNK, GMLP_GROUPS, GMLP_GROUP_CH)
    mask = jnp.tril(jnp.ones((CHUNK, CHUNK), dtype=w_s.dtype))
    z = jnp.einsum('gts,bnsgc->bntgc', w_s * mask[None], v)
    z = z + jnp.swapaxes(b_s, 0, 1)[None, None, :, :, None]
    return u * z.reshape(B, S, GMLP_WIDTH)


def peer(h, w_q, sub_keys, u_emb, v_emb):
    B, S, D = h.shape
    tokens = h.reshape(-1, PEER_TOK_BLOCK, D)
    half = PEER_QDIM // 2

    def block(hb):
        T = hb.shape[0]
        q = (hb @ w_q).reshape(T, PEER_HEADS, 2, half)
        s = jnp.einsum('thpd,hpnd->thpn', q, sub_keys).astype(jnp.float32)
        top_s, top_i = lax.top_k(s, PEER_TOPK)
        cand_s = top_s[:, :, 0, :, None] + top_s[:, :, 1, None, :]
        cand_i = top_i[:, :, 0, :, None] * N_KEYS + top_i[:, :, 1, None, :]
        best_s, best_j = lax.top_k(cand_s.reshape(T, PEER_HEADS, PEER_TOPK * PEER_TOPK), PEER_TOPK)
        expert = jnp.take_along_axis(cand_i.reshape(T, PEER_HEADS, PEER_TOPK * PEER_TOPK), best_j, axis=-1)
        gate = jax.nn.softmax(best_s, axis=-1).astype(hb.dtype)
        u_sel = u_emb[expert]
        v_sel = v_emb[expert]
        act = jax.nn.gelu(jnp.einsum('td,thkd->thk', hb, u_sel))
        return jnp.einsum('thk,thkd->td', gate * act, v_sel)

    return lax.map(block, tokens).reshape(B, S, D)


def setup_inputs(seed: int = 0) -> dict:
    key = jax.random.key(seed)
    ks = jax.random.split(key, 16)
    L = DEPTH
    n_in = 3 * ATTN_WIDTH + IDX_HEADS * IDX_DIM + IDX_DIM + IDX_HEADS + 2 * GMLP_WIDTH + N_BRANCH * D_MODEL

    def nrm(k, shape, scale):
        return jax.random.normal(k, shape, jnp.float32) * scale

    return {
        "x": nrm(ks[0], (BATCH, SEQ, D_MODEL), 1.0),
        "ln_mix_g": 1.0 + nrm(ks[1], (L, D_MODEL), 0.02),
        "w_in": nrm(ks[2], (L, D_MODEL, n_in), D_MODEL ** -0.5),
        "gmlp_norm_g": 1.0 + nrm(ks[3], (L, GMLP_WIDTH), 0.02),
        "w_spatial": nrm(ks[4], (L, GMLP_GROUPS, CHUNK, CHUNK), CHUNK ** -0.5),
        "b_spatial": nrm(ks[5], (L, GMLP_GROUPS, CHUNK), 0.02),
        "w_branch_attn": nrm(ks[6], (L, ATTN_WIDTH, D_MODEL), ATTN_WIDTH ** -0.5),
        "w_branch_gmlp": nrm(ks[7], (L, GMLP_WIDTH, D_MODEL), GMLP_WIDTH ** -0.5),
        "w_out": nrm(ks[8], (L, D_MODEL, D_MODEL), D_MODEL ** -0.5),
        "ln_ffn_g": 1.0 + nrm(ks[9], (L, D_MODEL), 0.02),
        "peer_w_q": nrm(ks[10], (L, D_MODEL, PEER_HEADS * PEER_QDIM), D_MODEL ** -0.5),
        "peer_sub_keys": nrm(ks[11], (L, PEER_HEADS, 2, N_KEYS, PEER_QDIM // 2), (PEER_QDIM // 2) ** -0.5),
        "peer_u": nrm(ks[12], (L, N_EXPERTS, D_MODEL), D_MODEL ** -0.5),
        "peer_v": nrm(ks[13], (L, N_EXPERTS, D_MODEL), (PEER_HEADS * PEER_TOPK) ** -0.5),
        "ln_final_g": 1.0 + nrm(ks[14], (D_MODEL,), 0.02),
    }


def reference(x, ln_mix_g, w_in, gmlp_norm_g, w_spatial, b_spatial, w_branch_attn, w_branch_gmlp,
              w_out, ln_ffn_g, peer_w_q, peer_sub_keys, peer_u, peer_v, ln_final_g):
    B, S, _ = x.shape
    sizes = [ATTN_WIDTH, ATTN_WIDTH, ATTN_WIDTH, IDX_HEADS * IDX_DIM, IDX_DIM, IDX_HEADS,
             GMLP_WIDTH, GMLP_WIDTH, D_MODEL, D_MODEL]
    split_at = [int(c) for c in np.cumsum(sizes)[:-1]]
    for l in range(DEPTH):
        h = rms_norm(x, ln_mix_g[l])
        proj = h @ w_in[l]
        q, k, v, qi, ki, wi, gu, gv, ga, gb = jnp.split(proj, split_at, axis=-1)
        q = q.reshape(B, S, ATTN_HEADS, HEAD_DIM)
        k = k.reshape(B, S, ATTN_HEADS, HEAD_DIM)
        v = v.reshape(B, S, ATTN_HEADS, HEAD_DIM)
        qi = qi.reshape(B, S, IDX_HEADS, IDX_DIM)
        wi = wi * (IDX_HEADS ** -0.5)
        y_a = dsa_attention(q, k, v, qi, ki, wi)
        y_b = chunked_spatial_gating(gu, gv, gmlp_norm_g[l], w_spatial[l], b_spatial[l])
        mixed = jax.nn.sigmoid(ga) * (y_a @ w_branch_attn[l]) + jax.nn.sigmoid(gb) * (y_b @ w_branch_gmlp[l])
        x = x + mixed @ w_out[l]
        x = x + peer(rms_norm(x, ln_ffn_g[l]), peer_w_q[l], peer_sub_keys[l], peer_u[l], peer_v[l])
    return rms_norm(x, ln_final_g)
```

```python
import functools
import math

import numpy as np
import jax
import jax.numpy as jnp
from jax import lax
from jax.experimental import pallas as pl
from jax.experimental.pallas import tpu as pltpu

F32 = jnp.float32
BF16 = jnp.bfloat16
I32 = jnp.int32

ATTN_HEADS = 8
HEAD_DIM = 128
ATTN_WIDTH = ATTN_HEADS * HEAD_DIM
IDX_HEADS = 8
IDX_DIM = 64
DSA_TOPK = 256
GMLP_GROUPS = 8
CHUNK = 128
GMLP_WIDTH = GMLP_GROUPS * CHUNK
N_KEYS = 128
PEER_HEADS = 8
PEER_TOPK = 16
EPS = 1e-6

LANES = 128
VMEM_LIMIT_BYTES = 56 * 1024 * 1024

INT_MIN = -(2 ** 31)
NEG_BIG = -1e30

TQ = 128
TK = 512
PROJ_TM = 512
PROJ_TN = 512
MIX_TT = 256
PEER_TT = 512
PEER_ROWS = 8
SEL_TL = 512


def _cparams(sem):
    return pltpu.CompilerParams(dimension_semantics=sem, vmem_limit_bytes=VMEM_LIMIT_BYTES)


def _norm_proj_kernel(x_ref, g_ref, w_ref, o_ref, h_scr):
    @pl.when(pl.program_id(1) == 0)
    def _():
        x = x_ref[...]
        ms = jnp.mean(x * x, axis=-1, keepdims=True)
        h_scr[...] = ((x * lax.rsqrt(ms + EPS)) * g_ref[...]).astype(BF16)

    o_ref[...] = jnp.dot(h_scr[...], w_ref[...], preferred_element_type=F32).astype(o_ref.dtype)


def _norm_proj(x2, g, w_bf, out_dtype, tn):
    t, d = x2.shape
    n = w_bf.shape[1]
    tm = min(PROJ_TM, t)
    return pl.pallas_call(
        _norm_proj_kernel,
        out_shape=jax.ShapeDtypeStruct((t, n), out_dtype),
        grid=(t // tm, n // tn),
        in_specs=[
            pl.BlockSpec((tm, d), lambda i, j: (i, 0)),
            pl.BlockSpec((1, d), lambda i, j: (0, 0)),
            pl.BlockSpec((d, tn), lambda i, j: (0, j)),
        ],
        out_specs=pl.BlockSpec((tm, tn), lambda i, j: (i, j)),
        scratch_shapes=[pltpu.VMEM((tm, d), BF16)],
        compiler_params=_cparams(("parallel", "arbitrary")),
        name="norm_proj",
    )(x2, g.reshape(1, d), w_bf)


def _index_keys(qi, kit_tile, wb_ref, tk):
    dots = jnp.dot(qi, kit_tile, preferred_element_type=F32)
    dots = jnp.maximum(dots * (IDX_DIM ** -0.5), 0.0)
    out = []
    for c in range(tk // LANES):
        acc = None
        for h in range(IDX_HEADS):
            term = dots[h * TQ:(h + 1) * TQ, c * LANES:(c + 1) * LANES] * wb_ref[h * TQ:(h + 1) * TQ, :]
            acc = term if acc is None else acc + term
        bits = pltpu.bitcast(acc + 0.0, I32)
        out.append(bits ^ ((bits >> 31) & 0x7FFFFFFF))
    return out


def _fill_wb(wi_ref, wb_ref):
    wb_ref[...] = jnp.broadcast_to(wi_ref[...] * (IDX_HEADS ** -0.5), wb_ref.shape)


def _dsa_select_kernel(qi_ref, wi_ref, kit_ref, t_ref, j_ref, keys_scr, wb_scr, *, n_sel, seq):
    i = pl.program_id(1)
    n_kt = (i * TQ + TQ + TK - 1) // TK
    _fill_wb(wi_ref, wb_scr)
    qi = qi_ref[...]
    q_pos = i * TQ + lax.broadcasted_iota(I32, (TQ, LANES), 0)
    lane = lax.broadcasted_iota(I32, (TQ, LANES), 1)

    def score_body(kt, carry):
        off = pl.multiple_of(kt * TK, TK)
        keys = _index_keys(qi, kit_ref[:, pl.ds(off, TK)], wb_scr, TK)
        for c, kc in enumerate(keys):
            pos = off + c * LANES + lane
            keys_scr[:, pl.ds(pl.multiple_of(off + c * LANES, LANES), LANES)] = jnp.where(pos <= q_pos, kc, INT_MIN)
        return carry

    lax.fori_loop(0, n_kt, score_body, 0)

    def count(pred):
        def body(kt, acc):
            off = pl.multiple_of(kt * TK, TK)
            for c in range(TK // LANES):
                blk = keys_scr[:, pl.ds(pl.multiple_of(off + c * LANES, LANES), LANES)]
                acc = acc + pred(blk, off + c * LANES)
            return acc
        acc = lax.fori_loop(0, n_kt, body, jnp.zeros((TQ, LANES), I32))
        return jnp.sum(acc.astype(F32), axis=1, keepdims=True).astype(I32)

    def count_ge(cand):
        cb = jnp.broadcast_to(cand, (TQ, LANES))
        return count(lambda blk, off: (blk >= cb).astype(I32))

    t0 = jnp.where(count_ge(jnp.zeros((TQ, 1), I32)) >= n_sel, 0, INT_MIN).astype(I32)

    def bit_body(k, t):
        cand = t + lax.shift_left(jnp.int32(1), jnp.int32(30) - k)
        return jnp.where(count_ge(cand) >= n_sel, cand, t)

    t = lax.fori_loop(0, 31, bit_body, t0)
    tb = jnp.broadcast_to(t, (TQ, LANES))
    t_ref[...] = tb

    n_gt = count(lambda blk, off: (blk > tb).astype(I32))
    n_eq = count(lambda blk, off: (blk == tb).astype(I32))
    need = n_sel - n_gt
    j_ref[...] = jnp.full((TQ, LANES), seq, I32)
    ambiguous = jnp.max(jnp.where((n_eq > need) & (t > INT_MIN), 1.0, 0.0))

    @pl.when(ambiguous > 0)
    def _():
        n_bits = max(1, (seq - 1).bit_length())

        def ties_below(xcut):
            xb = jnp.broadcast_to(xcut, (TQ, LANES))
            return count(lambda blk, off: jnp.where(blk == tb, ((off + lane) < xb).astype(I32), 0))

        def jbit_body(k, x):
            cand = x + lax.shift_left(jnp.int32(1), jnp.int32(n_bits - 1) - k)
            return jnp.where(ties_below(cand) < need, cand, x)

        x = lax.fori_loop(0, n_bits, jbit_body, jnp.zeros((TQ, 1), I32))
        j_ref[...] = jnp.broadcast_to(jnp.where(n_eq > need, x, seq), (TQ, LANES))


def _dsa_select(qi_r, wi_r, kit, n_sel):
    b, nq = qi_r.shape[0], qi_r.shape[1]
    s = kit.shape[2]
    out = jax.ShapeDtypeStruct((b, s, LANES), I32)
    return pl.pallas_call(
        functools.partial(_dsa_select_kernel, n_sel=n_sel, seq=s),
        out_shape=(out, out),
        grid=(b, nq),
        in_specs=[
            pl.BlockSpec((None, None, IDX_HEADS * TQ, IDX_DIM), lambda bb, i: (bb, i, 0, 0)),
            pl.BlockSpec((None, None, IDX_HEADS * TQ, 1), lambda bb, i: (bb, i, 0, 0)),
            pl.BlockSpec((None, IDX_DIM, s), lambda bb, i: (bb, 0, 0)),
        ],
        out_specs=(
            pl.BlockSpec((None, TQ, LANES), lambda bb, i: (bb, i, 0)),
            pl.BlockSpec((None, TQ, LANES), lambda bb, i: (bb, i, 0)),
        ),
        scratch_shapes=[pltpu.VMEM((TQ, s), I32), pltpu.VMEM((IDX_HEADS * TQ, LANES), F32)],
        compiler_params=_cparams(("parallel", "arbitrary")),
        name="dsa_select",
    )(qi_r, wi_r, kit)


def _dsa_attend_kernel(qtab_ref, ktab_ref, q_ref, k_ref, v_ref, qi_ref, wi_ref, kit_ref, t_ref, j_ref,
                       o_ref, m_scr, l_scr, acc_scr, wb_scr):
    step = pl.program_id(1)
    i = qtab_ref[step]
    kt = ktab_ref[step]
    last_kt = (i * TQ + TQ - 1) // TK

    @pl.when(kt == 0)
    def _():
        m_scr[...] = jnp.full(m_scr.shape, NEG_BIG, F32)
        l_scr[...] = jnp.zeros(l_scr.shape, F32)
        acc_scr[...] = jnp.zeros(acc_scr.shape, F32)
        _fill_wb(wi_ref, wb_scr)

    keys = _index_keys(qi_ref[...], kit_ref[...], wb_scr, TK)
    tb = t_ref[...]
    jb = j_ref[...]
    q_pos = i * TQ + lax.broadcasted_iota(I32, (TQ, LANES), 0)
    lane = lax.broadcasted_iota(I32, (TQ, LANES), 1)
    bias = []
    for c, kc in enumerate(keys):
        pos = kt * TK + c * LANES + lane
        sel = jnp.where(kc > tb, 1, jnp.where(kc == tb, (pos <= jb).astype(I32), 0))
        sel = jnp.where(pos <= q_pos, sel, 0)
        bias.append(jnp.where(sel > 0, 0.0, NEG_BIG).astype(F32))

    scale = HEAD_DIM ** -0.5
    for h in range(ATTN_HEADS):
        hs = slice(h * HEAD_DIM, (h + 1) * HEAD_DIM)
        logits = lax.dot_general(q_ref[:, hs], k_ref[:, hs], (((1,), (1,)), ((), ())),
                                 preferred_element_type=F32)
        s = [logits[:, c * LANES:(c + 1) * LANES] * scale + bias[c] for c in range(TK // LANES)]
        mx = functools.reduce(jnp.maximum, s)
        m_old = m_scr[h]
        m_new = jnp.maximum(m_old, jnp.max(mx, axis=1, keepdims=True))
        alpha = jnp.exp(m_old - m_new)
        p = [jnp.exp(sc - m_new) for sc in s]
        row = jnp.sum(functools.reduce(lambda a, b: a + b, p), axis=1, keepdims=True)
        l_scr[h] = alpha * l_scr[h] + row
        m_scr[h] = m_new
        pb = jnp.concatenate(p, axis=1).astype(BF16)
        acc_scr[:, hs] = alpha * acc_scr[:, hs] + jnp.dot(pb, v_ref[:, hs], preferred_element_type=F32)

    @pl.when(kt == last_kt)
    def _():
        for h in range(ATTN_HEADS):
            hs = slice(h * HEAD_DIM, (h + 1) * HEAD_DIM)
            o_ref[:, hs] = (acc_scr[:, hs] / l_scr[h]).astype(o_ref.dtype)


def _dsa_attend(q, k, v, qi_r, wi_r, kit, tb, jb):
    b, s, _ = q.shape
    nq = s // TQ
    qtab, ktab = [], []
    for i in range(nq):
        for kt in range((i * TQ + TQ - 1) // TK + 1):
            qtab.append(i)
            ktab.append(kt)
    n_steps = len(qtab)
    grid_spec = pltpu.PrefetchScalarGridSpec(
        num_scalar_prefetch=2,
        grid=(b, n_steps),
        in_specs=[
            pl.BlockSpec((None, TQ, ATTN_WIDTH), lambda bb, st, qt, kk: (bb, qt[st], 0)),
            pl.BlockSpec((None, TK, ATTN_WIDTH), lambda bb, st, qt, kk: (bb, kk[st], 0)),
            pl.BlockSpec((None, TK, ATTN_WIDTH), lambda bb, st, qt, kk: (bb, kk[st], 0)),
            pl.BlockSpec((None, None, IDX_HEADS * TQ, IDX_DIM), lambda bb, st, qt, kk: (bb, qt[st], 0, 0)),
            pl.BlockSpec((None, None, IDX_HEADS * TQ, 1), lambda bb, st, qt, kk: (bb, qt[st], 0, 0)),
            pl.BlockSpec((None, IDX_DIM, TK), lambda bb, st, qt, kk: (bb, 0, kk[st])),
            pl.BlockSpec((None, TQ, LANES), lambda bb, st, qt, kk: (bb, qt[st], 0)),
            pl.BlockSpec((None, TQ, LANES), lambda bb, st, qt, kk: (bb, qt[st], 0)),
        ],
        out_specs=pl.BlockSpec((None, TQ, ATTN_WIDTH), lambda bb, st, qt, kk: (bb, qt[st], 0)),
        scratch_shapes=[
            pltpu.VMEM((ATTN_HEADS, TQ, LANES), F32),
            pltpu.VMEM((ATTN_HEADS, TQ, LANES), F32),
            pltpu.VMEM((TQ, ATTN_WIDTH), F32),
            pltpu.VMEM((IDX_HEADS * TQ, LANES), F32),
        ],
    )
    return pl.pallas_call(
        _dsa_attend_kernel,
        out_shape=jax.ShapeDtypeStruct((b, s, ATTN_WIDTH), BF16),
        grid_spec=grid_spec,
        compiler_params=_cparams(("parallel", "arbitrary")),
        name="dsa_attend",
    )(jnp.asarray(qtab, I32), jnp.asarray(ktab, I32), q, k, v, qi_r, wi_r, kit, tb, jb)


def _gmlp_mix_kernel(gu_ref, gv_ref, ga_ref, gb_ref, ya_ref, ng_ref, ws_ref, bst_ref, wa_ref, wb_ref,
                     o_ref, yb_scr):
    tt = gu_ref.shape[0]
    row = lax.broadcasted_iota(I32, (CHUNK, CHUNK), 0)
    col = lax.broadcasted_iota(I32, (CHUNK, CHUNK), 1)
    tril = col <= row
    for c in range(tt // CHUNK):
        rs = slice(c * CHUNK, (c + 1) * CHUNK)
        u = jax.nn.gelu(gu_ref[rs, :])
        v = jax.nn.gelu(gv_ref[rs, :])
        v = (v * lax.rsqrt(jnp.mean(v * v, axis=-1, keepdims=True) + EPS)) * ng_ref[...]
        vb = v.astype(BF16)
        for g in range(GMLP_GROUPS):
            gs = slice(g * CHUNK, (g + 1) * CHUNK)
            wm = jnp.where(tril, ws_ref[g], 0.0).astype(BF16)
            z = jnp.dot(wm, vb[:, gs], preferred_element_type=F32) + bst_ref[:, g:g + 1]
            yb_scr[rs, gs] = (u[:, gs] * z).astype(BF16)
    ma = jnp.dot(ya_ref[...], wa_ref[...], preferred_element_type=F32)
    mb = jnp.dot(yb_scr[...], wb_ref[...], preferred_element_type=F32)
    o_ref[...] = (jax.nn.sigmoid(ga_ref[...]) * ma + jax.nn.sigmoid(gb_ref[...]) * mb).astype(o_ref.dtype)


def _gmlp_mix(gates, ya, norm_g, w_s, b_s_t, wa_bf, wb_bf, d_model):
    t = ya.shape[0]
    tt = min(MIX_TT, t)
    assert d_model == 2 * GMLP_WIDTH
    return pl.pallas_call(
        _gmlp_mix_kernel,
        out_shape=jax.ShapeDtypeStruct((t, d_model), BF16),
        grid=(t // tt,),
        in_specs=[
            pl.BlockSpec((tt, GMLP_WIDTH), lambda i: (i, 0)),
            pl.BlockSpec((tt, GMLP_WIDTH), lambda i: (i, 1)),
            pl.BlockSpec((tt, d_model), lambda i: (i, 1)),
            pl.BlockSpec((tt, d_model), lambda i: (i, 2)),
            pl.BlockSpec((tt, ATTN_WIDTH), lambda i: (i, 0)),
            pl.BlockSpec((1, GMLP_WIDTH), lambda i: (0, 0)),
            pl.BlockSpec((GMLP_GROUPS, CHUNK, CHUNK), lambda i: (0, 0, 0)),
            pl.BlockSpec((CHUNK, GMLP_GROUPS), lambda i: (0, 0)),
            pl.BlockSpec((ATTN_WIDTH, d_model), lambda i: (0, 0)),
            pl.BlockSpec((GMLP_WIDTH, d_model), lambda i: (0, 0)),
        ],
        out_specs=pl.BlockSpec((tt, d_model), lambda i: (i, 0)),
        scratch_shapes=[pltpu.VMEM((tt, GMLP_WIDTH), BF16)],
        compiler_params=_cparams(("parallel",)),
        name="gmlp_mix",
    )(gates, gates, gates, gates, ya, norm_g.reshape(1, GMLP_WIDTH), w_s, b_s_t, wa_bf, wb_bf)


def _out_proj_kernel(x_ref, m_ref, wo_ref, g_ref, x1_ref, h2_ref):
    x1 = x_ref[...] + jnp.dot(m_ref[...], wo_ref[...], preferred_element_type=F32)
    x1_ref[...] = x1
    ms = jnp.mean(x1 * x1, axis=-1, keepdims=True)
    h2_ref[...] = ((x1 * lax.rsqrt(ms + EPS)) * g_ref[...]).astype(BF16)


def _out_proj(x2, mixed, wo_bf, g):
    t, d = x2.shape
    tt = min(MIX_TT, t)
    return pl.pallas_call(
        _out_proj_kernel,
        out_shape=(jax.ShapeDtypeStruct((t, d), F32), jax.ShapeDtypeStruct((t, d), BF16)),
        grid=(t // tt,),
        in_specs=[
            pl.BlockSpec((tt, d), lambda i: (i, 0)),
            pl.BlockSpec((tt, d), lambda i: (i, 0)),
            pl.BlockSpec((d, d), lambda i: (0, 0)),
            pl.BlockSpec((1, d), lambda i: (0, 0)),
        ],
        out_specs=(pl.BlockSpec((tt, d), lambda i: (i, 0)), pl.BlockSpec((tt, d), lambda i: (i, 0))),
        compiler_params=_cparams(("parallel",)),
        name="out_proj",
    )(x2, mixed, wo_bf, g.reshape(1, d))


def _peer_scores_kernel(h2_ref, wq_ref, sk_ref, st_ref):
    qp = jnp.dot(h2_ref[...], wq_ref[...], preferred_element_type=F32).astype(BF16)
    half = sk_ref.shape[2]
    for hp in range(2 * PEER_HEADS):
        st_ref[hp] = lax.dot_general(sk_ref[hp], qp[:, hp * half:(hp + 1) * half], (((1,), (1,)), ((), ())),
                                     preferred_element_type=F32)


def _peer_scores(h2, wq_bf, sk_bf):
    t, d = h2.shape
    tt = min(MIX_TT, t)
    nq = wq_bf.shape[1]
    half = sk_bf.shape[2]
    return pl.pallas_call(
        _peer_scores_kernel,
        out_shape=jax.ShapeDtypeStruct((2 * PEER_HEADS, N_KEYS, t), F32),
        grid=(t // tt,),
        in_specs=[
            pl.BlockSpec((tt, d), lambda i: (i, 0)),
            pl.BlockSpec((d, nq), lambda i: (0, 0)),
            pl.BlockSpec((2 * PEER_HEADS, N_KEYS, half), lambda i: (0, 0, 0)),
        ],
        out_specs=pl.BlockSpec((2 * PEER_HEADS, N_KEYS, tt), lambda i: (0, 0, i)),
        compiler_params=_cparams(("parallel",)),
        name="peer_scores",
    )(h2, wq_bf, sk_bf)


def _top_values(cur, k):
    n = cur.shape[0]
    idx = lax.broadcasted_iota(I32, cur.shape, 0).astype(F32)
    vals = []
    for _ in range(k):
        mx = jnp.max(cur, axis=0, keepdims=True)
        first = jnp.min(jnp.where(cur == mx, idx, float(n)), axis=0, keepdims=True)
        vals.append(mx)
        cur = jnp.where(idx == first, -jnp.inf, cur)
    return vals


def _peer_select_kernel(st_ref, th_ref, e1_ref, e2_ref):
    tl = st_ref.shape[2]

    def lane_group(g, carry):
        ls = pl.ds(pl.multiple_of(g * LANES, LANES), LANES)
        for h in range(PEER_HEADS):
            s1 = st_ref[2 * h, :, ls]
            s2 = st_ref[2 * h + 1, :, ls]
            tv1 = _top_values(s1, PEER_TOPK)
            tv2 = jnp.concatenate(_top_values(s2, PEER_TOPK), axis=0)
            cand = jnp.concatenate([tv1[a] + tv2 for a in range(PEER_TOPK)], axis=0)
            best = _top_values(cand, PEER_TOPK)
            m = best[0]
            z = functools.reduce(lambda a, b: a + b, [jnp.exp(bs - m) for bs in best])
            th_ref[h:h + 1, ls] = best[PEER_TOPK - 1]
            e1_ref[h, :, ls] = jnp.exp(s1 - tv1[0]) / z
            e2_ref[h, :, ls] = jnp.exp(s2 - tv2[0:1])
        return carry

    lax.fori_loop(0, tl // LANES, lane_group, 0)


def _peer_select(st):
    t = st.shape[2]
    tl = min(SEL_TL, t)
    e_shape = jax.ShapeDtypeStruct((PEER_HEADS, N_KEYS, t), F32)
    return pl.pallas_call(
        _peer_select_kernel,
        out_shape=(jax.ShapeDtypeStruct((PEER_HEADS, t), F32), e_shape, e_shape),
        grid=(t // tl,),
        in_specs=[pl.BlockSpec((2 * PEER_HEADS, N_KEYS, tl), lambda i: (0, 0, i))],
        out_specs=(
            pl.BlockSpec((PEER_HEADS, tl), lambda i: (0, i)),
            pl.BlockSpec((PEER_HEADS, N_KEYS, tl), lambda i: (0, 0, i)),
            pl.BlockSpec((PEER_HEADS, N_KEYS, tl), lambda i: (0, 0, i)),
        ),
        compiler_params=_cparams(("parallel",)),
        name="peer_select",
    )(st)


def _peer_dense_kernel(h2_ref, u_ref, vt_ref, st_ref, row_ref, e1row_ref, th_ref, e2_ref, o_ref,
                       act_scr, p_scr, acc_scr):
    ei = pl.program_id(1)
    te, tt = act_scr.shape

    @pl.when(ei == 0)
    def _():
        acc_scr[...] = jnp.zeros(acc_scr.shape, F32)

    act_scr[...] = lax.dot_general(u_ref[...], h2_ref[...], (((1,), (1,)), ((), ())),
                                   preferred_element_type=F32)
    for ii in range(PEER_ROWS):
        rs = slice(ii * N_KEYS, (ii + 1) * N_KEYS)
        for lc in range(tt // LANES):
            ls = slice(lc * LANES, (lc + 1) * LANES)
            coef = jnp.zeros((N_KEYS, LANES), F32)
            for h in range(PEER_HEADS):
                s1row = row_ref[2 * h, ii:ii + 1, ls]
                e1row = e1row_ref[h, ii:ii + 1, ls]
                keep = (st_ref[2 * h + 1, :, ls] + s1row) >= th_ref[h:h + 1, ls]
                coef = coef + jnp.where(keep, e2_ref[h, :, ls] * e1row, 0.0)
            p_scr[rs, ls] = (coef * jax.nn.gelu(act_scr[rs, ls])).astype(BF16)
    acc_scr[...] += jnp.dot(vt_ref[...], p_scr[...], preferred_element_type=F32)

    @pl.when(ei == pl.num_programs(1) - 1)
    def _():
        o_ref[...] = acc_scr[...].T


def _peer_dense(h2, u_bf, vt_bf, st, th, e1, e2):
    t, d = h2.shape
    n_exp = u_bf.shape[0]
    tt = min(PEER_TT, t)
    te = PEER_ROWS * N_KEYS
    tok3 = lambda i, e: (0, 0, i)
    return pl.pallas_call(
        _peer_dense_kernel,
        out_shape=jax.ShapeDtypeStruct((t, d), F32),
        grid=(t // tt, n_exp // te),
        in_specs=[
            pl.BlockSpec((tt, d), lambda i, e: (i, 0)),
            pl.BlockSpec((te, d), lambda i, e: (e, 0)),
            pl.BlockSpec((d, te), lambda i, e: (0, e)),
            pl.BlockSpec((2 * PEER_HEADS, N_KEYS, tt), tok3),
            pl.BlockSpec((2 * PEER_HEADS, PEER_ROWS, tt), lambda i, e: (0, e, i)),
            pl.BlockSpec((PEER_HEADS, PEER_ROWS, tt), lambda i, e: (0, e, i)),
            pl.BlockSpec((PEER_HEADS, tt), lambda i, e: (0, i)),
            pl.BlockSpec((PEER_HEADS, N_KEYS, tt), tok3),
        ],
        out_specs=pl.BlockSpec((tt, d), lambda i, e: (i, 0)),
        scratch_shapes=[
            pltpu.VMEM((te, tt), F32),
            pltpu.VMEM((te, tt), BF16),
            pltpu.VMEM((d, tt), F32),
        ],
        compiler_params=_cparams(("parallel", "arbitrary")),
        name="peer_dense",
    )(h2, u_bf, vt_bf, st, st, e1, th, e2)


def _final_norm_kernel(x1_ref, p_ref, g_ref, o_ref):
    y = x1_ref[...] + p_ref[...]
    ms = jnp.mean(y * y, axis=-1, keepdims=True)
    o_ref[...] = (y * lax.rsqrt(ms + EPS)) * g_ref[...]


def _final_norm(x1, peer_out, g):
    t, d = x1.shape
    tt = min(PROJ_TM, t)
    return pl.pallas_call(
        _final_norm_kernel,
        out_shape=jax.ShapeDtypeStruct((t, d), F32),
        grid=(t // tt,),
        in_specs=[
            pl.BlockSpec((tt, d), lambda i: (i, 0)),
            pl.BlockSpec((tt, d), lambda i: (i, 0)),
            pl.BlockSpec((1, d), lambda i: (0, 0)),
        ],
        out_specs=pl.BlockSpec((tt, d), lambda i: (i, 0)),
        compiler_params=_cparams(("parallel",)),
        name="final_norm",
    )(x1, peer_out, g.reshape(1, d))


def _layer(x2, b, s, ln_mix_g, w_in, gmlp_norm_g, w_spatial, b_spatial, w_branch_attn, w_branch_gmlp,
           w_out, ln_ffn_g, peer_w_q, peer_sub_keys, peer_u, peer_v):
    t, d = x2.shape
    n_qi = IDX_HEADS * IDX_DIM
    o_qi = 3 * ATTN_WIDTH
    o_ki = o_qi + n_qi
    o_wi = o_ki + IDX_DIM
    o_gate = o_wi + IDX_HEADS

    w_attn = w_in[:, :o_ki].astype(BF16)
    w_idx = jnp.pad(w_in[:, o_ki:o_gate], ((0, 0), (0, LANES - IDX_DIM - IDX_HEADS))).astype(BF16)
    w_gate = w_in[:, o_gate:].astype(BF16)

    attn_in = _norm_proj(x2, ln_mix_g, w_attn, BF16, PROJ_TN)
    idx_in = _norm_proj(x2, ln_mix_g, w_idx, F32, LANES)
    gates = _norm_proj(x2, ln_mix_g, w_gate, F32, PROJ_TN)

    q = attn_in[:, :ATTN_WIDTH].reshape(b, s, ATTN_WIDTH)
    k = attn_in[:, ATTN_WIDTH:2 * ATTN_WIDTH].reshape(b, s, ATTN_WIDTH)
    v = attn_in[:, 2 * ATTN_WIDTH:o_qi].reshape(b, s, ATTN_WIDTH)
    nq = s // TQ
    qi_r = attn_in[:, o_qi:o_ki].reshape(b, nq, TQ, IDX_HEADS, IDX_DIM)
    qi_r = qi_r.transpose(0, 1, 3, 2, 4).reshape(b, nq, IDX_HEADS * TQ, IDX_DIM)
    kit = idx_in[:, :IDX_DIM].astype(BF16).reshape(b, s, IDX_DIM).transpose(0, 2, 1)
    wi_r = idx_in[:, IDX_DIM:IDX_DIM + IDX_HEADS].reshape(b, nq, TQ, IDX_HEADS)
    wi_r = wi_r.transpose(0, 1, 3, 2).reshape(b, nq, IDX_HEADS * TQ, 1)

    n_sel = min(DSA_TOPK, s // 4)
    tb, jb = _dsa_select(qi_r, wi_r, kit, n_sel)
    ya = _dsa_attend(q, k, v, qi_r, wi_r, kit, tb, jb).reshape(t, ATTN_WIDTH)

    mixed = _gmlp_mix(gates, ya, gmlp_norm_g, w_spatial, b_spatial.T, w_branch_attn.astype(BF16),
                      w_branch_gmlp.astype(BF16), d)
    x1, h2 = _out_proj(x2, mixed, w_out.astype(BF16), ln_ffn_g)

    half = peer_sub_keys.shape[-1]
    sk = peer_sub_keys.reshape(2 * PEER_HEADS, N_KEYS, half).astype(BF16)
    st = _peer_scores(h2, peer_w_q.astype(BF16), sk)
    th, e1, e2 = _peer_select(st)
    peer_out = _peer_dense(h2, peer_u.astype(BF16), peer_v.astype(BF16).T, st, th, e1, e2)
    return x1, peer_out


def kernel(x, ln_mix_g, w_in, gmlp_norm_g, w_spatial, b_spatial, w_branch_attn, w_branch_gmlp, w_out, ln_ffn_g, peer_w_q, peer_sub_keys, peer_u, peer_v, ln_final_g):
    b, s, d = x.shape
    depth = w_in.shape[0]
    x2 = x.reshape(b * s, d)
    for l in range(depth):
        x1, peer_out = _layer(x2, b, s, ln_mix_g[l], w_in[l], gmlp_norm_g[l], w_spatial[l], b_spatial[l],
                              w_branch_attn[l], w_branch_gmlp[l], w_out[l], ln_ffn_g[l], peer_w_q[l],
                              peer_sub_keys[l], peer_u[l], peer_v[l])
        if l + 1 < depth:
            x2 = x1 + peer_out
    return _final_norm(x1, peer_out, ln_final_g).reshape(b, s, d)
```

```python
import functools
import math

import numpy as np
import jax
import jax.numpy as jnp
from jax import lax
from jax.experimental import pallas as pl
from jax.experimental.pallas import tpu as pltpu

F32 = jnp.float32
BF16 = jnp.bfloat16
I32 = jnp.int32

ATTN_HEADS = 8
HEAD_DIM = 128
ATTN_WIDTH = ATTN_HEADS * HEAD_DIM
IDX_HEADS = 8
IDX_DIM = 64
DSA_TOPK = 256
GMLP_GROUPS = 8
CHUNK = 128
GMLP_WIDTH = GMLP_GROUPS * CHUNK
N_KEYS = 128
PEER_HEADS = 8
PEER_TOPK = 16
EPS = 1e-6

LANES = 128
VMEM_LIMIT_BYTES = 56 * 1024 * 1024

INT_MIN = -(2 ** 31)
NEG_BIG = -1e30

TQ = 128
TK = 512
ATQ = 512
ATK = 256
PROJ_TM = 512
PROJ_TN = 512
MIX_TT = 256
PEER_TT = 512
PEER_ROWS = 8
SEL_TL = 512


def _cparams(sem):
    return pltpu.CompilerParams(dimension_semantics=sem, vmem_limit_bytes=VMEM_LIMIT_BYTES)


def _norm_proj_kernel(x_ref, g_ref, w_ref, o_ref, h_scr):
    @pl.when(pl.program_id(1) == 0)
    def _():
        x = x_ref[...]
        ms = jnp.mean(x * x, axis=-1, keepdims=True)
        h_scr[...] = ((x * lax.rsqrt(ms + EPS)) * g_ref[...]).astype(BF16)

    o_ref[...] = jnp.dot(h_scr[...], w_ref[...], preferred_element_type=F32).astype(o_ref.dtype)


def _norm_proj(x2, g, w_bf, out_dtype, tn):
    t, d = x2.shape
    n = w_bf.shape[1]
    tm = min(PROJ_TM, t)
    return pl.pallas_call(
        _norm_proj_kernel,
        out_shape=jax.ShapeDtypeStruct((t, n), out_dtype),
        grid=(t // tm, n // tn),
        in_specs=[
            pl.BlockSpec((tm, d), lambda i, j: (i, 0)),
            pl.BlockSpec((1, d), lambda i, j: (0, 0)),
            pl.BlockSpec((d, tn), lambda i, j: (0, j)),
        ],
        out_specs=pl.BlockSpec((tm, tn), lambda i, j: (i, j)),
        scratch_shapes=[pltpu.VMEM((tm, d), BF16)],
        compiler_params=_cparams(("parallel", "arbitrary")),
        name="norm_proj",
    )(x2, g.reshape(1, d), w_bf)


def _sort_key(x):
    bits = pltpu.bitcast(x, I32)
    return bits ^ ((bits >> 31) & 0x7FFFFFFF)


def _dsa_select_kernel(qi_ref, wi_ref, kit_ref, bias_ref, keys_scr, wb_scr, j_scr, *, n_sel, seq):
    i = pl.program_id(1)
    n_kt = (i * TQ + TQ + TK - 1) // TK
    wb_scr[...] = jnp.broadcast_to((wi_ref[...] * (IDX_HEADS ** -0.5)) * (IDX_DIM ** -0.5), wb_scr.shape)
    qi = qi_ref[...]
    q_pos = i * TQ + lax.broadcasted_iota(I32, (TQ, LANES), 0)
    lane = lax.broadcasted_iota(I32, (TQ, LANES), 1)

    def chunk_at(off):
        return pl.ds(pl.multiple_of(off, LANES), LANES)

    def score_body(kt, carry):
        m1, m2 = carry
        off = pl.multiple_of(kt * TK, TK)
        dots = jnp.maximum(jnp.dot(qi, kit_ref[:, pl.ds(off, TK)], preferred_element_type=F32), 0.0)
        for c in range(TK // LANES):
            acc = None
            for h in range(IDX_HEADS):
                term = dots[h * TQ:(h + 1) * TQ, c * LANES:(c + 1) * LANES] * wb_scr[h * TQ:(h + 1) * TQ, :]
                acc = term if acc is None else acc + term
            sc = acc + 0.0
            causal = (off + c * LANES + lane) <= q_pos
            scm = jnp.where(causal, sc, -jnp.inf)
            m2 = jnp.maximum(m2, jnp.minimum(m1, scm))
            m1 = jnp.maximum(m1, scm)
            keys_scr[:, chunk_at(off + c * LANES)] = jnp.where(causal, _sort_key(sc), INT_MIN)
        return m1, m2

    neg_inf = jnp.full((TQ, LANES), -jnp.inf, F32)
    m1, m2 = lax.fori_loop(0, n_kt, score_body, (neg_inf, neg_inf))

    def count(preds):
        def body(kt, accs):
            off = pl.multiple_of(kt * TK, TK)
            for c in range(TK // LANES):
                blk = keys_scr[:, chunk_at(off + c * LANES)]
                accs = tuple(a + p(blk, off + c * LANES) for a, p in zip(accs, preds))
            return accs
        accs = lax.fori_loop(0, n_kt, body, tuple(jnp.zeros((TQ, LANES), I32) for _ in preds))
        return [jnp.sum(a.astype(F32), axis=1, keepdims=True).astype(I32) for a in accs]

    low_f = jnp.min(m2, axis=1, keepdims=True)
    low_key = _sort_key(jnp.broadcast_to(low_f, (TQ, LANES)))[:, :1]
    lo0 = jnp.where(low_f == -jnp.inf, INT_MIN, low_key)
    hi0 = _sort_key(jnp.broadcast_to(jnp.max(m1, axis=1, keepdims=True), (TQ, LANES)))[:, :1] + 1

    def bisect_body(state):
        lo, hi, _ = state
        mid = (lo >> 1) + (hi >> 1) + (lo & hi & 1)
        mb = jnp.broadcast_to(mid, (TQ, LANES))
        c, = count([lambda blk, off: (blk >= mb).astype(I32)])
        ge = c >= n_sel
        lo_n = jnp.where(ge, mid, lo)
        hi_n = jnp.where(c == n_sel, mid + 1, jnp.where(ge, hi, mid))
        return lo_n, hi_n, jnp.max(jnp.where(hi_n != lo_n + 1, 1.0, 0.0)).astype(F32)

    t, _, _ = lax.while_loop(lambda st: st[2] > 0.0, bisect_body, (lo0, hi0, jnp.float32(1.0)))
    tb = jnp.broadcast_to(t, (TQ, LANES))

    n_gt, n_eq = count([lambda blk, off: (blk > tb).astype(I32), lambda blk, off: (blk == tb).astype(I32)])
    need = n_sel - n_gt
    j_scr[...] = jnp.full((TQ, LANES), seq, I32)
    ambiguous = jnp.max(jnp.where((n_eq > need) & (t > INT_MIN), 1.0, 0.0))

    @pl.when(ambiguous > 0)
    def _():
        n_bits = max(1, (seq - 1).bit_length())

        def jbit_body(k, x):
            cand = x + lax.shift_left(jnp.int32(1), jnp.int32(n_bits - 1) - k)
            xb = jnp.broadcast_to(cand, (TQ, LANES))
            below, = count([lambda blk, off: jnp.where(blk == tb, ((off + lane) < xb).astype(I32), 0)])
            return jnp.where(below < need, cand, x)

        x = lax.fori_loop(0, n_bits, jbit_body, jnp.zeros((TQ, 1), I32))
        j_scr[...] = jnp.broadcast_to(jnp.where(n_eq > need, x, seq), (TQ, LANES))

    jb = j_scr[...]

    def bias_body(kt, carry):
        off = pl.multiple_of(kt * TK, TK)
        for c in range(TK // LANES):
            pos = off + c * LANES + lane
            blk = keys_scr[:, chunk_at(off + c * LANES)]
            keep = jnp.where(blk > tb, 1, jnp.where(blk == tb, (pos <= jb).astype(I32), 0))
            keep = jnp.where(pos <= q_pos, keep, 0)
            bias_ref[:, chunk_at(off + c * LANES)] = jnp.where(keep > 0, 0.0, NEG_BIG).astype(BF16)
        return carry

    lax.fori_loop(0, n_kt, bias_body, 0)

    def fill_body(kt, carry):
        bias_ref[:, pl.ds(pl.multiple_of(kt * TK, TK), TK)] = jnp.full((TQ, TK), NEG_BIG, BF16)
        return carry

    lax.fori_loop(n_kt, seq // TK, fill_body, 0)


def _dsa_select(qi_r, wi_r, kit, n_sel):
    b, nq = qi_r.shape[0], qi_r.shape[1]
    s = kit.shape[2]
    assert n_sel <= 2 * LANES and s % TK == 0
    return pl.pallas_call(
        functools.partial(_dsa_select_kernel, n_sel=n_sel, seq=s),
        out_shape=jax.ShapeDtypeStruct((b, s, s), BF16),
        grid=(b, nq),
        in_specs=[
            pl.BlockSpec((None, None, IDX_HEADS * TQ, IDX_DIM), lambda bb, i: (bb, i, 0, 0)),
            pl.BlockSpec((None, None, IDX_HEADS * TQ, 1), lambda bb, i: (bb, i, 0, 0)),
            pl.BlockSpec((None, IDX_DIM, s), lambda bb, i: (bb, 0, 0)),
        ],
        out_specs=pl.BlockSpec((None, TQ, s), lambda bb, i: (bb, i, 0)),
        scratch_shapes=[
            pltpu.VMEM((TQ, s), I32),
            pltpu.VMEM((IDX_HEADS * TQ, LANES), F32),
            pltpu.VMEM((TQ, LANES), I32),
        ],
        compiler_params=_cparams(("parallel", "arbitrary")),
        name="dsa_select",
    )(qi_r, wi_r, kit)


def _dsa_attend_kernel(qtab_ref, ktab_ref, q_ref, k_ref, v_ref, bias_ref, o_ref, m_scr, acc_scr):
    step = pl.program_id(1)
    i = qtab_ref[step]
    kt = ktab_ref[step]
    last_kt = (i * ATQ + ATQ - 1) // ATK

    @pl.when(kt == 0)
    def _():
        m_scr[...] = jnp.full(m_scr.shape, NEG_BIG, F32)
        acc_scr[...] = jnp.zeros(acc_scr.shape, F32)

    bias = bias_ref[...].astype(F32)
    log2e_scale = (HEAD_DIM ** -0.5) * math.log2(math.e)
    ones = jnp.ones((ATK, HEAD_DIM), BF16)
    for h in range(ATTN_HEADS):
        hs = slice(h * HEAD_DIM, (h + 1) * HEAD_DIM)
        logits = lax.dot_general(q_ref[:, hs], k_ref[:, hs], (((1,), (1,)), ((), ())),
                                 preferred_element_type=F32)
        s = logits * log2e_scale + bias
        m_old = m_scr[h]
        m_new = jnp.maximum(m_old, jnp.max(s, axis=1, keepdims=True))
        alpha = jnp.exp2(m_old - m_new)
        p = jnp.concatenate([jnp.exp2(s[:, c * LANES:(c + 1) * LANES] - m_new) for c in range(ATK // LANES)],
                            axis=1).astype(BF16)
        v_ext = jnp.concatenate([v_ref[:, hs], ones], axis=1)
        pv = jnp.dot(p, v_ext, preferred_element_type=F32)
        acc_scr[h] = jnp.concatenate([alpha, alpha], axis=1) * acc_scr[h] + pv
        m_scr[h] = m_new

    @pl.when(kt == last_kt)
    def _():
        for h in range(ATTN_HEADS):
            acc = acc_scr[h]
            o_ref[:, h * HEAD_DIM:(h + 1) * HEAD_DIM] = (acc[:, :HEAD_DIM] / acc[:, HEAD_DIM:]).astype(o_ref.dtype)


def _dsa_attend(q, k, v, bias):
    b, s, _ = q.shape
    atq = min(ATQ, s)
    assert atq == ATQ and s % ATQ == 0
    qtab, ktab = [], []
    for i in range(s // ATQ):
        for kt in range((i * ATQ + ATQ - 1) // ATK + 1):
            qtab.append(i)
            ktab.append(kt)
    grid_spec = pltpu.PrefetchScalarGridSpec(
        num_scalar_prefetch=2,
        grid=(b, len(qtab)),
        in_specs=[
            pl.BlockSpec((None, ATQ, ATTN_WIDTH), lambda bb, st, qt, kk: (bb, qt[st], 0)),
            pl.BlockSpec((None, ATK, ATTN_WIDTH), lambda bb, st, qt, kk: (bb, kk[st], 0)),
            pl.BlockSpec((None, ATK, ATTN_WIDTH), lambda bb, st, qt, kk: (bb, kk[st], 0)),
            pl.BlockSpec((None, ATQ, ATK), lambda bb, st, qt, kk: (bb, qt[st], kk[st])),
        ],
        out_specs=pl.BlockSpec((None, ATQ, ATTN_WIDTH), lambda bb, st, qt, kk: (bb, qt[st], 0)),
        scratch_shapes=[
            pltpu.VMEM((ATTN_HEADS, ATQ, LANES), F32),
            pltpu.VMEM((ATTN_HEADS, ATQ, 2 * HEAD_DIM), F32),
        ],
    )
    return pl.pallas_call(
        _dsa_attend_kernel,
        out_shape=jax.ShapeDtypeStruct((b, s, ATTN_WIDTH), BF16),
        grid_spec=grid_spec,
        compiler_params=_cparams(("parallel", "arbitrary")),
        name="dsa_attend",
    )(jnp.asarray(qtab, I32), jnp.asarray(ktab, I32), q, k, v, bias)


def _gmlp_mix_kernel(gu_ref, gv_ref, ga_ref, gb_ref, ya_ref, ng_ref, ws_ref, bst_ref, wa_ref, wb_ref,
                     o_ref, yb_scr):
    tt = gu_ref.shape[0]
    row = lax.broadcasted_iota(I32, (CHUNK, CHUNK), 0)
    col = lax.broadcasted_iota(I32, (CHUNK, CHUNK), 1)
    tril = col <= row
    for c in range(tt // CHUNK):
        rs = slice(c * CHUNK, (c + 1) * CHUNK)
        u = jax.nn.gelu(gu_ref[rs, :])
        v = jax.nn.gelu(gv_ref[rs, :])
        v = (v * lax.rsqrt(jnp.mean(v * v, axis=-1, keepdims=True) + EPS)) * ng_ref[...]
        vb = v.astype(BF16)
        for g in range(GMLP_GROUPS):
            gs = slice(g * CHUNK, (g + 1) * CHUNK)
            wm = jnp.where(tril, ws_ref[g], 0.0).astype(BF16)
            z = jnp.dot(wm, vb[:, gs], preferred_element_type=F32) + bst_ref[:, g:g + 1]
            yb_scr[rs, gs] = (u[:, gs] * z).astype(BF16)
    ma = jnp.dot(ya_ref[...], wa_ref[...], preferred_element_type=F32)
    mb = jnp.dot(yb_scr[...], wb_ref[...], preferred_element_type=F32)
    o_ref[...] = (jax.nn.sigmoid(ga_ref[...]) * ma + jax.nn.sigmoid(gb_ref[...]) * mb).astype(o_ref.dtype)


def _gmlp_mix(gates, ya, norm_g, w_s, b_s_t, wa_bf, wb_bf, d_model):
    t = ya.shape[0]
    tt = min(MIX_TT, t)
    assert d_model == 2 * GMLP_WIDTH
    return pl.pallas_call(
        _gmlp_mix_kernel,
        out_shape=jax.ShapeDtypeStruct((t, d_model), BF16),
        grid=(t // tt,),
        in_specs=[
            pl.BlockSpec((tt, GMLP_WIDTH), lambda i: (i, 0)),
            pl.BlockSpec((tt, GMLP_WIDTH), lambda i: (i, 1)),
            pl.BlockSpec((tt, d_model), lambda i: (i, 1)),
            pl.BlockSpec((tt, d_model), lambda i: (i, 2)),
            pl.BlockSpec((tt, ATTN_WIDTH), lambda i: (i, 0)),
            pl.BlockSpec((1, GMLP_WIDTH), lambda i: (0, 0)),
            pl.BlockSpec((GMLP_GROUPS, CHUNK, CHUNK), lambda i: (0, 0, 0)),
            pl.BlockSpec((CHUNK, GMLP_GROUPS), lambda i: (0, 0)),
            pl.BlockSpec((ATTN_WIDTH, d_model), lambda i: (0, 0)),
            pl.BlockSpec((GMLP_WIDTH, d_model), lambda i: (0, 0)),
        ],
        out_specs=pl.BlockSpec((tt, d_model), lambda i: (i, 0)),
        scratch_shapes=[pltpu.VMEM((tt, GMLP_WIDTH), BF16)],
        compiler_params=_cparams(("parallel",)),
        name="gmlp_mix",
    )(gates, gates, gates, gates, ya, norm_g.reshape(1, GMLP_WIDTH), w_s, b_s_t, wa_bf, wb_bf)


def _out_proj_kernel(x_ref, m_ref, wo_ref, g_ref, x1_ref, h2_ref):
    x1 = x_ref[...] + jnp.dot(m_ref[...], wo_ref[...], preferred_element_type=F32)
    x1_ref[...] = x1
    ms = jnp.mean(x1 * x1, axis=-1, keepdims=True)
    h2_ref[...] = ((x1 * lax.rsqrt(ms + EPS)) * g_ref[...]).astype(BF16)


def _out_proj(x2, mixed, wo_bf, g):
    t, d = x2.shape
    tt = min(MIX_TT, t)
    return pl.pallas_call(
        _out_proj_kernel,
        out_shape=(jax.ShapeDtypeStruct((t, d), F32), jax.ShapeDtypeStruct((t, d), BF16)),
        grid=(t // tt,),
        in_specs=[
            pl.BlockSpec((tt, d), lambda i: (i, 0)),
            pl.BlockSpec((tt, d), lambda i: (i, 0)),
            pl.BlockSpec((d, d), lambda i: (0, 0)),
            pl.BlockSpec((1, d), lambda i: (0, 0)),
        ],
        out_specs=(pl.BlockSpec((tt, d), lambda i: (i, 0)), pl.BlockSpec((tt, d), lambda i: (i, 0))),
        compiler_params=_cparams(("parallel",)),
        name="out_proj",
    )(x2, mixed, wo_bf, g.reshape(1, d))


def _peer_scores_kernel(h2_ref, wq_ref, sk_ref, st_ref):
    qp = jnp.dot(h2_ref[...], wq_ref[...], preferred_element_type=F32).astype(BF16)
    half = sk_ref.shape[2]
    for hp in range(2 * PEER_HEADS):
        st_ref[hp] = lax.dot_general(sk_ref[hp], qp[:, hp * half:(hp + 1) * half], (((1,), (1,)), ((), ())),
                                     preferred_element_type=F32)


def _peer_scores(h2, wq_bf, sk_bf):
    t, d = h2.shape
    tt = min(MIX_TT, t)
    nq = wq_bf.shape[1]
    half = sk_bf.shape[2]
    return pl.pallas_call(
        _peer_scores_kernel,
        out_shape=jax.ShapeDtypeStruct((2 * PEER_HEADS, N_KEYS, t), F32),
        grid=(t // tt,),
        in_specs=[
            pl.BlockSpec((tt, d), lambda i: (i, 0)),
            pl.BlockSpec((d, nq), lambda i: (0, 0)),
            pl.BlockSpec((2 * PEER_HEADS, N_KEYS, half), lambda i: (0, 0, 0)),
        ],
        out_specs=pl.BlockSpec((2 * PEER_HEADS, N_KEYS, tt), lambda i: (0, 0, i)),
        compiler_params=_cparams(("parallel",)),
        name="peer_scores",
    )(h2, wq_bf, sk_bf)


def _top_values(cur, k):
    n = cur.shape[0]
    idx = lax.broadcasted_iota(I32, cur.shape, 0).astype(F32)
    vals = []
    for _ in range(k):
        mx = jnp.max(cur, axis=0, keepdims=True)
        first = jnp.min(jnp.where(cur == mx, idx, float(n)), axis=0, keepdims=True)
        vals.append(mx)
        cur = jnp.where(idx == first, -jnp.inf, cur)
    return vals


def _staircase_sums(t1, t2):
    a1 = jnp.concatenate(t1, axis=0)
    a2 = jnp.concatenate(t2, axis=0)
    r16 = lax.broadcasted_iota(I32, (PEER_TOPK, LANES), 0)
    r8 = r16[:8]
    ninf = -jnp.inf
    return jnp.concatenate([
        t1[0] + a2,
        t1[1] + a2[:8],
        jnp.where(r16 >= 2, a1 + t2[0], ninf),
        jnp.where(r8 >= 2, a1[:8] + t2[1], ninf),
        jnp.where((r8 >= 2) & (r8 <= 4), t1[2] + a2[:8], ninf),
        jnp.where((r8 >= 2) & (r8 <= 3), t1[3] + a2[:8], ninf),
        jnp.where(r8 == 2, t1[4] + a2[:8], ninf),
    ], axis=0)


def _peer_select_kernel(st_ref, th_ref, e1_ref, e2_ref):
    tl = st_ref.shape[2]

    def lane_group(g, carry):
        ls = pl.ds(pl.multiple_of(g * LANES, LANES), LANES)
        for h in range(PEER_HEADS):
            s1 = st_ref[2 * h, :, ls]
            s2 = st_ref[2 * h + 1, :, ls]
            t1 = _top_values(s1, PEER_TOPK)
            t2 = _top_values(s2, PEER_TOPK)
            best = _top_values(_staircase_sums(t1, t2), PEER_TOPK)
            m = best[0]
            z = functools.reduce(lambda a, b: a + b, [jnp.exp(bs - m) for bs in best])
            th_ref[h:h + 1, ls] = best[PEER_TOPK - 1]
            e1_ref[h, :, ls] = jnp.exp(s1 - t1[0]) / z
            e2_ref[h, :, ls] = jnp.exp(s2 - t2[0])
        return carry

    lax.fori_loop(0, tl // LANES, lane_group, 0)


def _peer_select(st):
    t = st.shape[2]
    tl = min(SEL_TL, t)
    e_shape = jax.ShapeDtypeStruct((PEER_HEADS, N_KEYS, t), F32)
    return pl.pallas_call(
        _peer_select_kernel,
        out_shape=(jax.ShapeDtypeStruct((PEER_HEADS, t), F32), e_shape, e_shape),
        grid=(t // tl,),
        in_specs=[pl.BlockSpec((2 * PEER_HEADS, N_KEYS, tl), lambda i: (0, 0, i))],
        out_specs=(
            pl.BlockSpec((PEER_HEADS, tl), lambda i: (0, i)),
            pl.BlockSpec((PEER_HEADS, N_KEYS, tl), lambda i: (0, 0, i)),
            pl.BlockSpec((PEER_HEADS, N_KEYS, tl), lambda i: (0, 0, i)),
        ),
        compiler_params=_cparams(("parallel",)),
        name="peer_select",
    )(st)


def _peer_dense_kernel(h2_ref, u_ref, vt_ref, st_ref, row_ref, e1row_ref, th_ref, e2_ref, o_ref,
                       act_scr, p_scr, acc_scr):
    ei = pl.program_id(1)
    te, tt = act_scr.shape

    @pl.when(ei == 0)
    def _():
        acc_scr[...] = jnp.zeros(acc_scr.shape, F32)

    act_scr[...] = lax.dot_general(u_ref[...], h2_ref[...], (((1,), (1,)), ((), ())),
                                   preferred_element_type=F32)
    for ii in range(PEER_ROWS):
        rs = slice(ii * N_KEYS, (ii + 1) * N_KEYS)
        for lc in range(tt // LANES):
            ls = slice(lc * LANES, (lc + 1) * LANES)
            coef = jnp.zeros((N_KEYS, LANES), F32)
            for h in range(PEER_HEADS):
                s1row = row_ref[2 * h, ii:ii + 1, ls]
                e1row = e1row_ref[h, ii:ii + 1, ls]
                keep = (st_ref[2 * h + 1, :, ls] + s1row) >= th_ref[h:h + 1, ls]
                coef = coef + jnp.where(keep, e2_ref[h, :, ls] * e1row, 0.0)
            p_scr[rs, ls] = (coef * jax.nn.gelu(act_scr[rs, ls])).astype(BF16)
    acc_scr[...] += jnp.dot(vt_ref[...], p_scr[...], preferred_element_type=F32)

    @pl.when(ei == pl.num_programs(1) - 1)
    def _():
        o_ref[...] = acc_scr[...].T


def _peer_dense(h2, u_bf, vt_bf, st, th, e1, e2):
    t, d = h2.shape
    n_exp = u_bf.shape[0]
    tt = min(PEER_TT, t)
    te = PEER_ROWS * N_KEYS
    tok3 = lambda i, e: (0, 0, i)
    return pl.pallas_call(
        _peer_dense_kernel,
        out_shape=jax.ShapeDtypeStruct((t, d), F32),
        grid=(t // tt, n_exp // te),
        in_specs=[
            pl.BlockSpec((tt, d), lambda i, e: (i, 0)),
            pl.BlockSpec((te, d), lambda i, e: (e, 0)),
            pl.BlockSpec((d, te), lambda i, e: (0, e)),
            pl.BlockSpec((2 * PEER_HEADS, N_KEYS, tt), tok3),
            pl.BlockSpec((2 * PEER_HEADS, PEER_ROWS, tt), lambda i, e: (0, e, i)),
            pl.BlockSpec((PEER_HEADS, PEER_ROWS, tt), lambda i, e: (0, e, i)),
            pl.BlockSpec((PEER_HEADS, tt), lambda i, e: (0, i)),
            pl.BlockSpec((PEER_HEADS, N_KEYS, tt), tok3),
        ],
        out_specs=pl.BlockSpec((tt, d), lambda i, e: (i, 0)),
        scratch_shapes=[
            pltpu.VMEM((te, tt), F32),
            pltpu.VMEM((te, tt), BF16),
            pltpu.VMEM((d, tt), F32),
        ],
        compiler_params=_cparams(("parallel", "arbitrary")),
        name="peer_dense",
    )(h2, u_bf, vt_bf, st, st, e1, th, e2)


def _final_norm_kernel(x1_ref, p_ref, g_ref, o_ref):
    y = x1_ref[...] + p_ref[...]
    ms = jnp.mean(y * y, axis=-1, keepdims=True)
    o_ref[...] = (y * lax.rsqrt(ms + EPS)) * g_ref[...]


def _final_norm(x1, peer_out, g):
    t, d = x1.shape
    tt = min(PROJ_TM, t)
    return pl.pallas_call(
        _final_norm_kernel,
        out_shape=jax.ShapeDtypeStruct((t, d), F32),
        grid=(t // tt,),
        in_specs=[
            pl.BlockSpec((tt, d), lambda i: (i, 0)),
            pl.BlockSpec((tt, d), lambda i: (i, 0)),
            pl.BlockSpec((1, d), lambda i: (0, 0)),
        ],
        out_specs=pl.BlockSpec((tt, d), lambda i: (i, 0)),
        compiler_params=_cparams(("parallel",)),
        name="final_norm",
    )(x1, peer_out, g.reshape(1, d))


def _layer(x2, b, s, ln_mix_g, w_in, gmlp_norm_g, w_spatial, b_spatial, w_branch_attn, w_branch_gmlp,
           w_out, ln_ffn_g, peer_w_q, peer_sub_keys, peer_u, peer_v):
    t, d = x2.shape
    n_qi = IDX_HEADS * IDX_DIM
    o_qi = 3 * ATTN_WIDTH
    o_ki = o_qi + n_qi
    o_wi = o_ki + IDX_DIM
    o_gate = o_wi + IDX_HEADS

    w_attn = w_in[:, :o_ki].astype(BF16)
    w_idx = jnp.pad(w_in[:, o_ki:o_gate], ((0, 0), (0, LANES - IDX_DIM - IDX_HEADS))).astype(BF16)
    w_gate = w_in[:, o_gate:].astype(BF16)

    attn_in = _norm_proj(x2, ln_mix_g, w_attn, BF16, PROJ_TN)
    idx_in = _norm_proj(x2, ln_mix_g, w_idx, F32, LANES)
    gates = _norm_proj(x2, ln_mix_g, w_gate, F32, PROJ_TN)

    q = attn_in[:, :ATTN_WIDTH].reshape(b, s, ATTN_WIDTH)
    k = attn_in[:, ATTN_WIDTH:2 * ATTN_WIDTH].reshape(b, s, ATTN_WIDTH)
    v = attn_in[:, 2 * ATTN_WIDTH:o_qi].reshape(b, s, ATTN_WIDTH)
    nq = s // TQ
    qi_r = attn_in[:, o_qi:o_ki].reshape(b, nq, TQ, IDX_HEADS, IDX_DIM)
    qi_r = qi_r.transpose(0, 1, 3, 2, 4).reshape(b, nq, IDX_HEADS * TQ, IDX_DIM)
    kit = idx_in[:, :IDX_DIM].astype(BF16).reshape(b, s, IDX_DIM).transpose(0, 2, 1)
    wi_r = idx_in[:, IDX_DIM:IDX_DIM + IDX_HEADS].reshape(b, nq, TQ, IDX_HEADS)
    wi_r = wi_r.transpose(0, 1, 3, 2).reshape(b, nq, IDX_HEADS * TQ, 1)

    n_sel = min(DSA_TOPK, s // 4)
    bias = _dsa_select(qi_r, wi_r, kit, n_sel)
    ya = _dsa_attend(q, k, v, bias).reshape(t, ATTN_WIDTH)

    mixed = _gmlp_mix(gates, ya, gmlp_norm_g, w_spatial, b_spatial.T, w_branch_attn.astype(BF16),
                      w_branch_gmlp.astype(BF16), d)
    x1, h2 = _out_proj(x2, mixed, w_out.astype(BF16), ln_ffn_g)

    half = peer_sub_keys.shape[-1]
    sk = peer_sub_keys.reshape(2 * PEER_HEADS, N_KEYS, half).astype(BF16)
    st = _peer_scores(h2, peer_w_q.astype(BF16), sk)
    th, e1, e2 = _peer_select(st)
    peer_out = _peer_dense(h2, peer_u.astype(BF16), peer_v.astype(BF16).T, st, th, e1, e2)
    return x1, peer_out


def kernel(x, ln_mix_g, w_in, gmlp_norm_g, w_spatial, b_spatial, w_branch_attn, w_branch_gmlp, w_out, ln_ffn_g, peer_w_q, peer_sub_keys, peer_u, peer_v, ln_final_g):
    b, s, d = x.shape
    depth = w_in.shape[0]
    x2 = x.reshape(b * s, d)
    for l in range(depth):
        x1, peer_out = _layer(x2, b, s, ln_mix_g[l], w_in[l], gmlp_norm_g[l], w_spatial[l], b_spatial[l],
                              w_branch_attn[l], w_branch_gmlp[l], w_out[l], ln_ffn_g[l], peer_w_q[l],
                              peer_sub_keys[l], peer_u[l], peer_v[l])
        if l + 1 < depth:
            x2 = x1 + peer_out
    return _final_norm(x1, peer_out, ln_final_g).reshape(b, s, d)
```

```python
import functools
import math

import numpy as np
import jax
import jax.numpy as jnp
from jax import lax
from jax.experimental import pallas as pl
from jax.experimental.pallas import tpu as pltpu

F32 = jnp.float32
BF16 = jnp.bfloat16
I32 = jnp.int32

ATTN_HEADS = 8
HEAD_DIM = 128
ATTN_WIDTH = ATTN_HEADS * HEAD_DIM
IDX_HEADS = 8
IDX_DIM = 64
DSA_TOPK = 256
GMLP_GROUPS = 8
CHUNK = 128
GMLP_WIDTH = GMLP_GROUPS * CHUNK
N_KEYS = 128
PEER_HEADS = 8
PEER_TOPK = 16
EPS = 1e-6

LANES = 128
MXU_COLS = 256
VMEM_LIMIT_BYTES = 56 * 1024 * 1024

INT_MIN = -(2 ** 31)
NEG_BIG = -1e30

TQ = 128
TK = 512
ATQ = 512
ATK = 256
PROJ_TM = 1024
ATTN_TN = 896
GATE_TN = 1024
NORM_TT = 512
MIX_TT = 256
PEER_TT = 512
PEER_ROWS = 8
SEL_TL = 512


def _cparams(sem):
    return pltpu.CompilerParams(dimension_semantics=sem, vmem_limit_bytes=VMEM_LIMIT_BYTES)


def _norm_proj_kernel(x_ref, g_ref, w_ref, o_ref, h_scr):
    @pl.when(pl.program_id(1) == 0)
    def _():
        x = x_ref[...]
        ms = jnp.mean(x * x, axis=-1, keepdims=True)
        h_scr[...] = ((x * lax.rsqrt(ms + EPS)) * g_ref[...]).astype(BF16)

    o_ref[...] = jnp.dot(h_scr[...], w_ref[...], preferred_element_type=F32).astype(o_ref.dtype)


def _norm_proj(x2, g, w_bf, out_dtype, tn):
    t, d = x2.shape
    n = w_bf.shape[1]
    tm = min(PROJ_TM, t)
    return pl.pallas_call(
        _norm_proj_kernel,
        out_shape=jax.ShapeDtypeStruct((t, n), out_dtype),
        grid=(t // tm, n // tn),
        in_specs=[
            pl.BlockSpec((tm, d), lambda i, j: (i, 0)),
            pl.BlockSpec((1, d), lambda i, j: (0, 0)),
            pl.BlockSpec((d, tn), lambda i, j: (0, j)),
        ],
        out_specs=pl.BlockSpec((tm, tn), lambda i, j: (i, j)),
        scratch_shapes=[pltpu.VMEM((tm, d), BF16)],
        compiler_params=_cparams(("parallel", "arbitrary")),
        name="norm_proj",
    )(x2, g.reshape(1, d), w_bf)


def _sort_key(x):
    bits = pltpu.bitcast(x, I32)
    return bits ^ ((bits >> 31) & 0x7FFFFFFF)


def _dsa_select_kernel(qi_ref, wi_ref, kit_ref, bias_ref, keys_scr, wb_scr, *, n_sel, seq):
    i = pl.program_id(1)
    n_kt = (i * TQ + TQ + TK - 1) // TK
    wb_scr[...] = jnp.broadcast_to((wi_ref[...] * (IDX_HEADS ** -0.5)) * (IDX_DIM ** -0.5), wb_scr.shape)
    q_pos = i * TQ + lax.broadcasted_iota(I32, (TQ, LANES), 0)
    lane = lax.broadcasted_iota(I32, (TQ, LANES), 1)

    def chunk_at(off):
        return pl.ds(pl.multiple_of(off, LANES), LANES)

    def score_body(kt, carry):
        m1, m2 = carry
        off = pl.multiple_of(kt * TK, TK)
        accs = [None] * (TK // LANES)
        for half in range(TK // MXU_COLS):
            kit_tile = kit_ref[:, pl.ds(pl.multiple_of(off + half * MXU_COLS, MXU_COLS), MXU_COLS)]
            for h in range(IDX_HEADS):
                hs = slice(h * TQ, (h + 1) * TQ)
                dots = jnp.maximum(jnp.dot(qi_ref[hs, :], kit_tile, preferred_element_type=F32), 0.0)
                for cc in range(MXU_COLS // LANES):
                    c = half * (MXU_COLS // LANES) + cc
                    term = dots[:, cc * LANES:(cc + 1) * LANES] * wb_scr[hs, :]
                    accs[c] = term if accs[c] is None else accs[c] + term
        for c in range(TK // LANES):
            sc = accs[c] + 0.0
            causal = (off + c * LANES + lane) <= q_pos
            scm = jnp.where(causal, sc, -jnp.inf)
            m2 = jnp.maximum(m2, jnp.minimum(m1, scm))
            m1 = jnp.maximum(m1, scm)
            keys_scr[:, chunk_at(off + c * LANES)] = jnp.where(causal, _sort_key(sc), INT_MIN)
        return m1, m2

    neg_inf = jnp.full((TQ, LANES), -jnp.inf, F32)
    m1, m2 = lax.fori_loop(0, n_kt, score_body, (neg_inf, neg_inf))

    def count(preds):
        def body(kt, accs):
            off = pl.multiple_of(kt * TK, TK)
            for c in range(TK // LANES):
                blk = keys_scr[:, chunk_at(off + c * LANES)]
                accs = tuple(a + p(blk, off + c * LANES) for a, p in zip(accs, preds))
            return accs
        accs = lax.fori_loop(0, n_kt, body, tuple(jnp.zeros((TQ, LANES), I32) for _ in preds))
        return [jnp.sum(a.astype(F32), axis=1, keepdims=True).astype(I32) for a in accs]

    low_f = jnp.min(m2, axis=1, keepdims=True)
    low_key = _sort_key(jnp.broadcast_to(low_f, (TQ, LANES)))[:, :1]
    lo0 = jnp.where(low_f == -jnp.inf, INT_MIN, low_key)
    hi0 = _sort_key(jnp.broadcast_to(jnp.max(m1, axis=1, keepdims=True), (TQ, LANES)))[:, :1] + 1

    def bisect_body(state):
        lo, hi, _ = state
        mid = (lo >> 1) + (hi >> 1) + (lo & hi & 1)
        mb = jnp.broadcast_to(mid, (TQ, LANES))
        c, = count([lambda blk, off: (blk >= mb).astype(I32)])
        ge = c >= n_sel
        lo_n = jnp.where(ge, mid, lo)
        hi_n = jnp.where(c == n_sel, mid + 1, jnp.where(ge, hi, mid))
        return lo_n, hi_n, jnp.max(jnp.where(hi_n != lo_n + 1, 1.0, 0.0)).astype(F32)

    t, _, _ = lax.while_loop(lambda st: st[2] > 0.0, bisect_body, (lo0, hi0, jnp.float32(1.0)))
    tb = jnp.broadcast_to(t, (TQ, LANES))

    n_gt, n_eq = count([lambda blk, off: (blk > tb).astype(I32), lambda blk, off: (blk == tb).astype(I32)])
    need = n_sel - n_gt
    ambiguous = jnp.max(jnp.where((n_eq > need) & (t > INT_MIN), 1.0, 0.0))

    def write_bias(keep_fn):
        def body(kt, carry):
            off = pl.multiple_of(kt * TK, TK)
            for c in range(TK // LANES):
                blk = keys_scr[:, chunk_at(off + c * LANES)]
                bias_ref[:, chunk_at(off + c * LANES)] = jnp.where(
                    keep_fn(blk, off + c * LANES), 0.0, NEG_BIG).astype(BF16)
            return carry
        lax.fori_loop(0, n_kt, body, 0)

    @pl.when(ambiguous <= 0)
    def _():
        thr = jnp.broadcast_to(jnp.maximum(t, INT_MIN + 1), (TQ, LANES))
        write_bias(lambda blk, off: blk >= thr)

    @pl.when(ambiguous > 0)
    def _():
        n_bits = max(1, (seq - 1).bit_length())

        def jbit_body(k, x):
            cand = x + lax.shift_left(jnp.int32(1), jnp.int32(n_bits - 1) - k)
            xb = jnp.broadcast_to(cand, (TQ, LANES))
            below, = count([lambda blk, off: jnp.where(blk == tb, ((off + lane) < xb).astype(I32), 0)])
            return jnp.where(below < need, cand, x)

        x = lax.fori_loop(0, n_bits, jbit_body, jnp.zeros((TQ, 1), I32))
        cut = jnp.where(t == INT_MIN, -1, jnp.where(n_eq > need, x, seq))
        jb = jnp.broadcast_to(cut, (TQ, LANES))
        write_bias(lambda blk, off: jnp.where(blk > tb, 1, jnp.where(blk == tb, ((off + lane) <= jb).astype(I32), 0)) > 0)

    def fill_body(kt, carry):
        bias_ref[:, pl.ds(pl.multiple_of(kt * TK, TK), TK)] = jnp.full((TQ, TK), NEG_BIG, BF16)
        return carry

    lax.fori_loop(n_kt, seq // TK, fill_body, 0)


def _dsa_select(qi_r, wi_r, kit, n_sel):
    b, nq = qi_r.shape[0], qi_r.shape[1]
    s = kit.shape[2]
    assert n_sel <= 2 * LANES and s % TK == 0
    return pl.pallas_call(
        functools.partial(_dsa_select_kernel, n_sel=n_sel, seq=s),
        out_shape=jax.ShapeDtypeStruct((b, s, s), BF16),
        grid=(b, nq),
        in_specs=[
            pl.BlockSpec((None, None, IDX_HEADS * TQ, IDX_DIM), lambda bb, i: (bb, i, 0, 0)),
            pl.BlockSpec((None, None, IDX_HEADS * TQ, 1), lambda bb, i: (bb, i, 0, 0)),
            pl.BlockSpec((None, IDX_DIM, s), lambda bb, i: (bb, 0, 0)),
        ],
        out_specs=pl.BlockSpec((None, TQ, s), lambda bb, i: (bb, i, 0)),
        scratch_shapes=[
            pltpu.VMEM((TQ, s), I32),
            pltpu.VMEM((IDX_HEADS * TQ, LANES), F32),
        ],
        compiler_params=_cparams(("parallel", "arbitrary")),
        name="dsa_select",
    )(qi_r, wi_r, kit)


def _dsa_attend_kernel(qtab_ref, ktab_ref, q_ref, k_ref, v_ref, bias_ref, o_ref, m_scr, acc_scr):
    step = pl.program_id(1)
    i = qtab_ref[step]
    kt = ktab_ref[step]
    last_kt = (i * ATQ + ATQ - 1) // ATK

    @pl.when(kt == 0)
    def _():
        m_scr[...] = jnp.full(m_scr.shape, NEG_BIG, F32)
        acc_scr[...] = jnp.zeros(acc_scr.shape, F32)

    bias = bias_ref[...].astype(F32)
    log2e_scale = (HEAD_DIM ** -0.5) * math.log2(math.e)
    ones = jnp.ones((ATK, HEAD_DIM), BF16)
    for h in range(ATTN_HEADS):
        hs = slice(h * HEAD_DIM, (h + 1) * HEAD_DIM)
        logits = lax.dot_general(q_ref[:, hs], k_ref[:, hs], (((1,), (1,)), ((), ())),
                                 preferred_element_type=F32)
        s = logits * log2e_scale + bias
        m_old = m_scr[h]
        m_new = jnp.maximum(m_old, jnp.max(s, axis=1, keepdims=True))
        alpha = jnp.exp2(m_old - m_new)
        p = jnp.concatenate([jnp.exp2(s[:, c * LANES:(c + 1) * LANES] - m_new) for c in range(ATK // LANES)],
                            axis=1).astype(BF16)
        v_ext = jnp.concatenate([v_ref[:, hs], ones], axis=1)
        pv = jnp.dot(p, v_ext, preferred_element_type=F32)
        acc_scr[h] = jnp.concatenate([alpha, alpha], axis=1) * acc_scr[h] + pv
        m_scr[h] = m_new

    @pl.when(kt == last_kt)
    def _():
        for h in range(ATTN_HEADS):
            acc = acc_scr[h]
            o_ref[:, h * HEAD_DIM:(h + 1) * HEAD_DIM] = (acc[:, :HEAD_DIM] / acc[:, HEAD_DIM:]).astype(o_ref.dtype)


def _dsa_attend(q, k, v, bias):
    b, s, _ = q.shape
    atq = min(ATQ, s)
    assert atq == ATQ and s % ATQ == 0
    qtab, ktab = [], []
    for i in range(s // ATQ):
        for kt in range((i * ATQ + ATQ - 1) // ATK + 1):
            qtab.append(i)
            ktab.append(kt)
    grid_spec = pltpu.PrefetchScalarGridSpec(
        num_scalar_prefetch=2,
        grid=(b, len(qtab)),
        in_specs=[
            pl.BlockSpec((None, ATQ, ATTN_WIDTH), lambda bb, st, qt, kk: (bb, qt[st], 0)),
            pl.BlockSpec((None, ATK, ATTN_WIDTH), lambda bb, st, qt, kk: (bb, kk[st], 0)),
            pl.BlockSpec((None, ATK, ATTN_WIDTH), lambda bb, st, qt, kk: (bb, kk[st], 0)),
            pl.BlockSpec((None, ATQ, ATK), lambda bb, st, qt, kk: (bb, qt[st], kk[st])),
        ],
        out_specs=pl.BlockSpec((None, ATQ, ATTN_WIDTH), lambda bb, st, qt, kk: (bb, qt[st], 0)),
        scratch_shapes=[
            pltpu.VMEM((ATTN_HEADS, ATQ, LANES), F32),
            pltpu.VMEM((ATTN_HEADS, ATQ, 2 * HEAD_DIM), F32),
        ],
    )
    return pl.pallas_call(
        _dsa_attend_kernel,
        out_shape=jax.ShapeDtypeStruct((b, s, ATTN_WIDTH), BF16),
        grid_spec=grid_spec,
        compiler_params=_cparams(("parallel", "arbitrary")),
        name="dsa_attend",
    )(jnp.asarray(qtab, I32), jnp.asarray(ktab, I32), q, k, v, bias)


def _gmlp_mix_kernel(gu_ref, gv_ref, ga_ref, gb_ref, ya_ref, ng_ref, ws_ref, bst_ref, wa_ref, wb_ref,
                     o_ref, yb_scr):
    tt = gu_ref.shape[0]
    row = lax.broadcasted_iota(I32, (CHUNK, CHUNK), 0)
    col = lax.broadcasted_iota(I32, (CHUNK, CHUNK), 1)
    tril = col <= row
    for c in range(tt // CHUNK):
        rs = slice(c * CHUNK, (c + 1) * CHUNK)
        u = jax.nn.gelu(gu_ref[rs, :])
        v = jax.nn.gelu(gv_ref[rs, :])
        v = (v * lax.rsqrt(jnp.mean(v * v, axis=-1, keepdims=True) + EPS)) * ng_ref[...]
        vb = v.astype(BF16)
        for g in range(GMLP_GROUPS):
            gs = slice(g * CHUNK, (g + 1) * CHUNK)
            wm = jnp.where(tril, ws_ref[g], 0.0).astype(BF16)
            z = jnp.dot(wm, vb[:, gs], preferred_element_type=F32) + bst_ref[:, g:g + 1]
            yb_scr[rs, gs] = (u[:, gs] * z).astype(BF16)
    ma = jnp.dot(ya_ref[...], wa_ref[...], preferred_element_type=F32)
    mb = jnp.dot(yb_scr[...], wb_ref[...], preferred_element_type=F32)
    o_ref[...] = (jax.nn.sigmoid(ga_ref[...]) * ma + jax.nn.sigmoid(gb_ref[...]) * mb).astype(o_ref.dtype)


def _gmlp_mix(gates, ya, norm_g, w_s, b_s_t, wa_bf, wb_bf, d_model):
    t = ya.shape[0]
    tt = min(MIX_TT, t)
    assert d_model == 2 * GMLP_WIDTH
    return pl.pallas_call(
        _gmlp_mix_kernel,
        out_shape=jax.ShapeDtypeStruct((t, d_model), BF16),
        grid=(t // tt,),
        in_specs=[
            pl.BlockSpec((tt, GMLP_WIDTH), lambda i: (i, 0)),
            pl.BlockSpec((tt, GMLP_WIDTH), lambda i: (i, 1)),
            pl.BlockSpec((tt, d_model), lambda i: (i, 1)),
            pl.BlockSpec((tt, d_model), lambda i: (i, 2)),
            pl.BlockSpec((tt, ATTN_WIDTH), lambda i: (i, 0)),
            pl.BlockSpec((1, GMLP_WIDTH), lambda i: (0, 0)),
            pl.BlockSpec((GMLP_GROUPS, CHUNK, CHUNK), lambda i: (0, 0, 0)),
            pl.BlockSpec((CHUNK, GMLP_GROUPS), lambda i: (0, 0)),
            pl.BlockSpec((ATTN_WIDTH, d_model), lambda i: (0, 0)),
            pl.BlockSpec((GMLP_WIDTH, d_model), lambda i: (0, 0)),
        ],
        out_specs=pl.BlockSpec((tt, d_model), lambda i: (i, 0)),
        scratch_shapes=[pltpu.VMEM((tt, GMLP_WIDTH), BF16)],
        compiler_params=_cparams(("parallel",)),
        name="gmlp_mix",
    )(gates, gates, gates, gates, ya, norm_g.reshape(1, GMLP_WIDTH), w_s, b_s_t, wa_bf, wb_bf)


def _out_proj_kernel(x_ref, m_ref, wo_ref, g_ref, x1_ref, h2_ref):
    x1 = x_ref[...] + jnp.dot(m_ref[...], wo_ref[...], preferred_element_type=F32)
    x1_ref[...] = x1
    ms = jnp.mean(x1 * x1, axis=-1, keepdims=True)
    h2_ref[...] = ((x1 * lax.rsqrt(ms + EPS)) * g_ref[...]).astype(BF16)


def _out_proj(x2, mixed, wo_bf, g):
    t, d = x2.shape
    tt = min(MIX_TT, t)
    return pl.pallas_call(
        _out_proj_kernel,
        out_shape=(jax.ShapeDtypeStruct((t, d), F32), jax.ShapeDtypeStruct((t, d), BF16)),
        grid=(t // tt,),
        in_specs=[
            pl.BlockSpec((tt, d), lambda i: (i, 0)),
            pl.BlockSpec((tt, d), lambda i: (i, 0)),
            pl.BlockSpec((d, d), lambda i: (0, 0)),
            pl.BlockSpec((1, d), lambda i: (0, 0)),
        ],
        out_specs=(pl.BlockSpec((tt, d), lambda i: (i, 0)), pl.BlockSpec((tt, d), lambda i: (i, 0))),
        compiler_params=_cparams(("parallel",)),
        name="out_proj",
    )(x2, mixed, wo_bf, g.reshape(1, d))


def _peer_scores_kernel(h2_ref, wq_ref, sk_ref, st_ref):
    qp = jnp.dot(h2_ref[...], wq_ref[...], preferred_element_type=F32).astype(BF16)
    half = sk_ref.shape[2]
    for hp in range(2 * PEER_HEADS):
        st_ref[hp] = lax.dot_general(sk_ref[hp], qp[:, hp * half:(hp + 1) * half], (((1,), (1,)), ((), ())),
                                     preferred_element_type=F32)


def _peer_scores(h2, wq_bf, sk_bf):
    t, d = h2.shape
    tt = min(MIX_TT, t)
    nq = wq_bf.shape[1]
    half = sk_bf.shape[2]
    return pl.pallas_call(
        _peer_scores_kernel,
        out_shape=jax.ShapeDtypeStruct((2 * PEER_HEADS, N_KEYS, t), F32),
        grid=(t // tt,),
        in_specs=[
            pl.BlockSpec((tt, d), lambda i: (i, 0)),
            pl.BlockSpec((d, nq), lambda i: (0, 0)),
            pl.BlockSpec((2 * PEER_HEADS, N_KEYS, half), lambda i: (0, 0, 0)),
        ],
        out_specs=pl.BlockSpec((2 * PEER_HEADS, N_KEYS, tt), lambda i: (0, 0, i)),
        compiler_params=_cparams(("parallel",)),
        name="peer_scores",
    )(h2, wq_bf, sk_bf)


def _top_values(cur, k):
    n = cur.shape[0]
    idx = lax.broadcasted_iota(I32, cur.shape, 0).astype(F32)
    vals = []
    for _ in range(k):
        mx = jnp.max(cur, axis=0, keepdims=True)
        first = jnp.min(jnp.where(cur == mx, idx, float(n)), axis=0, keepdims=True)
        vals.append(mx)
        cur = jnp.where(idx == first, -jnp.inf, cur)
    return vals


def _staircase_sums(t1, t2):
    a1 = jnp.concatenate(t1, axis=0)
    a2 = jnp.concatenate(t2, axis=0)
    r16 = lax.broadcasted_iota(I32, (PEER_TOPK, LANES), 0)
    r8 = r16[:8]
    ninf = -jnp.inf
    return jnp.concatenate([
        t1[0] + a2,
        t1[1] + a2[:8],
        jnp.where(r16 >= 2, a1 + t2[0], ninf),
        jnp.where(r8 >= 2, a1[:8] + t2[1], ninf),
        jnp.where((r8 >= 2) & (r8 <= 4), t1[2] + a2[:8], ninf),
        jnp.where((r8 >= 2) & (r8 <= 3), t1[3] + a2[:8], ninf),
        jnp.where(r8 == 2, t1[4] + a2[:8], ninf),
    ], axis=0)


def _peer_select_kernel(st_ref, th_ref, e1_ref, e2_ref):
    tl = st_ref.shape[2]

    def lane_group(g, carry):
        ls = pl.ds(pl.multiple_of(g * LANES, LANES), LANES)
        for h in range(PEER_HEADS):
            s1 = st_ref[2 * h, :, ls]
            s2 = st_ref[2 * h + 1, :, ls]
            t1 = _top_values(s1, PEER_TOPK)
            t2 = _top_values(s2, PEER_TOPK)
            best = _top_values(_staircase_sums(t1, t2), PEER_TOPK)
            m = best[0]
            z = functools.reduce(lambda a, b: a + b, [jnp.exp(bs - m) for bs in best])
            th_ref[h:h + 1, ls] = best[PEER_TOPK - 1]
            e1_ref[h, :, ls] = jnp.exp(s1 - t1[0]) / z
            e2_ref[h, :, ls] = jnp.exp(s2 - t2[0])
        return carry

    lax.fori_loop(0, tl // LANES, lane_group, 0)


def _peer_select(st):
    t = st.shape[2]
    tl = min(SEL_TL, t)
    e_shape = jax.ShapeDtypeStruct((PEER_HEADS, N_KEYS, t), F32)
    return pl.pallas_call(
        _peer_select_kernel,
        out_shape=(jax.ShapeDtypeStruct((PEER_HEADS, t), F32), e_shape, e_shape),
        grid=(t // tl,),
        in_specs=[pl.BlockSpec((2 * PEER_HEADS, N_KEYS, tl), lambda i: (0, 0, i))],
        out_specs=(
            pl.BlockSpec((PEER_HEADS, tl), lambda i: (0, i)),
            pl.BlockSpec((PEER_HEADS, N_KEYS, tl), lambda i: (0, 0, i)),
            pl.BlockSpec((PEER_HEADS, N_KEYS, tl), lambda i: (0, 0, i)),
        ),
        compiler_params=_cparams(("parallel",)),
        name="peer_select",
    )(st)


def _peer_dense_kernel(h2_ref, u_ref, vt_ref, s2_ref, s1row_ref, e1row_ref, th_ref, e2_ref, o_ref,
                       act_scr, p_scr, acc_scr):
    ei = pl.program_id(1)
    te, tt = act_scr.shape

    @pl.when(ei == 0)
    def _():
        acc_scr[...] = jnp.zeros(acc_scr.shape, F32)

    act_scr[...] = lax.dot_general(u_ref[...], h2_ref[...], (((1,), (1,)), ((), ())),
                                   preferred_element_type=F32)
    for ii in range(PEER_ROWS):
        rs = slice(ii * N_KEYS, (ii + 1) * N_KEYS)
        for lc in range(tt // LANES):
            ls = slice(lc * LANES, (lc + 1) * LANES)
            coef = jnp.zeros((N_KEYS, LANES), F32)
            for h in range(PEER_HEADS):
                s1row = s1row_ref[h, ii:ii + 1, ls]
                e1row = e1row_ref[h, ii:ii + 1, ls]
                keep = (s2_ref[h, :, ls] + s1row) >= th_ref[h:h + 1, ls]
                coef = coef + jnp.where(keep, e2_ref[h, :, ls] * e1row, 0.0)
            p_scr[rs, ls] = (coef * jax.nn.gelu(act_scr[rs, ls])).astype(BF16)
    acc_scr[...] += jnp.dot(vt_ref[...], p_scr[...], preferred_element_type=F32)

    @pl.when(ei == pl.num_programs(1) - 1)
    def _():
        o_ref[...] = acc_scr[...].T


def _peer_dense(h2, u_bf, vt_bf, st, th, e1, e2):
    t, d = h2.shape
    n_exp = u_bf.shape[0]
    tt = min(PEER_TT, t)
    te = PEER_ROWS * N_KEYS
    st4 = st.reshape(PEER_HEADS, 2, N_KEYS, t)
    return pl.pallas_call(
        _peer_dense_kernel,
        out_shape=jax.ShapeDtypeStruct((t, d), F32),
        grid=(t // tt, n_exp // te),
        in_specs=[
            pl.BlockSpec((tt, d), lambda i, e: (i, 0)),
            pl.BlockSpec((te, d), lambda i, e: (e, 0)),
            pl.BlockSpec((d, te), lambda i, e: (0, e)),
            pl.BlockSpec((PEER_HEADS, None, N_KEYS, tt), lambda i, e: (0, 1, 0, i)),
            pl.BlockSpec((PEER_HEADS, None, PEER_ROWS, tt), lambda i, e: (0, 0, e, i)),
            pl.BlockSpec((PEER_HEADS, PEER_ROWS, tt), lambda i, e: (0, e, i)),
            pl.BlockSpec((PEER_HEADS, tt), lambda i, e: (0, i)),
            pl.BlockSpec((PEER_HEADS, N_KEYS, tt), lambda i, e: (0, 0, i)),
        ],
        out_specs=pl.BlockSpec((tt, d), lambda i, e: (i, 0)),
        scratch_shapes=[
            pltpu.VMEM((te, tt), F32),
            pltpu.VMEM((te, tt), BF16),
            pltpu.VMEM((d, tt), F32),
        ],
        compiler_params=_cparams(("parallel", "arbitrary")),
        name="peer_dense",
    )(h2, u_bf, vt_bf, st4, st4, e1, th, e2)


def _final_norm_kernel(x1_ref, p_ref, g_ref, o_ref):
    y = x1_ref[...] + p_ref[...]
    ms = jnp.mean(y * y, axis=-1, keepdims=True)
    o_ref[...] = (y * lax.rsqrt(ms + EPS)) * g_ref[...]


def _final_norm(x1, peer_out, g):
    t, d = x1.shape
    tt = min(NORM_TT, t)
    return pl.pallas_call(
        _final_norm_kernel,
        out_shape=jax.ShapeDtypeStruct((t, d), F32),
        grid=(t // tt,),
        in_specs=[
            pl.BlockSpec((tt, d), lambda i: (i, 0)),
            pl.BlockSpec((tt, d), lambda i: (i, 0)),
            pl.BlockSpec((1, d), lambda i: (0, 0)),
        ],
        out_specs=pl.BlockSpec((tt, d), lambda i: (i, 0)),
        compiler_params=_cparams(("parallel",)),
        name="final_norm",
    )(x1, peer_out, g.reshape(1, d))


def _layer(x2, b, s, ln_mix_g, w_in, gmlp_norm_g, w_spatial, b_spatial, w_branch_attn, w_branch_gmlp,
           w_out, ln_ffn_g, peer_w_q, peer_sub_keys, peer_u, peer_v):
    t, d = x2.shape
    n_qi = IDX_HEADS * IDX_DIM
    o_qi = 3 * ATTN_WIDTH
    o_ki = o_qi + n_qi
    o_wi = o_ki + IDX_DIM
    o_gate = o_wi + IDX_HEADS

    w_attn = w_in[:, :o_ki].astype(BF16)
    w_idx = jnp.pad(w_in[:, o_ki:o_gate], ((0, 0), (0, LANES - IDX_DIM - IDX_HEADS))).astype(BF16)
    w_gate = w_in[:, o_gate:].astype(BF16)

    attn_in = _norm_proj(x2, ln_mix_g, w_attn, BF16, ATTN_TN)
    idx_in = _norm_proj(x2, ln_mix_g, w_idx, F32, LANES)
    gates = _norm_proj(x2, ln_mix_g, w_gate, F32, GATE_TN)

    q = attn_in[:, :ATTN_WIDTH].reshape(b, s, ATTN_WIDTH)
    k = attn_in[:, ATTN_WIDTH:2 * ATTN_WIDTH].reshape(b, s, ATTN_WIDTH)
    v = attn_in[:, 2 * ATTN_WIDTH:o_qi].reshape(b, s, ATTN_WIDTH)
    nq = s // TQ
    qi_r = attn_in[:, o_qi:o_ki].reshape(b, nq, TQ, IDX_HEADS, IDX_DIM)
    qi_r = qi_r.transpose(0, 1, 3, 2, 4).reshape(b, nq, IDX_HEADS * TQ, IDX_DIM)
    kit = idx_in[:, :IDX_DIM].astype(BF16).reshape(b, s, IDX_DIM).transpose(0, 2, 1)
    wi_r = idx_in[:, IDX_DIM:IDX_DIM + IDX_HEADS].reshape(b, nq, TQ, IDX_HEADS)
    wi_r = wi_r.transpose(0, 1, 3, 2).reshape(b, nq, IDX_HEADS * TQ, 1)

    n_sel = min(DSA_TOPK, s // 4)
    bias = _dsa_select(qi_r, wi_r, kit, n_sel)
    ya = _dsa_attend(q, k, v, bias).reshape(t, ATTN_WIDTH)

    mixed = _gmlp_mix(gates, ya, gmlp_norm_g, w_spatial, b_spatial.T, w_branch_attn.astype(BF16),
                      w_branch_gmlp.astype(BF16), d)
    x1, h2 = _out_proj(x2, mixed, w_out.astype(BF16), ln_ffn_g)

    half = peer_sub_keys.shape[-1]
    sk = peer_sub_keys.reshape(2 * PEER_HEADS, N_KEYS, half).astype(BF16)
    st = _peer_scores(h2, peer_w_q.astype(BF16), sk)
    th, e1, e2 = _peer_select(st)
    peer_out = _peer_dense(h2, peer_u.astype(BF16), peer_v.astype(BF16).T, st, th, e1, e2)
    return x1, peer_out


def kernel(x, ln_mix_g, w_in, gmlp_norm_g, w_spatial, b_spatial, w_branch_attn, w_branch_gmlp, w_out, ln_ffn_g, peer_w_q, peer_sub_keys, peer_u, peer_v, ln_final_g):
    b, s, d = x.shape
    depth = w_in.shape[0]
    x2 = x.reshape(b * s, d)
    for l in range(depth):
        x1, peer_out = _layer(x2, b, s, ln_mix_g[l], w_in[l], gmlp_norm_g[l], w_spatial[l], b_spatial[l],
                              w_branch_attn[l], w_branch_gmlp[l], w_out[l], ln_ffn_g[l], peer_w_q[l],
                              peer_sub_keys[l], peer_u[l], peer_v[l])
        if l + 1 < depth:
            x2 = x1 + peer_out
    return _final_norm(x1, peer_out, ln_final_g).reshape(b, s, d)
```

```python
import functools
import math

import numpy as np
import jax
import jax.numpy as jnp
from jax import lax
from jax.experimental import pallas as pl
from jax.experimental.pallas import tpu as pltpu

F32 = jnp.float32
BF16 = jnp.bfloat16
I32 = jnp.int32

ATTN_HEADS = 8
HEAD_DIM = 128
ATTN_WIDTH = ATTN_HEADS * HEAD_DIM
IDX_HEADS = 8
IDX_DIM = 64
DSA_TOPK = 256
GMLP_GROUPS = 8
CHUNK = 128
GMLP_WIDTH = GMLP_GROUPS * CHUNK
N_KEYS = 128
PEER_HEADS = 8
PEER_TOPK = 16
EPS = 1e-6

LANES = 128
MXU_COLS = 256
VMEM_LIMIT_BYTES = 56 * 1024 * 1024

INT_MIN = -(2 ** 31)
NEG_BIG = -1e30

TQ = 128
TK = 512
SCAN_KEYS = 2048
ATQ = 512
ATK = 512
PROJ_TM = 1024
ATTN_TN = 896
GATE_TN = 1024
NORM_TT = 512
MIX_TT = 256
PEER_TT = 512
PEER_ROWS = 8
SEL_TL = 512


def _cparams(sem):
    return pltpu.CompilerParams(dimension_semantics=sem, vmem_limit_bytes=VMEM_LIMIT_BYTES)


def _norm_proj_kernel(x_ref, g_ref, w_ref, o_ref, h_scr):
    @pl.when(pl.program_id(1) == 0)
    def _():
        x = x_ref[...]
        ms = jnp.mean(x * x, axis=-1, keepdims=True)
        h_scr[...] = ((x * lax.rsqrt(ms + EPS)) * g_ref[...]).astype(BF16)

    o_ref[...] = jnp.dot(h_scr[...], w_ref[...], preferred_element_type=F32).astype(o_ref.dtype)


def _norm_proj(x2, g, w_bf, out_dtype, tn):
    t, d = x2.shape
    n = w_bf.shape[1]
    tm = min(PROJ_TM, t)
    return pl.pallas_call(
        _norm_proj_kernel,
        out_shape=jax.ShapeDtypeStruct((t, n), out_dtype),
        grid=(t // tm, n // tn),
        in_specs=[
            pl.BlockSpec((tm, d), lambda i, j: (i, 0)),
            pl.BlockSpec((1, d), lambda i, j: (0, 0)),
            pl.BlockSpec((d, tn), lambda i, j: (0, j)),
        ],
        out_specs=pl.BlockSpec((tm, tn), lambda i, j: (i, j)),
        scratch_shapes=[pltpu.VMEM((tm, d), BF16)],
        compiler_params=_cparams(("parallel", "arbitrary")),
        name="norm_proj",
    )(x2, g.reshape(1, d), w_bf)


def _sort_key(x):
    bits = pltpu.bitcast(x, I32)
    return bits ^ ((bits >> 31) & 0x7FFFFFFF)


def _dsa_select_kernel(qi_ref, wi_ref, kit_ref, bias_ref, keys_scr, wb_scr, *, n_sel, seq):
    i = pl.program_id(1)
    n_kt = (i * TQ + TQ + TK - 1) // TK
    wb_scr[...] = jnp.broadcast_to((wi_ref[...] * (IDX_HEADS ** -0.5)) * (IDX_DIM ** -0.5), wb_scr.shape)
    q_pos = i * TQ + lax.broadcasted_iota(I32, (TQ, LANES), 0)
    lane = lax.broadcasted_iota(I32, (TQ, LANES), 1)

    def chunk_at(off):
        return pl.ds(pl.multiple_of(off, LANES), LANES)

    def score_body(kt, carry):
        m1, m2 = carry
        off = pl.multiple_of(kt * TK, TK)
        accs = [None] * (TK // LANES)
        for half in range(TK // MXU_COLS):
            kit_tile = kit_ref[:, pl.ds(pl.multiple_of(off + half * MXU_COLS, MXU_COLS), MXU_COLS)]
            for h in range(IDX_HEADS):
                hs = slice(h * TQ, (h + 1) * TQ)
                dots = jnp.maximum(jnp.dot(qi_ref[hs, :], kit_tile, preferred_element_type=F32), 0.0)
                for cc in range(MXU_COLS // LANES):
                    c = half * (MXU_COLS // LANES) + cc
                    term = dots[:, cc * LANES:(cc + 1) * LANES] * wb_scr[hs, :]
                    accs[c] = term if accs[c] is None else accs[c] + term
        for c in range(TK // LANES):
            sc = accs[c] + 0.0
            causal = (off + c * LANES + lane) <= q_pos
            scm = jnp.where(causal, sc, -jnp.inf)
            m2 = jnp.maximum(m2, jnp.minimum(m1, scm))
            m1 = jnp.maximum(m1, scm)
            keys_scr[:, chunk_at(off + c * LANES)] = jnp.where(causal, _sort_key(sc), INT_MIN)
        return m1, m2

    neg_inf = jnp.full((TQ, LANES), -jnp.inf, F32)
    m1, m2 = lax.fori_loop(0, n_kt, score_body, (neg_inf, neg_inf))

    scan = SCAN_KEYS if seq % SCAN_KEYS == 0 else TK
    n_st = (n_kt * TK + scan - 1) // scan
    kt_end = n_st * (scan // TK)

    def pad_body(kt, carry):
        keys_scr[:, pl.ds(pl.multiple_of(kt * TK, TK), TK)] = jnp.full((TQ, TK), INT_MIN, I32)
        return carry

    lax.fori_loop(n_kt, kt_end, pad_body, 0)

    def count(preds):
        def body(st, accs):
            off = pl.multiple_of(st * scan, scan)
            for c in range(scan // LANES):
                blk = keys_scr[:, chunk_at(off + c * LANES)]
                accs = tuple(a + p(blk, off + c * LANES) for a, p in zip(accs, preds))
            return accs
        accs = lax.fori_loop(0, n_st, body, tuple(jnp.zeros((TQ, LANES), I32) for _ in preds))
        return [jnp.sum(a.astype(F32), axis=1, keepdims=True).astype(I32) for a in accs]

    low_f = jnp.min(m2, axis=1, keepdims=True)
    low_key = _sort_key(jnp.broadcast_to(low_f, (TQ, LANES)))[:, :1]
    lo0 = jnp.where(low_f == -jnp.inf, INT_MIN, low_key)
    hi0 = _sort_key(jnp.broadcast_to(jnp.max(m1, axis=1, keepdims=True), (TQ, LANES)))[:, :1] + 1

    def bisect_body(state):
        lo, hi, _ = state
        mid = (lo >> 1) + (hi >> 1) + (lo & hi & 1)
        mb = jnp.broadcast_to(mid, (TQ, LANES))
        c, = count([lambda blk, off: (blk >= mb).astype(I32)])
        ge = c >= n_sel
        lo_n = jnp.where(ge, mid, lo)
        hi_n = jnp.where(c == n_sel, mid + 1, jnp.where(ge, hi, mid))
        return lo_n, hi_n, jnp.max(jnp.where(hi_n != lo_n + 1, 1.0, 0.0)).astype(F32)

    t, _, _ = lax.while_loop(lambda st: st[2] > 0.0, bisect_body, (lo0, hi0, jnp.float32(1.0)))
    tb = jnp.broadcast_to(t, (TQ, LANES))

    n_gt, n_eq = count([lambda blk, off: (blk > tb).astype(I32), lambda blk, off: (blk == tb).astype(I32)])
    need = n_sel - n_gt
    ambiguous = jnp.max(jnp.where((n_eq > need) & (t > INT_MIN), 1.0, 0.0))

    def write_bias(keep_fn):
        def body(st, carry):
            off = pl.multiple_of(st * scan, scan)
            for c in range(scan // LANES):
                blk = keys_scr[:, chunk_at(off + c * LANES)]
                bias_ref[:, chunk_at(off + c * LANES)] = jnp.where(
                    keep_fn(blk, off + c * LANES), 0.0, NEG_BIG).astype(BF16)
            return carry
        lax.fori_loop(0, n_st, body, 0)

    @pl.when(ambiguous <= 0)
    def _():
        thr = jnp.broadcast_to(jnp.maximum(t, INT_MIN + 1), (TQ, LANES))
        write_bias(lambda blk, off: blk >= thr)

    @pl.when(ambiguous > 0)
    def _():
        n_bits = max(1, (seq - 1).bit_length())

        def jbit_body(k, x):
            cand = x + lax.shift_left(jnp.int32(1), jnp.int32(n_bits - 1) - k)
            xb = jnp.broadcast_to(cand, (TQ, LANES))
            below, = count([lambda blk, off: jnp.where(blk == tb, ((off + lane) < xb).astype(I32), 0)])
            return jnp.where(below < need, cand, x)

        x = lax.fori_loop(0, n_bits, jbit_body, jnp.zeros((TQ, 1), I32))
        cut = jnp.where(t == INT_MIN, -1, jnp.where(n_eq > need, x, seq))
        jb = jnp.broadcast_to(cut, (TQ, LANES))
        write_bias(lambda blk, off: jnp.where(blk > tb, 1, jnp.where(blk == tb, ((off + lane) <= jb).astype(I32), 0)) > 0)

    def fill_body(kt, carry):
        bias_ref[:, pl.ds(pl.multiple_of(kt * TK, TK), TK)] = jnp.full((TQ, TK), NEG_BIG, BF16)
        return carry

    lax.fori_loop(kt_end, seq // TK, fill_body, 0)


def _dsa_select(qi_r, wi_r, kit, n_sel):
    b, nq = qi_r.shape[0], qi_r.shape[1]
    s = kit.shape[2]
    assert n_sel <= 2 * LANES and s % TK == 0
    return pl.pallas_call(
        functools.partial(_dsa_select_kernel, n_sel=n_sel, seq=s),
        out_shape=jax.ShapeDtypeStruct((b, s, s), BF16),
        grid=(b, nq),
        in_specs=[
            pl.BlockSpec((None, None, IDX_HEADS * TQ, IDX_DIM), lambda bb, i: (bb, i, 0, 0)),
            pl.BlockSpec((None, None, IDX_HEADS * TQ, 1), lambda bb, i: (bb, i, 0, 0)),
            pl.BlockSpec((None, IDX_DIM, s), lambda bb, i: (bb, 0, 0)),
        ],
        out_specs=pl.BlockSpec((None, TQ, s), lambda bb, i: (bb, i, 0)),
        scratch_shapes=[
            pltpu.VMEM((TQ, s), I32),
            pltpu.VMEM((IDX_HEADS * TQ, LANES), F32),
        ],
        compiler_params=_cparams(("parallel", "arbitrary")),
        name="dsa_select",
    )(qi_r, wi_r, kit)


def _dsa_attend_kernel(qtab_ref, ktab_ref, q_ref, k_ref, v_ref, bias_ref, o_ref, m_scr, acc_scr):
    step = pl.program_id(1)
    i = qtab_ref[step]
    kt = ktab_ref[step]
    last_kt = (i * ATQ + ATQ - 1) // ATK

    @pl.when(kt == 0)
    def _():
        m_scr[...] = jnp.full(m_scr.shape, NEG_BIG, F32)
        acc_scr[...] = jnp.zeros(acc_scr.shape, F32)

    bias = bias_ref[...].astype(F32)
    log2e_scale = (HEAD_DIM ** -0.5) * math.log2(math.e)
    ones = jnp.ones((ATK, HEAD_DIM), BF16)
    for h in range(ATTN_HEADS):
        hs = slice(h * HEAD_DIM, (h + 1) * HEAD_DIM)
        logits = lax.dot_general(q_ref[:, hs], k_ref[:, hs], (((1,), (1,)), ((), ())),
                                 preferred_element_type=F32)
        s = logits * log2e_scale + bias
        m_old = m_scr[h]
        m_new = jnp.maximum(m_old, jnp.max(s, axis=1, keepdims=True))
        alpha = jnp.exp2(m_old - m_new)
        p = jnp.concatenate([jnp.exp2(s[:, c * LANES:(c + 1) * LANES] - m_new) for c in range(ATK // LANES)],
                            axis=1).astype(BF16)
        v_ext = jnp.concatenate([v_ref[:, hs], ones], axis=1)
        pv = jnp.dot(p, v_ext, preferred_element_type=F32)
        acc_scr[h] = jnp.concatenate([alpha, alpha], axis=1) * acc_scr[h] + pv
        m_scr[h] = m_new

    @pl.when(kt == last_kt)
    def _():
        for h in range(ATTN_HEADS):
            acc = acc_scr[h]
            o_ref[:, h * HEAD_DIM:(h + 1) * HEAD_DIM] = (acc[:, :HEAD_DIM] / acc[:, HEAD_DIM:]).astype(o_ref.dtype)


def _dsa_attend(q, k, v, bias):
    b, s, _ = q.shape
    atq = min(ATQ, s)
    assert atq == ATQ and s % ATQ == 0
    qtab, ktab = [], []
    for i in range(s // ATQ):
        for kt in range((i * ATQ + ATQ - 1) // ATK + 1):
            qtab.append(i)
            ktab.append(kt)
    grid_spec = pltpu.PrefetchScalarGridSpec(
        num_scalar_prefetch=2,
        grid=(b, len(qtab)),
        in_specs=[
            pl.BlockSpec((None, ATQ, ATTN_WIDTH), lambda bb, st, qt, kk: (bb, qt[st], 0)),
            pl.BlockSpec((None, ATK, ATTN_WIDTH), lambda bb, st, qt, kk: (bb, kk[st], 0)),
            pl.BlockSpec((None, ATK, ATTN_WIDTH), lambda bb, st, qt, kk: (bb, kk[st], 0)),
            pl.BlockSpec((None, ATQ, ATK), lambda bb, st, qt, kk: (bb, qt[st], kk[st])),
        ],
        out_specs=pl.BlockSpec((None, ATQ, ATTN_WIDTH), lambda bb, st, qt, kk: (bb, qt[st], 0)),
        scratch_shapes=[
            pltpu.VMEM((ATTN_HEADS, ATQ, LANES), F32),
            pltpu.VMEM((ATTN_HEADS, ATQ, 2 * HEAD_DIM), F32),
        ],
    )
    return pl.pallas_call(
        _dsa_attend_kernel,
        out_shape=jax.ShapeDtypeStruct((b, s, ATTN_WIDTH), BF16),
        grid_spec=grid_spec,
        compiler_params=_cparams(("parallel", "arbitrary")),
        name="dsa_attend",
    )(jnp.asarray(qtab, I32), jnp.asarray(ktab, I32), q, k, v, bias)


def _gmlp_mix_kernel(gu_ref, gv_ref, ga_ref, gb_ref, ya_ref, ng_ref, ws_ref, bst_ref, wa_ref, wb_ref,
                     o_ref, yb_scr):
    tt = gu_ref.shape[0]
    row = lax.broadcasted_iota(I32, (CHUNK, CHUNK), 0)
    col = lax.broadcasted_iota(I32, (CHUNK, CHUNK), 1)
    tril = col <= row
    for c in range(tt // CHUNK):
        rs = slice(c * CHUNK, (c + 1) * CHUNK)
        u = jax.nn.gelu(gu_ref[rs, :])
        v = jax.nn.gelu(gv_ref[rs, :])
        v = (v * lax.rsqrt(jnp.mean(v * v, axis=-1, keepdims=True) + EPS)) * ng_ref[...]
        vb = v.astype(BF16)
        for g in range(GMLP_GROUPS):
            gs = slice(g * CHUNK, (g + 1) * CHUNK)
            wm = jnp.where(tril, ws_ref[g], 0.0).astype(BF16)
            z = jnp.dot(wm, vb[:, gs], preferred_element_type=F32) + bst_ref[:, g:g + 1]
            yb_scr[rs, gs] = (u[:, gs] * z).astype(BF16)
    ma = jnp.dot(ya_ref[...], wa_ref[...], preferred_element_type=F32)
    mb = jnp.dot(yb_scr[...], wb_ref[...], preferred_element_type=F32)
    o_ref[...] = (jax.nn.sigmoid(ga_ref[...]) * ma + jax.nn.sigmoid(gb_ref[...]) * mb).astype(o_ref.dtype)


def _gmlp_mix(gates, ya, norm_g, w_s, b_s_t, wa_bf, wb_bf, d_model):
    t = ya.shape[0]
    tt = min(MIX_TT, t)
    assert d_model == 2 * GMLP_WIDTH
    return pl.pallas_call(
        _gmlp_mix_kernel,
        out_shape=jax.ShapeDtypeStruct((t, d_model), BF16),
        grid=(t // tt,),
        in_specs=[
            pl.BlockSpec((tt, GMLP_WIDTH), lambda i: (i, 0)),
            pl.BlockSpec((tt, GMLP_WIDTH), lambda i: (i, 1)),
            pl.BlockSpec((tt, d_model), lambda i: (i, 1)),
            pl.BlockSpec((tt, d_model), lambda i: (i, 2)),
            pl.BlockSpec((tt, ATTN_WIDTH), lambda i: (i, 0)),
            pl.BlockSpec((1, GMLP_WIDTH), lambda i: (0, 0)),
            pl.BlockSpec((GMLP_GROUPS, CHUNK, CHUNK), lambda i: (0, 0, 0)),
            pl.BlockSpec((CHUNK, GMLP_GROUPS), lambda i: (0, 0)),
            pl.BlockSpec((ATTN_WIDTH, d_model), lambda i: (0, 0)),
            pl.BlockSpec((GMLP_WIDTH, d_model), lambda i: (0, 0)),
        ],
        out_specs=pl.BlockSpec((tt, d_model), lambda i: (i, 0)),
        scratch_shapes=[pltpu.VMEM((tt, GMLP_WIDTH), BF16)],
        compiler_params=_cparams(("parallel",)),
        name="gmlp_mix",
    )(gates, gates, gates, gates, ya, norm_g.reshape(1, GMLP_WIDTH), w_s, b_s_t, wa_bf, wb_bf)


def _out_proj_kernel(x_ref, m_ref, wo_ref, g_ref, x1_ref, h2_ref):
    x1 = x_ref[...] + jnp.dot(m_ref[...], wo_ref[...], preferred_element_type=F32)
    x1_ref[...] = x1
    ms = jnp.mean(x1 * x1, axis=-1, keepdims=True)
    h2_ref[...] = ((x1 * lax.rsqrt(ms + EPS)) * g_ref[...]).astype(BF16)


def _out_proj(x2, mixed, wo_bf, g):
    t, d = x2.shape
    tt = min(MIX_TT, t)
    return pl.pallas_call(
        _out_proj_kernel,
        out_shape=(jax.ShapeDtypeStruct((t, d), F32), jax.ShapeDtypeStruct((t, d), BF16)),
        grid=(t // tt,),
        in_specs=[
            pl.BlockSpec((tt, d), lambda i: (i, 0)),
            pl.BlockSpec((tt, d), lambda i: (i, 0)),
            pl.BlockSpec((d, d), lambda i: (0, 0)),
            pl.BlockSpec((1, d), lambda i: (0, 0)),
        ],
        out_specs=(pl.BlockSpec((tt, d), lambda i: (i, 0)), pl.BlockSpec((tt, d), lambda i: (i, 0))),
        compiler_params=_cparams(("parallel",)),
        name="out_proj",
    )(x2, mixed, wo_bf, g.reshape(1, d))


def _peer_scores_kernel(h2_ref, wq_ref, sk_ref, st_ref):
    qp = jnp.dot(h2_ref[...], wq_ref[...], preferred_element_type=F32).astype(BF16)
    half = sk_ref.shape[2]
    for hp in range(2 * PEER_HEADS):
        st_ref[hp] = lax.dot_general(sk_ref[hp], qp[:, hp * half:(hp + 1) * half], (((1,), (1,)), ((), ())),
                                     preferred_element_type=F32)


def _peer_scores(h2, wq_bf, sk_bf):
    t, d = h2.shape
    tt = min(MIX_TT, t)
    nq = wq_bf.shape[1]
    half = sk_bf.shape[2]
    return pl.pallas_call(
        _peer_scores_kernel,
        out_shape=jax.ShapeDtypeStruct((2 * PEER_HEADS, N_KEYS, t), F32),
        grid=(t // tt,),
        in_specs=[
            pl.BlockSpec((tt, d), lambda i: (i, 0)),
            pl.BlockSpec((d, nq), lambda i: (0, 0)),
            pl.BlockSpec((2 * PEER_HEADS, N_KEYS, half), lambda i: (0, 0, 0)),
        ],
        out_specs=pl.BlockSpec((2 * PEER_HEADS, N_KEYS, tt), lambda i: (0, 0, i)),
        compiler_params=_cparams(("parallel",)),
        name="peer_scores",
    )(h2, wq_bf, sk_bf)


def _top_values(cur, k):
    n = cur.shape[0]
    idx = lax.broadcasted_iota(I32, cur.shape, 0).astype(F32)
    vals = []
    for _ in range(k):
        mx = jnp.max(cur, axis=0, keepdims=True)
        first = jnp.min(jnp.where(cur == mx, idx, float(n)), axis=0, keepdims=True)
        vals.append(mx)
        cur = jnp.where(idx == first, -jnp.inf, cur)
    return vals


def _staircase_sums(t1, t2):
    a1 = jnp.concatenate(t1, axis=0)
    a2 = jnp.concatenate(t2, axis=0)
    r16 = lax.broadcasted_iota(I32, (PEER_TOPK, LANES), 0)
    r8 = r16[:8]
    ninf = -jnp.inf
    return jnp.concatenate([
        t1[0] + a2,
        t1[1] + a2[:8],
        jnp.where(r16 >= 2, a1 + t2[0], ninf),
        jnp.where(r8 >= 2, a1[:8] + t2[1], ninf),
        jnp.where((r8 >= 2) & (r8 <= 4), t1[2] + a2[:8], ninf),
        jnp.where((r8 >= 2) & (r8 <= 3), t1[3] + a2[:8], ninf),
        jnp.where(r8 == 2, t1[4] + a2[:8], ninf),
    ], axis=0)


def _peer_select_kernel(st_ref, th_ref, e1_ref, e2_ref):
    tl = st_ref.shape[2]

    def lane_group(g, carry):
        ls = pl.ds(pl.multiple_of(g * LANES, LANES), LANES)
        for h in range(PEER_HEADS):
            s1 = st_ref[2 * h, :, ls]
            s2 = st_ref[2 * h + 1, :, ls]
            t1 = _top_values(s1, PEER_TOPK)
            t2 = _top_values(s2, PEER_TOPK)
            best = _top_values(_staircase_sums(t1, t2), PEER_TOPK)
            m = best[0]
            z = functools.reduce(lambda a, b: a + b, [jnp.exp(bs - m) for bs in best])
            th_ref[h:h + 1, ls] = best[PEER_TOPK - 1]
            e1_ref[h, :, ls] = jnp.exp(s1 - t1[0]) / z
            e2_ref[h, :, ls] = jnp.exp(s2 - t2[0])
        return carry

    lax.fori_loop(0, tl // LANES, lane_group, 0)


def _peer_select(st):
    t = st.shape[2]
    tl = min(SEL_TL, t)
    e_shape = jax.ShapeDtypeStruct((PEER_HEADS, N_KEYS, t), F32)
    return pl.pallas_call(
        _peer_select_kernel,
        out_shape=(jax.ShapeDtypeStruct((PEER_HEADS, t), F32), e_shape, e_shape),
        grid=(t // tl,),
        in_specs=[pl.BlockSpec((2 * PEER_HEADS, N_KEYS, tl), lambda i: (0, 0, i))],
        out_specs=(
            pl.BlockSpec((PEER_HEADS, tl), lambda i: (0, i)),
            pl.BlockSpec((PEER_HEADS, N_KEYS, tl), lambda i: (0, 0, i)),
            pl.BlockSpec((PEER_HEADS, N_KEYS, tl), lambda i: (0, 0, i)),
        ),
        compiler_params=_cparams(("parallel",)),
        name="peer_select",
    )(st)


def _peer_dense_kernel(h2_ref, u_ref, vt_ref, s2_ref, s1row_ref, e1row_ref, th_ref, e2_ref, o_ref,
                       act_scr, p_scr, acc_scr):
    ei = pl.program_id(1)
    te, tt = act_scr.shape

    @pl.when(ei == 0)
    def _():
        acc_scr[...] = jnp.zeros(acc_scr.shape, F32)

    act_scr[...] = lax.dot_general(u_ref[...], h2_ref[...], (((1,), (1,)), ((), ())),
                                   preferred_element_type=F32)
    for ii in range(PEER_ROWS):
        rs = slice(ii * N_KEYS, (ii + 1) * N_KEYS)
        for lc in range(tt // LANES):
            ls = slice(lc * LANES, (lc + 1) * LANES)
            coef = jnp.zeros((N_KEYS, LANES), F32)
            for h in range(PEER_HEADS):
                s1row = s1row_ref[h, ii:ii + 1, ls]
                e1row = e1row_ref[h, ii:ii + 1, ls]
                keep = (s2_ref[h, :, ls] + s1row) >= th_ref[h:h + 1, ls]
                coef = coef + jnp.where(keep, e2_ref[h, :, ls] * e1row, 0.0)
            p_scr[rs, ls] = (coef * jax.nn.gelu(act_scr[rs, ls])).astype(BF16)
    acc_scr[...] += jnp.dot(vt_ref[...], p_scr[...], preferred_element_type=F32)

    @pl.when(ei == pl.num_programs(1) - 1)
    def _():
        o_ref[...] = acc_scr[...].T


def _peer_dense(h2, u_bf, vt_bf, st, th, e1, e2):
    t, d = h2.shape
    n_exp = u_bf.shape[0]
    tt = min(PEER_TT, t)
    te = PEER_ROWS * N_KEYS
    st4 = st.reshape(PEER_HEADS, 2, N_KEYS, t)
    return pl.pallas_call(
        _peer_dense_kernel,
        out_shape=jax.ShapeDtypeStruct((t, d), F32),
        grid=(t // tt, n_exp // te),
        in_specs=[
            pl.BlockSpec((tt, d), lambda i, e: (i, 0)),
            pl.BlockSpec((te, d), lambda i, e: (e, 0)),
            pl.BlockSpec((d, te), lambda i, e: (0, e)),
            pl.BlockSpec((PEER_HEADS, None, N_KEYS, tt), lambda i, e: (0, 1, 0, i)),
            pl.BlockSpec((PEER_HEADS, None, PEER_ROWS, tt), lambda i, e: (0, 0, e, i)),
            pl.BlockSpec((PEER_HEADS, PEER_ROWS, tt), lambda i, e: (0, e, i)),
            pl.BlockSpec((PEER_HEADS, tt), lambda i, e: (0, i)),
            pl.BlockSpec((PEER_HEADS, N_KEYS, tt), lambda i, e: (0, 0, i)),
        ],
        out_specs=pl.BlockSpec((tt, d), lambda i, e: (i, 0)),
        scratch_shapes=[
            pltpu.VMEM((te, tt), F32),
            pltpu.VMEM((te, tt), BF16),
            pltpu.VMEM((d, tt), F32),
        ],
        compiler_params=_cparams(("parallel", "arbitrary")),
        name="peer_dense",
    )(h2, u_bf, vt_bf, st4, st4, e1, th, e2)


def _final_norm_kernel(x1_ref, p_ref, g_ref, o_ref):
    y = x1_ref[...] + p_ref[...]
    ms = jnp.mean(y * y, axis=-1, keepdims=True)
    o_ref[...] = (y * lax.rsqrt(ms + EPS)) * g_ref[...]


def _final_norm(x1, peer_out, g):
    t, d = x1.shape
    tt = min(NORM_TT, t)
    return pl.pallas_call(
        _final_norm_kernel,
        out_shape=jax.ShapeDtypeStruct((t, d), F32),
        grid=(t // tt,),
        in_specs=[
            pl.BlockSpec((tt, d), lambda i: (i, 0)),
            pl.BlockSpec((tt, d), lambda i: (i, 0)),
            pl.BlockSpec((1, d), lambda i: (0, 0)),
        ],
        out_specs=pl.BlockSpec((tt, d), lambda i: (i, 0)),
        compiler_params=_cparams(("parallel",)),
        name="final_norm",
    )(x1, peer_out, g.reshape(1, d))


def _layer(x2, b, s, ln_mix_g, w_in, gmlp_norm_g, w_spatial, b_spatial, w_branch_attn, w_branch_gmlp,
           w_out, ln_ffn_g, peer_w_q, peer_sub_keys, peer_u, peer_v):
    t, d = x2.shape
    n_qi = IDX_HEADS * IDX_DIM
    o_qi = 3 * ATTN_WIDTH
    o_ki = o_qi + n_qi
    o_wi = o_ki + IDX_DIM
    o_gate = o_wi + IDX_HEADS

    w_attn = w_in[:, :o_ki].astype(BF16)
    w_idx = jnp.pad(w_in[:, o_ki:o_gate], ((0, 0), (0, LANES - IDX_DIM - IDX_HEADS))).astype(BF16)
    w_gate = w_in[:, o_gate:].astype(BF16)

    attn_in = _norm_proj(x2, ln_mix_g, w_attn, BF16, ATTN_TN)
    idx_in = _norm_proj(x2, ln_mix_g, w_idx, F32, LANES)
    gates = _norm_proj(x2, ln_mix_g, w_gate, F32, GATE_TN)

    q = attn_in[:, :ATTN_WIDTH].reshape(b, s, ATTN_WIDTH)
    k = attn_in[:, ATTN_WIDTH:2 * ATTN_WIDTH].reshape(b, s, ATTN_WIDTH)
    v = attn_in[:, 2 * ATTN_WIDTH:o_qi].reshape(b, s, ATTN_WIDTH)
    nq = s // TQ
    qi_r = attn_in[:, o_qi:o_ki].reshape(b, nq, TQ, IDX_HEADS, IDX_DIM)
    qi_r = qi_r.transpose(0, 1, 3, 2, 4).reshape(b, nq, IDX_HEADS * TQ, IDX_DIM)
    kit = idx_in[:, :IDX_DIM].astype(BF16).reshape(b, s, IDX_DIM).transpose(0, 2, 1)
    wi_r = idx_in[:, IDX_DIM:IDX_DIM + IDX_HEADS].reshape(b, nq, TQ, IDX_HEADS)
    wi_r = wi_r.transpose(0, 1, 3, 2).reshape(b, nq, IDX_HEADS * TQ, 1)

    n_sel = min(DSA_TOPK, s // 4)
    bias = _dsa_select(qi_r, wi_r, kit, n_sel)
    ya = _dsa_attend(q, k, v, bias).reshape(t, ATTN_WIDTH)

    mixed = _gmlp_mix(gates, ya, gmlp_norm_g, w_spatial, b_spatial.T, w_branch_attn.astype(BF16),
                      w_branch_gmlp.astype(BF16), d)
    x1, h2 = _out_proj(x2, mixed, w_out.astype(BF16), ln_ffn_g)

    half = peer_sub_keys.shape[-1]
    sk = peer_sub_keys.reshape(2 * PEER_HEADS, N_KEYS, half).astype(BF16)
    st = _peer_scores(h2, peer_w_q.astype(BF16), sk)
    th, e1, e2 = _peer_select(st)
    peer_out = _peer_dense(h2, peer_u.astype(BF16), peer_v.astype(BF16).T, st, th, e1, e2)
    return x1, peer_out


def kernel(x, ln_mix_g, w_in, gmlp_norm_g, w_spatial, b_spatial, w_branch_attn, w_branch_gmlp, w_out, ln_ffn_g, peer_w_q, peer_sub_keys, peer_u, peer_v, ln_final_g):
    b, s, d = x.shape
    depth = w_in.shape[0]
    x2 = x.reshape(b * s, d)
    for l in range(depth):
        x1, peer_out = _layer(x2, b, s, ln_mix_g[l], w_in[l], gmlp_norm_g[l], w_spatial[l], b_spatial[l],
                              w_branch_attn[l], w_branch_gmlp[l], w_out[l], ln_ffn_g[l], peer_w_q[l],
                              peer_sub_keys[l], peer_u[l], peer_v[l])
        if l + 1 < depth:
            x2 = x1 + peer_out
    return _final_norm(x1, peer_out, ln_final_g).reshape(b, s, d)
```

```python
import functools
import math

import numpy as np
import jax
import jax.numpy as jnp
from jax import lax
from jax.experimental import pallas as pl
from jax.experimental.pallas import tpu as pltpu

F32 = jnp.float32
BF16 = jnp.bfloat16
I32 = jnp.int32

ATTN_HEADS = 8
HEAD_DIM = 128
ATTN_WIDTH = ATTN_HEADS * HEAD_DIM
IDX_HEADS = 8
IDX_DIM = 64
DSA_TOPK = 256
GMLP_GROUPS = 8
CHUNK = 128
GMLP_WIDTH = GMLP_GROUPS * CHUNK
N_KEYS = 128
PEER_HEADS = 8
PEER_TOPK = 16
EPS = 1e-6

LANES = 128
MXU_COLS = 256
VMEM_LIMIT_BYTES = 56 * 1024 * 1024

INT_MIN = -(2 ** 31)
NEG_BIG = -1e30

TQ = 128
TK = 512
SCAN_KEYS = 2048
ATQ = 512
ATK = 512
PROJ_TM = 1024
ATTN_TN = 896
GATE_TN = 1024
NORM_TT = 512
MIX_TT = 256
PEER_TT = 512
PEER_ROWS = 8
SEL_TL = 512


def _cparams(sem):
    return pltpu.CompilerParams(dimension_semantics=sem, vmem_limit_bytes=VMEM_LIMIT_BYTES)


def _norm_proj_kernel(x_ref, g_ref, w_ref, o_ref, h_scr):
    @pl.when(pl.program_id(1) == 0)
    def _():
        x = x_ref[...]
        ms = jnp.mean(x * x, axis=-1, keepdims=True)
        h_scr[...] = ((x * lax.rsqrt(ms + EPS)) * g_ref[...]).astype(BF16)

    o_ref[...] = jnp.dot(h_scr[...], w_ref[...], preferred_element_type=F32).astype(o_ref.dtype)


def _norm_proj(x2, g, w_bf, out_dtype, tn):
    t, d = x2.shape
    n = w_bf.shape[1]
    tm = min(PROJ_TM, t)
    return pl.pallas_call(
        _norm_proj_kernel,
        out_shape=jax.ShapeDtypeStruct((t, n), out_dtype),
        grid=(t // tm, n // tn),
        in_specs=[
            pl.BlockSpec((tm, d), lambda i, j: (i, 0)),
            pl.BlockSpec((1, d), lambda i, j: (0, 0)),
            pl.BlockSpec((d, tn), lambda i, j: (0, j)),
        ],
        out_specs=pl.BlockSpec((tm, tn), lambda i, j: (i, j)),
        scratch_shapes=[pltpu.VMEM((tm, d), BF16)],
        compiler_params=_cparams(("parallel", "arbitrary")),
        name="norm_proj",
    )(x2, g.reshape(1, d), w_bf)


def _sort_key(x):
    bits = pltpu.bitcast(x, I32)
    return bits ^ ((bits >> 31) & 0x7FFFFFFF)


def _dsa_select_kernel(qi_ref, wi_ref, kit_ref, bias_ref, keys_scr, wb_scr, *, n_sel, seq):
    i = pl.program_id(1)
    n_kt = (i * TQ + TQ + TK - 1) // TK
    wb_scr[...] = jnp.broadcast_to((wi_ref[...] * (IDX_HEADS ** -0.5)) * (IDX_DIM ** -0.5), wb_scr.shape)
    q_pos = i * TQ + lax.broadcasted_iota(I32, (TQ, LANES), 0)
    lane = lax.broadcasted_iota(I32, (TQ, LANES), 1)

    def chunk_at(off):
        return pl.ds(pl.multiple_of(off, LANES), LANES)

    def score_body(kt, carry):
        m1, m2 = carry
        off = pl.multiple_of(kt * TK, TK)
        accs = [None] * (TK // LANES)
        for half in range(TK // MXU_COLS):
            kit_tile = kit_ref[:, pl.ds(pl.multiple_of(off + half * MXU_COLS, MXU_COLS), MXU_COLS)]
            for h in range(IDX_HEADS):
                hs = slice(h * TQ, (h + 1) * TQ)
                dots = jnp.maximum(jnp.dot(qi_ref[hs, :], kit_tile, preferred_element_type=F32), 0.0)
                for cc in range(MXU_COLS // LANES):
                    c = half * (MXU_COLS // LANES) + cc
                    term = dots[:, cc * LANES:(cc + 1) * LANES] * wb_scr[hs, :]
                    accs[c] = term if accs[c] is None else accs[c] + term
        for c in range(TK // LANES):
            sc = accs[c] + 0.0
            causal = (off + c * LANES + lane) <= q_pos
            scm = jnp.where(causal, sc, -jnp.inf)
            m2 = jnp.maximum(m2, jnp.minimum(m1, scm))
            m1 = jnp.maximum(m1, scm)
            keys_scr[:, chunk_at(off + c * LANES)] = jnp.where(causal, _sort_key(sc), INT_MIN)
        return m1, m2

    neg_inf = jnp.full((TQ, LANES), -jnp.inf, F32)
    m1, m2 = lax.fori_loop(0, n_kt, score_body, (neg_inf, neg_inf))

    scan = SCAN_KEYS if seq % SCAN_KEYS == 0 else TK
    n_st = (n_kt * TK + scan - 1) // scan
    kt_end = n_st * (scan // TK)

    def pad_body(kt, carry):
        keys_scr[:, pl.ds(pl.multiple_of(kt * TK, TK), TK)] = jnp.full((TQ, TK), INT_MIN, I32)
        return carry

    lax.fori_loop(n_kt, kt_end, pad_body, 0)

    def count(preds):
        def body(st, accs):
            off = pl.multiple_of(st * scan, scan)
            for c in range(scan // LANES):
                blk = keys_scr[:, chunk_at(off + c * LANES)]
                accs = tuple(a + p(blk, off + c * LANES) for a, p in zip(accs, preds))
            return accs
        accs = lax.fori_loop(0, n_st, body, tuple(jnp.zeros((TQ, LANES), I32) for _ in preds))
        return [jnp.sum(a.astype(F32), axis=1, keepdims=True).astype(I32) for a in accs]

    low_f = jnp.min(m2, axis=1, keepdims=True)
    low_key = _sort_key(jnp.broadcast_to(low_f, (TQ, LANES)))[:, :1]
    lo0 = jnp.where(low_f == -jnp.inf, INT_MIN, low_key)
    hi0 = _sort_key(jnp.broadcast_to(jnp.max(m1, axis=1, keepdims=True), (TQ, LANES)))[:, :1] + 1

    def bisect_body(state):
        lo, hi, _ = state
        mid = (lo >> 1) + (hi >> 1) + (lo & hi & 1)
        mb = jnp.broadcast_to(mid, (TQ, LANES))
        c, = count([lambda blk, off: (blk >= mb).astype(I32)])
        ge = c >= n_sel
        lo_n = jnp.where(ge, mid, lo)
        hi_n = jnp.where(c == n_sel, mid + 1, jnp.where(ge, hi, mid))
        return lo_n, hi_n, jnp.max(jnp.where(hi_n != lo_n + 1, 1.0, 0.0)).astype(F32)

    t, _, _ = lax.while_loop(lambda st: st[2] > 0.0, bisect_body, (lo0, hi0, jnp.float32(1.0)))
    tb = jnp.broadcast_to(t, (TQ, LANES))

    n_gt, n_eq = count([lambda blk, off: (blk > tb).astype(I32), lambda blk, off: (blk == tb).astype(I32)])
    need = n_sel - n_gt
    ambiguous = jnp.max(jnp.where((n_eq > need) & (t > INT_MIN), 1.0, 0.0))

    def write_bias(keep_fn):
        def body(st, carry):
            off = pl.multiple_of(st * scan, scan)
            for c in range(scan // LANES):
                blk = keys_scr[:, chunk_at(off + c * LANES)]
                bias_ref[:, chunk_at(off + c * LANES)] = jnp.where(
                    keep_fn(blk, off + c * LANES), 0.0, NEG_BIG).astype(BF16)
            return carry
        lax.fori_loop(0, n_st, body, 0)

    @pl.when(ambiguous <= 0)
    def _():
        thr = jnp.broadcast_to(jnp.maximum(t, INT_MIN + 1), (TQ, LANES))
        write_bias(lambda blk, off: blk >= thr)

    @pl.when(ambiguous > 0)
    def _():
        n_bits = max(1, (seq - 1).bit_length())

        def jbit_body(k, x):
            cand = x + lax.shift_left(jnp.int32(1), jnp.int32(n_bits - 1) - k)
            xb = jnp.broadcast_to(cand, (TQ, LANES))
            below, = count([lambda blk, off: jnp.where(blk == tb, ((off + lane) < xb).astype(I32), 0)])
            return jnp.where(below < need, cand, x)

        x = lax.fori_loop(0, n_bits, jbit_body, jnp.zeros((TQ, 1), I32))
        cut = jnp.where(t == INT_MIN, -1, jnp.where(n_eq > need, x, seq))
        jb = jnp.broadcast_to(cut, (TQ, LANES))
        write_bias(lambda blk, off: jnp.where(blk > tb, 1, jnp.where(blk == tb, ((off + lane) <= jb).astype(I32), 0)) > 0)

    def fill_body(kt, carry):
        bias_ref[:, pl.ds(pl.multiple_of(kt * TK, TK), TK)] = jnp.full((TQ, TK), NEG_BIG, BF16)
        return carry

    lax.fori_loop(kt_end, seq // TK, fill_body, 0)


def _dsa_select(qi_r, wi_r, kit, n_sel):
    b, nq = qi_r.shape[0], qi_r.shape[1]
    s = kit.shape[2]
    assert n_sel <= 2 * LANES and s % TK == 0
    return pl.pallas_call(
        functools.partial(_dsa_select_kernel, n_sel=n_sel, seq=s),
        out_shape=jax.ShapeDtypeStruct((b, s, s), BF16),
        grid=(b, nq),
        in_specs=[
            pl.BlockSpec((None, None, IDX_HEADS * TQ, IDX_DIM), lambda bb, i: (bb, i, 0, 0)),
            pl.BlockSpec((None, None, IDX_HEADS * TQ, 1), lambda bb, i: (bb, i, 0, 0)),
            pl.BlockSpec((None, IDX_DIM, s), lambda bb, i: (bb, 0, 0)),
        ],
        out_specs=pl.BlockSpec((None, TQ, s), lambda bb, i: (bb, i, 0)),
        scratch_shapes=[
            pltpu.VMEM((TQ, s + LANES), I32),
            pltpu.VMEM((IDX_HEADS * TQ, LANES), F32),
        ],
        compiler_params=_cparams(("parallel", "arbitrary")),
        name="dsa_select",
    )(qi_r, wi_r, kit)


def _dsa_attend_kernel(qtab_ref, ktab_ref, q_ref, k_ref, v_ref, bias_ref, o_ref, m_scr, acc_scr):
    step = pl.program_id(1)
    i = qtab_ref[step]
    kt = ktab_ref[step]
    last_kt = (i * ATQ + ATQ - 1) // ATK

    @pl.when(kt == 0)
    def _():
        m_scr[...] = jnp.full(m_scr.shape, NEG_BIG, F32)
        acc_scr[...] = jnp.zeros(acc_scr.shape, F32)

    bias = bias_ref[...].astype(F32)
    log2e_scale = (HEAD_DIM ** -0.5) * math.log2(math.e)
    ones = jnp.ones((ATK, HEAD_DIM), BF16)
    for h in range(ATTN_HEADS):
        hs = slice(h * HEAD_DIM, (h + 1) * HEAD_DIM)
        logits = lax.dot_general(q_ref[:, hs], k_ref[:, hs], (((1,), (1,)), ((), ())),
                                 preferred_element_type=F32)
        s = logits * log2e_scale + bias
        m_old = m_scr[h]
        m_new = jnp.maximum(m_old, jnp.max(s, axis=1, keepdims=True))
        alpha = jnp.exp2(m_old - m_new)
        p = jnp.concatenate([jnp.exp2(s[:, c * LANES:(c + 1) * LANES] - m_new) for c in range(ATK // LANES)],
                            axis=1).astype(BF16)
        v_ext = jnp.concatenate([v_ref[:, hs], ones], axis=1)
        pv = jnp.dot(p, v_ext, preferred_element_type=F32)
        acc_scr[h] = jnp.concatenate([alpha, alpha], axis=1) * acc_scr[h] + pv
        m_scr[h] = m_new

    @pl.when(kt == last_kt)
    def _():
        for h in range(ATTN_HEADS):
            acc = acc_scr[h]
            o_ref[:, h * HEAD_DIM:(h + 1) * HEAD_DIM] = (acc[:, :HEAD_DIM] / acc[:, HEAD_DIM:]).astype(o_ref.dtype)


def _dsa_attend(q, k, v, bias):
    b, s, _ = q.shape
    atq = min(ATQ, s)
    assert atq == ATQ and s % ATQ == 0
    qtab, ktab = [], []
    for i in range(s // ATQ):
        for kt in range((i * ATQ + ATQ - 1) // ATK + 1):
            qtab.append(i)
            ktab.append(kt)
    grid_spec = pltpu.PrefetchScalarGridSpec(
        num_scalar_prefetch=2,
        grid=(b, len(qtab)),
        in_specs=[
            pl.BlockSpec((None, ATQ, ATTN_WIDTH), lambda bb, st, qt, kk: (bb, qt[st], 0)),
            pl.BlockSpec((None, ATK, ATTN_WIDTH), lambda bb, st, qt, kk: (bb, kk[st], 0)),
            pl.BlockSpec((None, ATK, ATTN_WIDTH), lambda bb, st, qt, kk: (bb, kk[st], 0)),
            pl.BlockSpec((None, ATQ, ATK), lambda bb, st, qt, kk: (bb, qt[st], kk[st])),
        ],
        out_specs=pl.BlockSpec((None, ATQ, ATTN_WIDTH), lambda bb, st, qt, kk: (bb, qt[st], 0)),
        scratch_shapes=[
            pltpu.VMEM((ATTN_HEADS, ATQ, LANES), F32),
            pltpu.VMEM((ATTN_HEADS, ATQ, 2 * HEAD_DIM), F32),
        ],
    )
    return pl.pallas_call(
        _dsa_attend_kernel,
        out_shape=jax.ShapeDtypeStruct((b, s, ATTN_WIDTH), BF16),
        grid_spec=grid_spec,
        compiler_params=_cparams(("parallel", "arbitrary")),
        name="dsa_attend",
    )(jnp.asarray(qtab, I32), jnp.asarray(ktab, I32), q, k, v, bias)


def _gmlp_mix_kernel(gu_ref, gv_ref, ga_ref, gb_ref, ya_ref, ng_ref, ws_ref, bst_ref, wa_ref, wb_ref,
                     o_ref, yb_scr):
    tt = gu_ref.shape[0]
    row = lax.broadcasted_iota(I32, (CHUNK, CHUNK), 0)
    col = lax.broadcasted_iota(I32, (CHUNK, CHUNK), 1)
    tril = col <= row
    for c in range(tt // CHUNK):
        rs = slice(c * CHUNK, (c + 1) * CHUNK)
        u = jax.nn.gelu(gu_ref[rs, :])
        v = jax.nn.gelu(gv_ref[rs, :])
        v = (v * lax.rsqrt(jnp.mean(v * v, axis=-1, keepdims=True) + EPS)) * ng_ref[...]
        vb = v.astype(BF16)
        for g in range(GMLP_GROUPS):
            gs = slice(g * CHUNK, (g + 1) * CHUNK)
            wm = jnp.where(tril, ws_ref[g], 0.0).astype(BF16)
            z = jnp.dot(wm, vb[:, gs], preferred_element_type=F32) + bst_ref[:, g:g + 1]
            yb_scr[rs, gs] = (u[:, gs] * z).astype(BF16)
    ma = jnp.dot(ya_ref[...], wa_ref[...], preferred_element_type=F32)
    mb = jnp.dot(yb_scr[...], wb_ref[...], preferred_element_type=F32)
    o_ref[...] = (jax.nn.sigmoid(ga_ref[...]) * ma + jax.nn.sigmoid(gb_ref[...]) * mb).astype(o_ref.dtype)


def _gmlp_mix(gates, ya, norm_g, w_s, b_s_t, wa_bf, wb_bf, d_model):
    t = ya.shape[0]
    tt = min(MIX_TT, t)
    assert d_model == 2 * GMLP_WIDTH
    return pl.pallas_call(
        _gmlp_mix_kernel,
        out_shape=jax.ShapeDtypeStruct((t, d_model), BF16),
        grid=(t // tt,),
        in_specs=[
            pl.BlockSpec((tt, GMLP_WIDTH), lambda i: (i, 0)),
            pl.BlockSpec((tt, GMLP_WIDTH), lambda i: (i, 1)),
            pl.BlockSpec((tt, d_model), lambda i: (i, 1)),
            pl.BlockSpec((tt, d_model), lambda i: (i, 2)),
            pl.BlockSpec((tt, ATTN_WIDTH), lambda i: (i, 0)),
            pl.BlockSpec((1, GMLP_WIDTH), lambda i: (0, 0)),
            pl.BlockSpec((GMLP_GROUPS, CHUNK, CHUNK), lambda i: (0, 0, 0)),
            pl.BlockSpec((CHUNK, GMLP_GROUPS), lambda i: (0, 0)),
            pl.BlockSpec((ATTN_WIDTH, d_model), lambda i: (0, 0)),
            pl.BlockSpec((GMLP_WIDTH, d_model), lambda i: (0, 0)),
        ],
        out_specs=pl.BlockSpec((tt, d_model), lambda i: (i, 0)),
        scratch_shapes=[pltpu.VMEM((tt, GMLP_WIDTH), BF16)],
        compiler_params=_cparams(("parallel",)),
        name="gmlp_mix",
    )(gates, gates, gates, gates, ya, norm_g.reshape(1, GMLP_WIDTH), w_s, b_s_t, wa_bf, wb_bf)


def _out_proj_kernel(x_ref, m_ref, wo_ref, g_ref, x1_ref, h2_ref):
    x1 = x_ref[...] + jnp.dot(m_ref[...], wo_ref[...], preferred_element_type=F32)
    x1_ref[...] = x1
    ms = jnp.mean(x1 * x1, axis=-1, keepdims=True)
    h2_ref[...] = ((x1 * lax.rsqrt(ms + EPS)) * g_ref[...]).astype(BF16)


def _out_proj(x2, mixed, wo_bf, g):
    t, d = x2.shape
    tt = min(MIX_TT, t)
    return pl.pallas_call(
        _out_proj_kernel,
        out_shape=(jax.ShapeDtypeStruct((t, d), F32), jax.ShapeDtypeStruct((t, d), BF16)),
        grid=(t // tt,),
        in_specs=[
            pl.BlockSpec((tt, d), lambda i: (i, 0)),
            pl.BlockSpec((tt, d), lambda i: (i, 0)),
            pl.BlockSpec((d, d), lambda i: (0, 0)),
            pl.BlockSpec((1, d), lambda i: (0, 0)),
        ],
        out_specs=(pl.BlockSpec((tt, d), lambda i: (i, 0)), pl.BlockSpec((tt, d), lambda i: (i, 0))),
        compiler_params=_cparams(("parallel",)),
        name="out_proj",
    )(x2, mixed, wo_bf, g.reshape(1, d))


def _peer_scores_kernel(h2_ref, wq_ref, sk_ref, st_ref):
    qp = jnp.dot(h2_ref[...], wq_ref[...], preferred_element_type=F32).astype(BF16)
    half = sk_ref.shape[2]
    for hp in range(2 * PEER_HEADS):
        st_ref[hp] = lax.dot_general(sk_ref[hp], qp[:, hp * half:(hp + 1) * half], (((1,), (1,)), ((), ())),
                                     preferred_element_type=F32)


def _peer_scores(h2, wq_bf, sk_bf):
    t, d = h2.shape
    tt = min(MIX_TT, t)
    nq = wq_bf.shape[1]
    half = sk_bf.shape[2]
    return pl.pallas_call(
        _peer_scores_kernel,
        out_shape=jax.ShapeDtypeStruct((2 * PEER_HEADS, N_KEYS, t), F32),
        grid=(t // tt,),
        in_specs=[
            pl.BlockSpec((tt, d), lambda i: (i, 0)),
            pl.BlockSpec((d, nq), lambda i: (0, 0)),
            pl.BlockSpec((2 * PEER_HEADS, N_KEYS, half), lambda i: (0, 0, 0)),
        ],
        out_specs=pl.BlockSpec((2 * PEER_HEADS, N_KEYS, tt), lambda i: (0, 0, i)),
        compiler_params=_cparams(("parallel",)),
        name="peer_scores",
    )(h2, wq_bf, sk_bf)


def _top_values(cur, k):
    n = cur.shape[0]
    idx = lax.broadcasted_iota(I32, cur.shape, 0).astype(F32)
    vals = []
    for _ in range(k):
        mx = jnp.max(cur, axis=0, keepdims=True)
        first = jnp.min(jnp.where(cur == mx, idx, float(n)), axis=0, keepdims=True)
        vals.append(mx)
        cur = jnp.where(idx == first, -jnp.inf, cur)
    return vals


def _staircase_sums(t1, t2):
    a1 = jnp.concatenate(t1, axis=0)
    a2 = jnp.concatenate(t2, axis=0)
    r16 = lax.broadcasted_iota(I32, (PEER_TOPK, LANES), 0)
    r8 = r16[:8]
    ninf = -jnp.inf
    return jnp.concatenate([
        t1[0] + a2,
        t1[1] + a2[:8],
        jnp.where(r16 >= 2, a1 + t2[0], ninf),
        jnp.where(r8 >= 2, a1[:8] + t2[1], ninf),
        jnp.where((r8 >= 2) & (r8 <= 4), t1[2] + a2[:8], ninf),
        jnp.where((r8 >= 2) & (r8 <= 3), t1[3] + a2[:8], ninf),
        jnp.where(r8 == 2, t1[4] + a2[:8], ninf),
    ], axis=0)


def _peer_select_kernel(st_ref, th_ref, e1_ref, e2_ref):
    tl = st_ref.shape[2]

    def lane_group(g, carry):
        ls = pl.ds(pl.multiple_of(g * LANES, LANES), LANES)
        for h in range(PEER_HEADS):
            s1 = st_ref[2 * h, :, ls]
            s2 = st_ref[2 * h + 1, :, ls]
            t1 = _top_values(s1, PEER_TOPK)
            t2 = _top_values(s2, PEER_TOPK)
            best = _top_values(_staircase_sums(t1, t2), PEER_TOPK)
            m = best[0]
            z = functools.reduce(lambda a, b: a + b, [jnp.exp(bs - m) for bs in best])
            th_ref[h:h + 1, ls] = best[PEER_TOPK - 1]
            e1_ref[h, :, ls] = jnp.exp(s1 - t1[0]) / z
            e2_ref[h, :, ls] = jnp.exp(s2 - t2[0])
        return carry

    lax.fori_loop(0, tl // LANES, lane_group, 0)


def _peer_select(st):
    t = st.shape[2]
    tl = min(SEL_TL, t)
    e_shape = jax.ShapeDtypeStruct((PEER_HEADS, N_KEYS, t), F32)
    return pl.pallas_call(
        _peer_select_kernel,
        out_shape=(jax.ShapeDtypeStruct((PEER_HEADS, t), F32), e_shape, e_shape),
        grid=(t // tl,),
        in_specs=[pl.BlockSpec((2 * PEER_HEADS, N_KEYS, tl), lambda i: (0, 0, i))],
        out_specs=(
            pl.BlockSpec((PEER_HEADS, tl), lambda i: (0, i)),
            pl.BlockSpec((PEER_HEADS, N_KEYS, tl), lambda i: (0, 0, i)),
            pl.BlockSpec((PEER_HEADS, N_KEYS, tl), lambda i: (0, 0, i)),
        ),
        compiler_params=_cparams(("parallel",)),
        name="peer_select",
    )(st)


def _peer_dense_kernel(h2_ref, u_ref, vt_ref, s2_ref, s1row_ref, e1row_ref, th_ref, e2_ref, o_ref,
                       act_scr, p_scr, acc_scr):
    ei = pl.program_id(1)
    te, tt = act_scr.shape

    @pl.when(ei == 0)
    def _():
        acc_scr[...] = jnp.zeros(acc_scr.shape, F32)

    act_scr[...] = lax.dot_general(u_ref[...], h2_ref[...], (((1,), (1,)), ((), ())),
                                   preferred_element_type=F32)
    for ii in range(PEER_ROWS):
        rs = slice(ii * N_KEYS, (ii + 1) * N_KEYS)
        for lc in range(tt // LANES):
            ls = slice(lc * LANES, (lc + 1) * LANES)
            coef = jnp.zeros((N_KEYS, LANES), F32)
            for h in range(PEER_HEADS):
                s1row = s1row_ref[h, ii:ii + 1, ls]
                e1row = e1row_ref[h, ii:ii + 1, ls]
                keep = (s2_ref[h, :, ls] + s1row) >= th_ref[h:h + 1, ls]
                coef = coef + jnp.where(keep, e2_ref[h, :, ls] * e1row, 0.0)
            p_scr[rs, ls] = (coef * jax.nn.gelu(act_scr[rs, ls])).astype(BF16)
    acc_scr[...] += jnp.dot(vt_ref[...], p_scr[...], preferred_element_type=F32)

    @pl.when(ei == pl.num_programs(1) - 1)
    def _():
        o_ref[...] = acc_scr[...].T


def _peer_dense(h2, u_bf, vt_bf, st, th, e1, e2):
    t, d = h2.shape
    n_exp = u_bf.shape[0]
    tt = min(PEER_TT, t)
    te = PEER_ROWS * N_KEYS
    st4 = st.reshape(PEER_HEADS, 2, N_KEYS, t)
    return pl.pallas_call(
        _peer_dense_kernel,
        out_shape=jax.ShapeDtypeStruct((t, d), F32),
        grid=(t // tt, n_exp // te),
        in_specs=[
            pl.BlockSpec((tt, d), lambda i, e: (i, 0)),
            pl.BlockSpec((te, d), lambda i, e: (e, 0)),
            pl.BlockSpec((d, te), lambda i, e: (0, e)),
            pl.BlockSpec((PEER_HEADS, None, N_KEYS, tt), lambda i, e: (0, 1, 0, i)),
            pl.BlockSpec((PEER_HEADS, None, PEER_ROWS, tt), lambda i, e: (0, 0, e, i)),
            pl.BlockSpec((PEER_HEADS, PEER_ROWS, tt), lambda i, e: (0, e, i)),
            pl.BlockSpec((PEER_HEADS, tt), lambda i, e: (0, i)),
            pl.BlockSpec((PEER_HEADS, N_KEYS, tt), lambda i, e: (0, 0, i)),
        ],
        out_specs=pl.BlockSpec((tt, d), lambda i, e: (i, 0)),
        scratch_shapes=[
            pltpu.VMEM((te, tt), F32),
            pltpu.VMEM((te, tt), BF16),
            pltpu.VMEM((d, tt), F32),
        ],
        compiler_params=_cparams(("parallel", "arbitrary")),
        name="peer_dense",
    )(h2, u_bf, vt_bf, st4, st4, e1, th, e2)


def _final_norm_kernel(x1_ref, p_ref, g_ref, o_ref):
    y = x1_ref[...] + p_ref[...]
    ms = jnp.mean(y * y, axis=-1, keepdims=True)
    o_ref[...] = (y * lax.rsqrt(ms + EPS)) * g_ref[...]


def _final_norm(x1, peer_out, g):
    t, d = x1.shape
    tt = min(NORM_TT, t)
    return pl.pallas_call(
        _final_norm_kernel,
        out_shape=jax.ShapeDtypeStruct((t, d), F32),
        grid=(t // tt,),
        in_specs=[
            pl.BlockSpec((tt, d), lambda i: (i, 0)),
            pl.BlockSpec((tt, d), lambda i: (i, 0)),
            pl.BlockSpec((1, d), lambda i: (0, 0)),
        ],
        out_specs=pl.BlockSpec((tt, d), lambda i: (i, 0)),
        compiler_params=_cparams(("parallel",)),
        name="final_norm",
    )(x1, peer_out, g.reshape(1, d))


def _layer(x2, b, s, ln_mix_g, w_in, gmlp_norm_g, w_spatial, b_spatial, w_branch_attn, w_branch_gmlp,
           w_out, ln_ffn_g, peer_w_q, peer_sub_keys, peer_u, peer_v):
    t, d = x2.shape
    n_qi = IDX_HEADS * IDX_DIM
    o_qi = 3 * ATTN_WIDTH
    o_ki = o_qi + n_qi
    o_wi = o_ki + IDX_DIM
    o_gate = o_wi + IDX_HEADS

    w_attn = w_in[:, :o_ki].astype(BF16)
    w_idx = jnp.pad(w_in[:, o_ki:o_gate], ((0, 0), (0, LANES - IDX_DIM - IDX_HEADS))).astype(BF16)
    w_gate = w_in[:, o_gate:].astype(BF16)

    attn_in = _norm_proj(x2, ln_mix_g, w_attn, BF16, ATTN_TN)
    idx_in = _norm_proj(x2, ln_mix_g, w_idx, F32, LANES)
    gates = _norm_proj(x2, ln_mix_g, w_gate, F32, GATE_TN)

    q = attn_in[:, :ATTN_WIDTH].reshape(b, s, ATTN_WIDTH)
    k = attn_in[:, ATTN_WIDTH:2 * ATTN_WIDTH].reshape(b, s, ATTN_WIDTH)
    v = attn_in[:, 2 * ATTN_WIDTH:o_qi].reshape(b, s, ATTN_WIDTH)
    nq = s // TQ
    qi_r = attn_in[:, o_qi:o_ki].reshape(b, nq, TQ, IDX_HEADS, IDX_DIM)
    qi_r = qi_r.transpose(0, 1, 3, 2, 4).reshape(b, nq, IDX_HEADS * TQ, IDX_DIM)
    kit = idx_in[:, :IDX_DIM].astype(BF16).reshape(b, s, IDX_DIM).transpose(0, 2, 1)
    wi_r = idx_in[:, IDX_DIM:IDX_DIM + IDX_HEADS].reshape(b, nq, TQ, IDX_HEADS)
    wi_r = wi_r.transpose(0, 1, 3, 2).reshape(b, nq, IDX_HEADS * TQ, 1)

    n_sel = min(DSA_TOPK, s // 4)
    bias = _dsa_select(qi_r, wi_r, kit, n_sel)
    ya = _dsa_attend(q, k, v, bias).reshape(t, ATTN_WIDTH)

    mixed = _gmlp_mix(gates, ya, gmlp_norm_g, w_spatial, b_spatial.T, w_branch_attn.astype(BF16),
                      w_branch_gmlp.astype(BF16), d)
    x1, h2 = _out_proj(x2, mixed, w_out.astype(BF16), ln_ffn_g)

    half = peer_sub_keys.shape[-1]
    sk = peer_sub_keys.reshape(2 * PEER_HEADS, N_KEYS, half).astype(BF16)
    st = _peer_scores(h2, peer_w_q.astype(BF16), sk)
    th, e1, e2 = _peer_select(st)
    peer_out = _peer_dense(h2, peer_u.astype(BF16), peer_v.astype(BF16).T, st, th, e1, e2)
    return x1, peer_out


def kernel(x, ln_mix_g, w_in, gmlp_norm_g, w_spatial, b_spatial, w_branch_attn, w_branch_gmlp, w_out, ln_ffn_g, peer_w_q, peer_sub_keys, peer_u, peer_v, ln_final_g):
    b, s, d = x.shape
    depth = w_in.shape[0]
    x2 = x.reshape(b * s, d)
    for l in range(depth):
        x1, peer_out = _layer(x2, b, s, ln_mix_g[l], w_in[l], gmlp_norm_g[l], w_spatial[l], b_spatial[l],
                              w_branch_attn[l], w_branch_gmlp[l], w_out[l], ln_ffn_g[l], peer_w_q[l],
                              peer_sub_keys[l], peer_u[l], peer_v[l])
        if l + 1 < depth:
            x2 = x1 + peer_out
    return _final_norm(x1, peer_out, ln_final_g).reshape(b, s, d)
```

```python
import functools
import math

import numpy as np
import jax
import jax.numpy as jnp
from jax import lax
from jax.experimental import pallas as pl
from jax.experimental.pallas import tpu as pltpu

F32 = jnp.float32
BF16 = jnp.bfloat16
I32 = jnp.int32

ATTN_HEADS = 8
HEAD_DIM = 128
ATTN_WIDTH = ATTN_HEADS * HEAD_DIM
IDX_HEADS = 8
IDX_DIM = 64
DSA_TOPK = 256
GMLP_GROUPS = 8
CHUNK = 128
GMLP_WIDTH = GMLP_GROUPS * CHUNK
N_KEYS = 128
PEER_HEADS = 8
PEER_TOPK = 16
EPS = 1e-6

LANES = 128
MXU_COLS = 256
VMEM_LIMIT_BYTES = 56 * 1024 * 1024

INT_MIN = -(2 ** 31)
NEG_BIG = -1e30

TQ = 128
TK = 512
SCAN_KEYS = 2048
ATQ = 512
ATK = 512
PROJ_TM = 1024
ATTN_TN = 896
GATE_TN = 1024
NORM_TT = 512
MIX_TT = 256
PEER_TT = 512
PEER_ROWS = 8
SEL_TL = 512


def _cparams(sem):
    return pltpu.CompilerParams(dimension_semantics=sem, vmem_limit_bytes=VMEM_LIMIT_BYTES)


def _norm_proj_kernel(x_ref, g_ref, w_ref, o_ref, h_scr):
    @pl.when(pl.program_id(1) == 0)
    def _():
        x = x_ref[...]
        ms = jnp.mean(x * x, axis=-1, keepdims=True)
        h_scr[...] = ((x * lax.rsqrt(ms + EPS)) * g_ref[...]).astype(BF16)

    o_ref[...] = jnp.dot(h_scr[...], w_ref[...], preferred_element_type=F32).astype(o_ref.dtype)


def _norm_proj(x2, g, w_bf, out_dtype, tn):
    t, d = x2.shape
    n = w_bf.shape[1]
    tm = min(PROJ_TM, t)
    return pl.pallas_call(
        _norm_proj_kernel,
        out_shape=jax.ShapeDtypeStruct((t, n), out_dtype),
        grid=(t // tm, n // tn),
        in_specs=[
            pl.BlockSpec((tm, d), lambda i, j: (i, 0)),
            pl.BlockSpec((1, d), lambda i, j: (0, 0)),
            pl.BlockSpec((d, tn), lambda i, j: (0, j)),
        ],
        out_specs=pl.BlockSpec((tm, tn), lambda i, j: (i, j)),
        scratch_shapes=[pltpu.VMEM((tm, d), BF16)],
        compiler_params=_cparams(("parallel", "arbitrary")),
        name="norm_proj",
    )(x2, g.reshape(1, d), w_bf)


def _sort_key(x):
    bits = pltpu.bitcast(x, I32)
    return bits ^ ((bits >> 31) & 0x7FFFFFFF)


def _dsa_select_kernel(qi_ref, wi_ref, kit_ref, bias_ref, keys_scr, wb_scr, *, n_sel, seq):
    i = pl.program_id(1)
    n_kt = (i * TQ + TQ + TK - 1) // TK
    wb_scr[...] = jnp.broadcast_to((wi_ref[...] * (IDX_HEADS ** -0.5)) * (IDX_DIM ** -0.5), wb_scr.shape)
    q_pos = i * TQ + lax.broadcasted_iota(I32, (TQ, LANES), 0)
    lane = lax.broadcasted_iota(I32, (TQ, LANES), 1)

    def chunk_at(off):
        return pl.ds(pl.multiple_of(off, LANES), LANES)

    def score_body(kt, carry):
        m1, m2 = carry
        off = pl.multiple_of(kt * TK, TK)
        accs = [None] * (TK // LANES)
        for half in range(TK // MXU_COLS):
            kit_tile = kit_ref[:, pl.ds(pl.multiple_of(off + half * MXU_COLS, MXU_COLS), MXU_COLS)]
            for h in range(IDX_HEADS):
                hs = slice(h * TQ, (h + 1) * TQ)
                dots = jnp.maximum(jnp.dot(qi_ref[hs, :], kit_tile, preferred_element_type=F32), 0.0)
                for cc in range(MXU_COLS // LANES):
                    c = half * (MXU_COLS // LANES) + cc
                    term = dots[:, cc * LANES:(cc + 1) * LANES] * wb_scr[hs, :]
                    accs[c] = term if accs[c] is None else accs[c] + term
        for c in range(TK // LANES):
            sc = accs[c] + 0.0
            causal = (off + c * LANES + lane) <= q_pos
            scm = jnp.where(causal, sc, -jnp.inf)
            m2 = jnp.maximum(m2, jnp.minimum(m1, scm))
            m1 = jnp.maximum(m1, scm)
            keys_scr[:, chunk_at(off + c * LANES)] = jnp.where(causal, _sort_key(sc), INT_MIN)
        return m1, m2

    neg_inf = jnp.full((TQ, LANES), -jnp.inf, F32)
    m1, m2 = lax.fori_loop(0, n_kt, score_body, (neg_inf, neg_inf))

    scan = SCAN_KEYS if seq % SCAN_KEYS == 0 else TK
    n_st = (n_kt * TK + scan - 1) // scan
    kt_end = n_st * (scan // TK)

    def pad_body(kt, carry):
        keys_scr[:, pl.ds(pl.multiple_of(kt * TK, TK), TK)] = jnp.full((TQ, TK), INT_MIN, I32)
        return carry

    lax.fori_loop(n_kt, kt_end, pad_body, 0)

    def count(preds):
        def body(st, accs):
            off = pl.multiple_of(st * scan, scan)
            for c in range(scan // LANES):
                blk = keys_scr[:, chunk_at(off + c * LANES)]
                accs = tuple(a + p(blk, off + c * LANES) for a, p in zip(accs, preds))
            return accs
        accs = lax.fori_loop(0, n_st, body, tuple(jnp.zeros((TQ, LANES), I32) for _ in preds))
        return [jnp.sum(a.astype(F32), axis=1, keepdims=True).astype(I32) for a in accs]

    low_f = jnp.min(m2, axis=1, keepdims=True)
    low_key = _sort_key(jnp.broadcast_to(low_f, (TQ, LANES)))[:, :1]
    lo0 = jnp.where(low_f == -jnp.inf, INT_MIN, low_key)
    hi0 = _sort_key(jnp.broadcast_to(jnp.max(m1, axis=1, keepdims=True), (TQ, LANES)))[:, :1] + 1

    def bisect_body(state):
        lo, hi, _ = state
        mid = (lo >> 1) + (hi >> 1) + (lo & hi & 1)
        mb = jnp.broadcast_to(mid, (TQ, LANES))
        c, = count([lambda blk, off: (blk >= mb).astype(I32)])
        ge = c >= n_sel
        lo_n = jnp.where(ge, mid, lo)
        hi_n = jnp.where(c == n_sel, mid + 1, jnp.where(ge, hi, mid))
        return lo_n, hi_n, jnp.max(jnp.where(hi_n != lo_n + 1, 1.0, 0.0)).astype(F32)

    t, _, _ = lax.while_loop(lambda st: st[2] > 0.0, bisect_body, (lo0, hi0, jnp.float32(1.0)))
    tb = jnp.broadcast_to(t, (TQ, LANES))

    n_gt, n_eq = count([lambda blk, off: (blk > tb).astype(I32), lambda blk, off: (blk == tb).astype(I32)])
    need = n_sel - n_gt
    ambiguous = jnp.max(jnp.where((n_eq > need) & (t > INT_MIN), 1.0, 0.0))

    def write_bias(keep_fn):
        def body(st, carry):
            off = pl.multiple_of(st * scan, scan)
            for c in range(scan // LANES):
                blk = keys_scr[:, chunk_at(off + c * LANES)]
                bias_ref[:, chunk_at(off + c * LANES)] = jnp.where(
                    keep_fn(blk, off + c * LANES), 0.0, NEG_BIG).astype(BF16)
            return carry
        lax.fori_loop(0, n_st, body, 0)

    @pl.when(ambiguous <= 0)
    def _():
        thr = jnp.broadcast_to(jnp.maximum(t, INT_MIN + 1), (TQ, LANES))
        write_bias(lambda blk, off: blk >= thr)

    @pl.when(ambiguous > 0)
    def _():
        n_bits = max(1, (seq - 1).bit_length())

        def jbit_body(k, x):
            cand = x + lax.shift_left(jnp.int32(1), jnp.int32(n_bits - 1) - k)
            xb = jnp.broadcast_to(cand, (TQ, LANES))
            below, = count([lambda blk, off: jnp.where(blk == tb, ((off + lane) < xb).astype(I32), 0)])
            return jnp.where(below < need, cand, x)

        x = lax.fori_loop(0, n_bits, jbit_body, jnp.zeros((TQ, 1), I32))
        cut = jnp.where(t == INT_MIN, -1, jnp.where(n_eq > need, x, seq))
        jb = jnp.broadcast_to(cut, (TQ, LANES))
        write_bias(lambda blk, off: jnp.where(blk > tb, 1, jnp.where(blk == tb, ((off + lane) <= jb).astype(I32), 0)) > 0)

    def fill_body(kt, carry):
        bias_ref[:, pl.ds(pl.multiple_of(kt * TK, TK), TK)] = jnp.full((TQ, TK), NEG_BIG, BF16)
        return carry

    lax.fori_loop(kt_end, seq // TK, fill_body, 0)


def _dsa_select(qi_r, wi_r, kit, n_sel):
    b, nq = qi_r.shape[0], qi_r.shape[1]
    s = kit.shape[2]
    assert n_sel <= 2 * LANES and s % TK == 0
    return pl.pallas_call(
        functools.partial(_dsa_select_kernel, n_sel=n_sel, seq=s),
        out_shape=jax.ShapeDtypeStruct((b, s, s), BF16),
        grid=(b, nq),
        in_specs=[
            pl.BlockSpec((None, None, IDX_HEADS * TQ, IDX_DIM), lambda bb, i: (bb, i, 0, 0)),
            pl.BlockSpec((None, None, IDX_HEADS * TQ, 1), lambda bb, i: (bb, i, 0, 0)),
            pl.BlockSpec((None, IDX_DIM, s), lambda bb, i: (bb, 0, 0)),
        ],
        out_specs=pl.BlockSpec((None, TQ, s), lambda bb, i: (bb, i, 0)),
        scratch_shapes=[
            pltpu.VMEM((TQ, s + LANES), I32),
            pltpu.VMEM((IDX_HEADS * TQ, LANES), F32),
        ],
        compiler_params=_cparams(("parallel", "arbitrary")),
        name="dsa_select",
    )(qi_r, wi_r, kit)


def _dsa_attend_kernel(qtab_ref, ktab_ref, q_ref, k_ref, v_ref, bias_ref, o_ref, m_scr, acc_scr):
    step = pl.program_id(1)
    i = qtab_ref[step]
    kt = ktab_ref[step]
    last_kt = (i * ATQ + ATQ - 1) // ATK

    @pl.when(kt == 0)
    def _():
        m_scr[...] = jnp.full(m_scr.shape, NEG_BIG, F32)
        acc_scr[...] = jnp.zeros(acc_scr.shape, F32)

    bias = bias_ref[...].astype(F32)
    log2e_scale = (HEAD_DIM ** -0.5) * math.log2(math.e)
    ones = jnp.ones((ATK, HEAD_DIM), BF16)
    for h in range(ATTN_HEADS):
        hs = slice(h * HEAD_DIM, (h + 1) * HEAD_DIM)
        logits = lax.dot_general(q_ref[:, hs], k_ref[:, hs], (((1,), (1,)), ((), ())),
                                 preferred_element_type=F32)
        s = logits * log2e_scale + bias
        m_old = m_scr[h]
        m_new = jnp.maximum(m_old, jnp.max(s, axis=1, keepdims=True))
        alpha = jnp.exp2(m_old - m_new)
        p = jnp.concatenate([jnp.exp2(s[:, c * LANES:(c + 1) * LANES] - m_new) for c in range(ATK // LANES)],
                            axis=1).astype(BF16)
        v_ext = jnp.concatenate([v_ref[:, hs], ones], axis=1)
        pv = jnp.dot(p, v_ext, preferred_element_type=F32)
        acc_scr[h] = jnp.concatenate([alpha, alpha], axis=1) * acc_scr[h] + pv
        m_scr[h] = m_new

    @pl.when(kt == last_kt)
    def _():
        for h in range(ATTN_HEADS):
            acc = acc_scr[h]
            o_ref[:, h * HEAD_DIM:(h + 1) * HEAD_DIM] = (acc[:, :HEAD_DIM] / acc[:, HEAD_DIM:]).astype(o_ref.dtype)


def _dsa_attend(q, k, v, bias):
    b, s, _ = q.shape
    atq = min(ATQ, s)
    assert atq == ATQ and s % ATQ == 0
    qtab, ktab = [], []
    for i in range(s // ATQ):
        for kt in range((i * ATQ + ATQ - 1) // ATK + 1):
            qtab.append(i)
            ktab.append(kt)
    grid_spec = pltpu.PrefetchScalarGridSpec(
        num_scalar_prefetch=2,
        grid=(b, len(qtab)),
        in_specs=[
            pl.BlockSpec((None, ATQ, ATTN_WIDTH), lambda bb, st, qt, kk: (bb, qt[st], 0)),
            pl.BlockSpec((None, ATK, ATTN_WIDTH), lambda bb, st, qt, kk: (bb, kk[st], 0)),
            pl.BlockSpec((None, ATK, ATTN_WIDTH), lambda bb, st, qt, kk: (bb, kk[st], 0)),
            pl.BlockSpec((None, ATQ, ATK), lambda bb, st, qt, kk: (bb, qt[st], kk[st])),
        ],
        out_specs=pl.BlockSpec((None, ATQ, ATTN_WIDTH), lambda bb, st, qt, kk: (bb, qt[st], 0)),
        scratch_shapes=[
            pltpu.VMEM((ATTN_HEADS, ATQ, LANES), F32),
            pltpu.VMEM((ATTN_HEADS, ATQ, 2 * HEAD_DIM), F32),
        ],
    )
    return pl.pallas_call(
        _dsa_attend_kernel,
        out_shape=jax.ShapeDtypeStruct((b, s, ATTN_WIDTH), BF16),
        grid_spec=grid_spec,
        compiler_params=_cparams(("parallel", "arbitrary")),
        name="dsa_attend",
    )(jnp.asarray(qtab, I32), jnp.asarray(ktab, I32), q, k, v, bias)


def _gmlp_mix_kernel(gu_ref, gv_ref, ga_ref, gb_ref, ya_ref, ng_ref, ws_ref, bst_ref, wa_ref, wb_ref,
                     o_ref, yb_scr):
    tt = gu_ref.shape[0]
    row = lax.broadcasted_iota(I32, (CHUNK, CHUNK), 0)
    col = lax.broadcasted_iota(I32, (CHUNK, CHUNK), 1)
    tril = col <= row
    for c in range(tt // CHUNK):
        rs = slice(c * CHUNK, (c + 1) * CHUNK)
        u = jax.nn.gelu(gu_ref[rs, :])
        v = jax.nn.gelu(gv_ref[rs, :])
        v = (v * lax.rsqrt(jnp.mean(v * v, axis=-1, keepdims=True) + EPS)) * ng_ref[...]
        vb = v.astype(BF16)
        for g in range(GMLP_GROUPS):
            gs = slice(g * CHUNK, (g + 1) * CHUNK)
            wm = jnp.where(tril, ws_ref[g], 0.0).astype(BF16)
            z = jnp.dot(wm, vb[:, gs], preferred_element_type=F32) + bst_ref[:, g:g + 1]
            yb_scr[rs, gs] = (u[:, gs] * z).astype(BF16)
    ma = jnp.dot(ya_ref[...], wa_ref[...], preferred_element_type=F32)
    mb = jnp.dot(yb_scr[...], wb_ref[...], preferred_element_type=F32)
    o_ref[...] = (jax.nn.sigmoid(ga_ref[...]) * ma + jax.nn.sigmoid(gb_ref[...]) * mb).astype(o_ref.dtype)


def _gmlp_mix(gates, ya, norm_g, w_s, b_s_t, wa_bf, wb_bf, d_model):
    t = ya.shape[0]
    tt = min(MIX_TT, t)
    assert d_model == 2 * GMLP_WIDTH
    return pl.pallas_call(
        _gmlp_mix_kernel,
        out_shape=jax.ShapeDtypeStruct((t, d_model), BF16),
        grid=(t // tt,),
        in_specs=[
            pl.BlockSpec((tt, GMLP_WIDTH), lambda i: (i, 0)),
            pl.BlockSpec((tt, GMLP_WIDTH), lambda i: (i, 1)),
            pl.BlockSpec((tt, d_model), lambda i: (i, 1)),
            pl.BlockSpec((tt, d_model), lambda i: (i, 2)),
            pl.BlockSpec((tt, ATTN_WIDTH), lambda i: (i, 0)),
            pl.BlockSpec((1, GMLP_WIDTH), lambda i: (0, 0)),
            pl.BlockSpec((GMLP_GROUPS, CHUNK, CHUNK), lambda i: (0, 0, 0)),
            pl.BlockSpec((CHUNK, GMLP_GROUPS), lambda i: (0, 0)),
            pl.BlockSpec((ATTN_WIDTH, d_model), lambda i: (0, 0)),
            pl.BlockSpec((GMLP_WIDTH, d_model), lambda i: (0, 0)),
        ],
        out_specs=pl.BlockSpec((tt, d_model), lambda i: (i, 0)),
        scratch_shapes=[pltpu.VMEM((tt, GMLP_WIDTH), BF16)],
        compiler_params=_cparams(("parallel",)),
        name="gmlp_mix",
    )(gates, gates, gates, gates, ya, norm_g.reshape(1, GMLP_WIDTH), w_s, b_s_t, wa_bf, wb_bf)


def _out_proj_kernel(x_ref, m_ref, wo_ref, g_ref, x1_ref, h2_ref):
    x1 = x_ref[...] + jnp.dot(m_ref[...], wo_ref[...], preferred_element_type=F32)
    x1_ref[...] = x1
    ms = jnp.mean(x1 * x1, axis=-1, keepdims=True)
    h2_ref[...] = ((x1 * lax.rsqrt(ms + EPS)) * g_ref[...]).astype(BF16)


def _out_proj(x2, mixed, wo_bf, g):
    t, d = x2.shape
    tt = min(MIX_TT, t)
    return pl.pallas_call(
        _out_proj_kernel,
        out_shape=(jax.ShapeDtypeStruct((t, d), F32), jax.ShapeDtypeStruct((t, d), BF16)),
        grid=(t // tt,),
        in_specs=[
            pl.BlockSpec((tt, d), lambda i: (i, 0)),
            pl.BlockSpec((tt, d), lambda i: (i, 0)),
            pl.BlockSpec((d, d), lambda i: (0, 0)),
            pl.BlockSpec((1, d), lambda i: (0, 0)),
        ],
        out_specs=(pl.BlockSpec((tt, d), lambda i: (i, 0)), pl.BlockSpec((tt, d), lambda i: (i, 0))),
        compiler_params=_cparams(("parallel",)),
        name="out_proj",
    )(x2, mixed, wo_bf, g.reshape(1, d))


def _peer_scores_kernel(h2_ref, wq_ref, sk_ref, st_ref):
    qp = jnp.dot(h2_ref[...], wq_ref[...], preferred_element_type=F32).astype(BF16)
    half = sk_ref.shape[2]
    for hp in range(2 * PEER_HEADS):
        st_ref[hp] = lax.dot_general(sk_ref[hp], qp[:, hp * half:(hp + 1) * half], (((1,), (1,)), ((), ())),
                                     preferred_element_type=F32)


def _peer_scores(h2, wq_bf, sk_bf):
    t, d = h2.shape
    tt = min(MIX_TT, t)
    nq = wq_bf.shape[1]
    half = sk_bf.shape[2]
    return pl.pallas_call(
        _peer_scores_kernel,
        out_shape=jax.ShapeDtypeStruct((2 * PEER_HEADS, N_KEYS, t), F32),
        grid=(t // tt,),
        in_specs=[
            pl.BlockSpec((tt, d), lambda i: (i, 0)),
            pl.BlockSpec((d, nq), lambda i: (0, 0)),
            pl.BlockSpec((2 * PEER_HEADS, N_KEYS, half), lambda i: (0, 0, 0)),
        ],
        out_specs=pl.BlockSpec((2 * PEER_HEADS, N_KEYS, tt), lambda i: (0, 0, i)),
        compiler_params=_cparams(("parallel",)),
        name="peer_scores",
    )(h2, wq_bf, sk_bf)


def _top_values(cur, k):
    n = cur.shape[0]
    idx = lax.broadcasted_iota(I32, cur.shape, 0).astype(F32)
    vals = []
    for _ in range(k):
        mx = jnp.max(cur, axis=0, keepdims=True)
        first = jnp.min(jnp.where(cur == mx, idx, float(n)), axis=0, keepdims=True)
        vals.append(mx)
        cur = jnp.where(idx == first, -jnp.inf, cur)
    return vals


def _staircase_sums(t1, t2):
    a1 = jnp.concatenate(t1, axis=0)
    a2 = jnp.concatenate(t2, axis=0)
    r16 = lax.broadcasted_iota(I32, (PEER_TOPK, LANES), 0)
    r8 = r16[:8]
    ninf = -jnp.inf
    return jnp.concatenate([
        t1[0] + a2,
        t1[1] + a2[:8],
        jnp.where(r16 >= 2, a1 + t2[0], ninf),
        jnp.where(r8 >= 2, a1[:8] + t2[1], ninf),
        jnp.where((r8 >= 2) & (r8 <= 4), t1[2] + a2[:8], ninf),
        jnp.where((r8 >= 2) & (r8 <= 3), t1[3] + a2[:8], ninf),
        jnp.where(r8 == 2, t1[4] + a2[:8], ninf),
    ], axis=0)


def _peer_select_kernel(st_ref, th_ref, e1_ref, e2_ref):
    tl = st_ref.shape[2]

    def lane_group(g, carry):
        ls = pl.ds(pl.multiple_of(g * LANES, LANES), LANES)
        for h in range(PEER_HEADS):
            s1 = st_ref[2 * h, :, ls]
            s2 = st_ref[2 * h + 1, :, ls]
            t1 = _top_values(s1, PEER_TOPK)
            t2 = _top_values(s2, PEER_TOPK)
            best = _top_values(_staircase_sums(t1, t2), PEER_TOPK)
            m = best[0]
            z = functools.reduce(lambda a, b: a + b, [jnp.exp(bs - m) for bs in best])
            th_ref[h:h + 1, ls] = best[PEER_TOPK - 1]
            e1_ref[h, :, ls] = jnp.exp(s1 - t1[0]) / z
            e2_ref[h, :, ls] = jnp.exp(s2 - t2[0])
        return carry

    lax.fori_loop(0, tl // LANES, lane_group, 0)


def _peer_select(st):
    t = st.shape[2]
    tl = min(SEL_TL, t)
    e_shape = jax.ShapeDtypeStruct((PEER_HEADS, N_KEYS, t), F32)
    return pl.pallas_call(
        _peer_select_kernel,
        out_shape=(jax.ShapeDtypeStruct((PEER_HEADS, t), F32), e_shape, e_shape),
        grid=(t // tl,),
        in_specs=[pl.BlockSpec((2 * PEER_HEADS, N_KEYS, tl), lambda i: (0, 0, i))],
        out_specs=(
            pl.BlockSpec((PEER_HEADS, tl), lambda i: (0, i)),
            pl.BlockSpec((PEER_HEADS, N_KEYS, tl), lambda i: (0, 0, i)),
            pl.BlockSpec((PEER_HEADS, N_KEYS, tl), lambda i: (0, 0, i)),
        ),
        compiler_params=_cparams(("parallel",)),
        name="peer_select",
    )(st)


def _peer_dense_kernel(h2_ref, u_ref, vt_ref, s2_ref, s1row_ref, e1row_ref, th_ref, e2_ref, o_ref,
                       s2_scr, e2_scr, act_scr, p_scr, acc_scr):
    ei = pl.program_id(1)
    te = act_scr.shape[0]
    tt = h2_ref.shape[0]

    @pl.when(ei == 0)
    def _():
        acc_scr[...] = jnp.zeros(acc_scr.shape, F32)
        s2_scr[:, :, :tt] = s2_ref[...]
        e2_scr[:, :, :tt] = e2_ref[...]

    act_scr[:, :tt] = lax.dot_general(u_ref[...], h2_ref[...], (((1,), (1,)), ((), ())),
                                      preferred_element_type=F32)
    for ii in range(PEER_ROWS):
        rs = slice(ii * N_KEYS, (ii + 1) * N_KEYS)
        for lc in range(tt // LANES):
            ls = slice(lc * LANES, (lc + 1) * LANES)
            coef = jnp.zeros((N_KEYS, LANES), F32)
            for h in range(PEER_HEADS):
                s1row = s1row_ref[h, ii:ii + 1, ls]
                e1row = e1row_ref[h, ii:ii + 1, ls]
                keep = (s2_scr[h, :, ls] + s1row) >= th_ref[h:h + 1, ls]
                coef = coef + jnp.where(keep, e2_scr[h, :, ls] * e1row, 0.0)
            p_scr[rs, ls] = (coef * jax.nn.gelu(act_scr[rs, ls])).astype(BF16)
    acc_scr[...] += jnp.dot(vt_ref[...], p_scr[:, :tt], preferred_element_type=F32)

    @pl.when(ei == pl.num_programs(1) - 1)
    def _():
        o_ref[...] = acc_scr[...].T


def _peer_dense(h2, u_bf, vt_bf, st, th, e1, e2):
    t, d = h2.shape
    n_exp = u_bf.shape[0]
    tt = min(PEER_TT, t)
    te = PEER_ROWS * N_KEYS
    st4 = st.reshape(PEER_HEADS, 2, N_KEYS, t)
    return pl.pallas_call(
        _peer_dense_kernel,
        out_shape=jax.ShapeDtypeStruct((t, d), F32),
        grid=(t // tt, n_exp // te),
        in_specs=[
            pl.BlockSpec((tt, d), lambda i, e: (i, 0)),
            pl.BlockSpec((te, d), lambda i, e: (e, 0)),
            pl.BlockSpec((d, te), lambda i, e: (0, e)),
            pl.BlockSpec((PEER_HEADS, None, N_KEYS, tt), lambda i, e: (0, 1, 0, i)),
            pl.BlockSpec((PEER_HEADS, None, PEER_ROWS, tt), lambda i, e: (0, 0, e, i)),
            pl.BlockSpec((PEER_HEADS, PEER_ROWS, tt), lambda i, e: (0, e, i)),
            pl.BlockSpec((PEER_HEADS, tt), lambda i, e: (0, i)),
            pl.BlockSpec((PEER_HEADS, N_KEYS, tt), lambda i, e: (0, 0, i)),
        ],
        out_specs=pl.BlockSpec((tt, d), lambda i, e: (i, 0)),
        scratch_shapes=[
            pltpu.VMEM((PEER_HEADS, N_KEYS, tt + LANES), F32),
            pltpu.VMEM((PEER_HEADS, N_KEYS, tt + LANES), F32),
            pltpu.VMEM((te, tt + LANES), F32),
            pltpu.VMEM((te, tt + LANES), BF16),
            pltpu.VMEM((d, tt), F32),
        ],
        compiler_params=_cparams(("parallel", "arbitrary")),
        name="peer_dense",
    )(h2, u_bf, vt_bf, st4, st4, e1, th, e2)


def _final_norm_kernel(x1_ref, p_ref, g_ref, o_ref):
    y = x1_ref[...] + p_ref[...]
    ms = jnp.mean(y * y, axis=-1, keepdims=True)
    o_ref[...] = (y * lax.rsqrt(ms + EPS)) * g_ref[...]


def _final_norm(x1, peer_out, g):
    t, d = x1.shape
    tt = min(NORM_TT, t)
    return pl.pallas_call(
        _final_norm_kernel,
        out_shape=jax.ShapeDtypeStruct((t, d), F32),
        grid=(t // tt,),
        in_specs=[
            pl.BlockSpec((tt, d), lambda i: (i, 0)),
            pl.BlockSpec((tt, d), lambda i: (i, 0)),
            pl.BlockSpec((1, d), lambda i: (0, 0)),
        ],
        out_specs=pl.BlockSpec((tt, d), lambda i: (i, 0)),
        compiler_params=_cparams(("parallel",)),
        name="final_norm",
    )(x1, peer_out, g.reshape(1, d))


def _layer(x2, b, s, ln_mix_g, w_in, gmlp_norm_g, w_spatial, b_spatial, w_branch_attn, w_branch_gmlp,
           w_out, ln_ffn_g, peer_w_q, peer_sub_keys, peer_u, peer_v):
    t, d = x2.shape
    n_qi = IDX_HEADS * IDX_DIM
    o_qi = 3 * ATTN_WIDTH
    o_ki = o_qi + n_qi
    o_wi = o_ki + IDX_DIM
    o_gate = o_wi + IDX_HEADS

    w_attn = w_in[:, :o_ki].astype(BF16)
    w_idx = jnp.pad(w_in[:, o_ki:o_gate], ((0, 0), (0, LANES - IDX_DIM - IDX_HEADS))).astype(BF16)
    w_gate = w_in[:, o_gate:].astype(BF16)

    attn_in = _norm_proj(x2, ln_mix_g, w_attn, BF16, ATTN_TN)
    idx_in = _norm_proj(x2, ln_mix_g, w_idx, F32, LANES)
    gates = _norm_proj(x2, ln_mix_g, w_gate, F32, GATE_TN)

    q = attn_in[:, :ATTN_WIDTH].reshape(b, s, ATTN_WIDTH)
    k = attn_in[:, ATTN_WIDTH:2 * ATTN_WIDTH].reshape(b, s, ATTN_WIDTH)
    v = attn_in[:, 2 * ATTN_WIDTH:o_qi].reshape(b, s, ATTN_WIDTH)
    nq = s // TQ
    qi_r = attn_in[:, o_qi:o_ki].reshape(b, nq, TQ, IDX_HEADS, IDX_DIM)
    qi_r = qi_r.transpose(0, 1, 3, 2, 4).reshape(b, nq, IDX_HEADS * TQ, IDX_DIM)
    kit = idx_in[:, :IDX_DIM].astype(BF16).reshape(b, s, IDX_DIM).transpose(0, 2, 1)
    wi_r = idx_in[:, IDX_DIM:IDX_DIM + IDX_HEADS].reshape(b, nq, TQ, IDX_HEADS)
    wi_r = wi_r.transpose(0, 1, 3, 2).reshape(b, nq, IDX_HEADS * TQ, 1)

    n_sel = min(DSA_TOPK, s // 4)
    bias = _dsa_select(qi_r, wi_r, kit, n_sel)
    ya = _dsa_attend(q, k, v, bias).reshape(t, ATTN_WIDTH)

    mixed = _gmlp_mix(gates, ya, gmlp_norm_g, w_spatial, b_spatial.T, w_branch_attn.astype(BF16),
                      w_branch_gmlp.astype(BF16), d)
    x1, h2 = _out_proj(x2, mixed, w_out.astype(BF16), ln_ffn_g)

    half = peer_sub_keys.shape[-1]
    sk = peer_sub_keys.reshape(2 * PEER_HEADS, N_KEYS, half).astype(BF16)
    st = _peer_scores(h2, peer_w_q.astype(BF16), sk)
    th, e1, e2 = _peer_select(st)
    peer_out = _peer_dense(h2, peer_u.astype(BF16), peer_v.astype(BF16).T, st, th, e1, e2)
    return x1, peer_out


def kernel(x, ln_mix_g, w_in, gmlp_norm_g, w_spatial, b_spatial, w_branch_attn, w_branch_gmlp, w_out, ln_ffn_g, peer_w_q, peer_sub_keys, peer_u, peer_v, ln_final_g):
    b, s, d = x.shape
    depth = w_in.shape[0]
    x2 = x.reshape(b * s, d)
    for l in range(depth):
        x1, peer_out = _layer(x2, b, s, ln_mix_g[l], w_in[l], gmlp_norm_g[l], w_spatial[l], b_spatial[l],
                              w_branch_attn[l], w_branch_gmlp[l], w_out[l], ln_ffn_g[l], peer_w_q[l],
                              peer_sub_keys[l], peer_u[l], peer_v[l])
        if l + 1 < depth:
            x2 = x1 + peer_out
    return _final_norm(x1, peer_out, ln_final_g).reshape(b, s, d)
```

```python
import functools
import math

import numpy as np
import jax
import jax.numpy as jnp
from jax import lax
from jax.experimental import pallas as pl
from jax.experimental.pallas import tpu as pltpu

F32 = jnp.float32
BF16 = jnp.bfloat16
I32 = jnp.int32

ATTN_HEADS = 8
HEAD_DIM = 128
ATTN_WIDTH = ATTN_HEADS * HEAD_DIM
IDX_HEADS = 8
IDX_DIM = 64
DSA_TOPK = 256
GMLP_GROUPS = 8
CHUNK = 128
GMLP_WIDTH = GMLP_GROUPS * CHUNK
N_KEYS = 128
PEER_HEADS = 8
PEER_TOPK = 16
EPS = 1e-6

LANES = 128
MXU_COLS = 256
VMEM_LIMIT_BYTES = 56 * 1024 * 1024

INT_MIN = -(2 ** 31)
NEG_BIG = -1e30

TQ = 128
TK = 512
SCAN_KEYS = 2048
ATQ = 512
ATK = 512
PROJ_TM = 1024
ATTN_TN = 896
GATE_TN = 1024
NORM_TT = 512
MIX_TT = 256
PEER_TT = 512
PEER_ROWS = 8
SEL_TL = 512


def _cparams(sem):
    return pltpu.CompilerParams(dimension_semantics=sem, vmem_limit_bytes=VMEM_LIMIT_BYTES)


def _norm_proj_kernel(x_ref, g_ref, w_ref, o_ref, h_scr):
    @pl.when(pl.program_id(1) == 0)
    def _():
        x = x_ref[...]
        ms = jnp.mean(x * x, axis=-1, keepdims=True)
        h_scr[...] = ((x * lax.rsqrt(ms + EPS)) * g_ref[...]).astype(BF16)

    o_ref[...] = jnp.dot(h_scr[...], w_ref[...], preferred_element_type=F32).astype(o_ref.dtype)


def _norm_proj(x2, g, w_bf, out_dtype, tn):
    t, d = x2.shape
    n = w_bf.shape[1]
    tm = min(PROJ_TM, t)
    return pl.pallas_call(
        _norm_proj_kernel,
        out_shape=jax.ShapeDtypeStruct((t, n), out_dtype),
        grid=(t // tm, n // tn),
        in_specs=[
            pl.BlockSpec((tm, d), lambda i, j: (i, 0)),
            pl.BlockSpec((1, d), lambda i, j: (0, 0)),
            pl.BlockSpec((d, tn), lambda i, j: (0, j)),
        ],
        out_specs=pl.BlockSpec((tm, tn), lambda i, j: (i, j)),
        scratch_shapes=[pltpu.VMEM((tm, d), BF16)],
        compiler_params=_cparams(("parallel", "arbitrary")),
        name="norm_proj",
    )(x2, g.reshape(1, d), w_bf)


def _sort_key(x):
    bits = pltpu.bitcast(x, I32)
    return bits ^ ((bits >> 31) & 0x7FFFFFFF)


def _dsa_select_kernel(qi_ref, wi_ref, kit_ref, bias_ref, keys_scr, wb_scr, *, n_sel, seq):
    i = pl.program_id(1)
    n_kt = (i * TQ + TQ + TK - 1) // TK
    wb_scr[...] = jnp.broadcast_to((wi_ref[...] * (IDX_HEADS ** -0.5)) * (IDX_DIM ** -0.5), wb_scr.shape)
    q_pos = i * TQ + lax.broadcasted_iota(I32, (TQ, LANES), 0)
    lane = lax.broadcasted_iota(I32, (TQ, LANES), 1)

    def chunk_at(off):
        return pl.ds(pl.multiple_of(off, LANES), LANES)

    def score_body(kt, carry):
        m1, m2 = carry
        off = pl.multiple_of(kt * TK, TK)
        accs = [None] * (TK // LANES)
        for half in range(TK // MXU_COLS):
            kit_tile = kit_ref[:, pl.ds(pl.multiple_of(off + half * MXU_COLS, MXU_COLS), MXU_COLS)]
            for h in range(IDX_HEADS):
                hs = slice(h * TQ, (h + 1) * TQ)
                dots = jnp.maximum(jnp.dot(qi_ref[hs, :], kit_tile, preferred_element_type=F32), 0.0)
                for cc in range(MXU_COLS // LANES):
                    c = half * (MXU_COLS // LANES) + cc
                    term = dots[:, cc * LANES:(cc + 1) * LANES] * wb_scr[hs, :]
                    accs[c] = term if accs[c] is None else accs[c] + term
        for c in range(TK // LANES):
            sc = accs[c] + 0.0
            causal = (off + c * LANES + lane) <= q_pos
            scm = jnp.where(causal, sc, -jnp.inf)
            m2 = jnp.maximum(m2, jnp.minimum(m1, scm))
            m1 = jnp.maximum(m1, scm)
            keys_scr[:, chunk_at(off + c * LANES)] = jnp.where(causal, _sort_key(sc), INT_MIN)
        return m1, m2

    neg_inf = jnp.full((TQ, LANES), -jnp.inf, F32)
    m1, m2 = lax.fori_loop(0, n_kt, score_body, (neg_inf, neg_inf))

    scan = SCAN_KEYS if seq % SCAN_KEYS == 0 else TK
    n_st = (n_kt * TK + scan - 1) // scan
    kt_end = n_st * (scan // TK)

    def pad_body(kt, carry):
        keys_scr[:, pl.ds(pl.multiple_of(kt * TK, TK), TK)] = jnp.full((TQ, TK), INT_MIN, I32)
        return carry

    lax.fori_loop(n_kt, kt_end, pad_body, 0)

    def count(preds):
        def body(st, accs):
            off = pl.multiple_of(st * scan, scan)
            for c in range(scan // LANES):
                blk = keys_scr[:, chunk_at(off + c * LANES)]
                accs = tuple(a + p(blk, off + c * LANES) for a, p in zip(accs, preds))
            return accs
        accs = lax.fori_loop(0, n_st, body, tuple(jnp.zeros((TQ, LANES), I32) for _ in preds))
        return [jnp.sum(a.astype(F32), axis=1, keepdims=True).astype(I32) for a in accs]

    low_f = jnp.min(m2, axis=1, keepdims=True)
    low_key = _sort_key(jnp.broadcast_to(low_f, (TQ, LANES)))[:, :1]
    lo0 = jnp.where(low_f == -jnp.inf, INT_MIN, low_key)
    hi0 = _sort_key(jnp.broadcast_to(jnp.max(m1, axis=1, keepdims=True), (TQ, LANES)))[:, :1] + 1

    def bisect_body(state):
        lo, hi, _ = state
        mid = (lo >> 1) + (hi >> 1) + (lo & hi & 1)
        mb = jnp.broadcast_to(mid, (TQ, LANES))
        c, = count([lambda blk, off: (blk >= mb).astype(I32)])
        ge = c >= n_sel
        lo_n = jnp.where(ge, mid, lo)
        hi_n = jnp.where(c == n_sel, mid + 1, jnp.where(ge, hi, mid))
        return lo_n, hi_n, jnp.max(jnp.where(hi_n != lo_n + 1, 1.0, 0.0)).astype(F32)

    t, _, _ = lax.while_loop(lambda st: st[2] > 0.0, bisect_body, (lo0, hi0, jnp.float32(1.0)))
    tb = jnp.broadcast_to(t, (TQ, LANES))

    n_gt, n_eq = count([lambda blk, off: (blk > tb).astype(I32), lambda blk, off: (blk == tb).astype(I32)])
    need = n_sel - n_gt
    ambiguous = jnp.max(jnp.where((n_eq > need) & (t > INT_MIN), 1.0, 0.0))

    def write_bias(keep_fn):
        def body(st, carry):
            off = pl.multiple_of(st * scan, scan)
            for c in range(scan // LANES):
                blk = keys_scr[:, chunk_at(off + c * LANES)]
                bias_ref[:, chunk_at(off + c * LANES)] = jnp.where(
                    keep_fn(blk, off + c * LANES), 0.0, NEG_BIG).astype(BF16)
            return carry
        lax.fori_loop(0, n_st, body, 0)

    @pl.when(ambiguous <= 0)
    def _():
        thr = jnp.broadcast_to(jnp.maximum(t, INT_MIN + 1), (TQ, LANES))
        write_bias(lambda blk, off: blk >= thr)

    @pl.when(ambiguous > 0)
    def _():
        n_bits = max(1, (seq - 1).bit_length())

        def jbit_body(k, x):
            cand = x + lax.shift_left(jnp.int32(1), jnp.int32(n_bits - 1) - k)
            xb = jnp.broadcast_to(cand, (TQ, LANES))
            below, = count([lambda blk, off: jnp.where(blk == tb, ((off + lane) < xb).astype(I32), 0)])
            return jnp.where(below < need, cand, x)

        x = lax.fori_loop(0, n_bits, jbit_body, jnp.zeros((TQ, 1), I32))
        cut = jnp.where(t == INT_MIN, -1, jnp.where(n_eq > need, x, seq))
        jb = jnp.broadcast_to(cut, (TQ, LANES))
        write_bias(lambda blk, off: jnp.where(blk > tb, 1, jnp.where(blk == tb, ((off + lane) <= jb).astype(I32), 0)) > 0)

    def fill_body(kt, carry):
        bias_ref[:, pl.ds(pl.multiple_of(kt * TK, TK), TK)] = jnp.full((TQ, TK), NEG_BIG, BF16)
        return carry

    lax.fori_loop(kt_end, seq // TK, fill_body, 0)


def _dsa_select(qi_r, wi_r, kit, n_sel):
    b, nq = qi_r.shape[0], qi_r.shape[1]
    s = kit.shape[2]
    assert n_sel <= 2 * LANES and s % TK == 0
    return pl.pallas_call(
        functools.partial(_dsa_select_kernel, n_sel=n_sel, seq=s),
        out_shape=jax.ShapeDtypeStruct((b, s, s), BF16),
        grid=(b, nq),
        in_specs=[
            pl.BlockSpec((None, None, IDX_HEADS * TQ, IDX_DIM), lambda bb, i: (bb, i, 0, 0)),
            pl.BlockSpec((None, None, IDX_HEADS * TQ, 1), lambda bb, i: (bb, i, 0, 0)),
            pl.BlockSpec((None, IDX_DIM, s), lambda bb, i: (bb, 0, 0)),
        ],
        out_specs=pl.BlockSpec((None, TQ, s), lambda bb, i: (bb, i, 0)),
        scratch_shapes=[
            pltpu.VMEM((TQ, s + LANES), I32),
            pltpu.VMEM((IDX_HEADS * TQ, LANES), F32),
        ],
        compiler_params=_cparams(("parallel", "arbitrary")),
        name="dsa_select",
    )(qi_r, wi_r, kit)


def _dsa_attend_kernel(qtab_ref, ktab_ref, q_ref, k_ref, v_ref, bias_ref, o_ref, m_scr, acc_scr):
    step = pl.program_id(1)
    i = qtab_ref[step]
    kt = ktab_ref[step]
    last_kt = (i * ATQ + ATQ - 1) // ATK

    @pl.when(kt == 0)
    def _():
        m_scr[...] = jnp.full(m_scr.shape, NEG_BIG, F32)
        acc_scr[...] = jnp.zeros(acc_scr.shape, F32)

    bias = bias_ref[...].astype(F32)
    log2e_scale = (HEAD_DIM ** -0.5) * math.log2(math.e)
    ones = jnp.ones((ATK, HEAD_DIM), BF16)
    for h in range(ATTN_HEADS):
        hs = slice(h * HEAD_DIM, (h + 1) * HEAD_DIM)
        logits = lax.dot_general(q_ref[:, hs], k_ref[:, hs], (((1,), (1,)), ((), ())),
                                 preferred_element_type=F32)
        s = logits * log2e_scale + bias
        m_old = m_scr[h]
        m_new = jnp.maximum(m_old, jnp.max(s, axis=1, keepdims=True))
        alpha = jnp.exp2(m_old - m_new)
        p = jnp.concatenate([jnp.exp2(s[:, c * LANES:(c + 1) * LANES] - m_new) for c in range(ATK // LANES)],
                            axis=1).astype(BF16)
        v_ext = jnp.concatenate([v_ref[:, hs], ones], axis=1)
        pv = jnp.dot(p, v_ext, preferred_element_type=F32)
        acc_scr[h] = jnp.concatenate([alpha, alpha], axis=1) * acc_scr[h] + pv
        m_scr[h] = m_new

    @pl.when(kt == last_kt)
    def _():
        for h in range(ATTN_HEADS):
            acc = acc_scr[h]
            o_ref[:, h * HEAD_DIM:(h + 1) * HEAD_DIM] = (acc[:, :HEAD_DIM] / acc[:, HEAD_DIM:]).astype(o_ref.dtype)


def _dsa_attend(q, k, v, bias):
    b, s, _ = q.shape
    atq = min(ATQ, s)
    assert atq == ATQ and s % ATQ == 0
    qtab, ktab = [], []
    for i in range(s // ATQ):
        for kt in range((i * ATQ + ATQ - 1) // ATK + 1):
            qtab.append(i)
            ktab.append(kt)
    grid_spec = pltpu.PrefetchScalarGridSpec(
        num_scalar_prefetch=2,
        grid=(b, len(qtab)),
        in_specs=[
            pl.BlockSpec((None, ATQ, ATTN_WIDTH), lambda bb, st, qt, kk: (bb, qt[st], 0)),
            pl.BlockSpec((None, ATK, ATTN_WIDTH), lambda bb, st, qt, kk: (bb, kk[st], 0)),
            pl.BlockSpec((None, ATK, ATTN_WIDTH), lambda bb, st, qt, kk: (bb, kk[st], 0)),
            pl.BlockSpec((None, ATQ, ATK), lambda bb, st, qt, kk: (bb, qt[st], kk[st])),
        ],
        out_specs=pl.BlockSpec((None, ATQ, ATTN_WIDTH), lambda bb, st, qt, kk: (bb, qt[st], 0)),
        scratch_shapes=[
            pltpu.VMEM((ATTN_HEADS, ATQ, LANES), F32),
            pltpu.VMEM((ATTN_HEADS, ATQ, 2 * HEAD_DIM), F32),
        ],
    )
    return pl.pallas_call(
        _dsa_attend_kernel,
        out_shape=jax.ShapeDtypeStruct((b, s, ATTN_WIDTH), BF16),
        grid_spec=grid_spec,
        compiler_params=_cparams(("parallel", "arbitrary")),
        name="dsa_attend",
    )(jnp.asarray(qtab, I32), jnp.asarray(ktab, I32), q, k, v, bias)


def _gmlp_mix_kernel(gu_ref, gv_ref, ga_ref, gb_ref, ya_ref, ng_ref, ws_ref, bst_ref, wa_ref, wb_ref,
                     o_ref, yb_scr):
    tt = gu_ref.shape[0]
    row = lax.broadcasted_iota(I32, (CHUNK, CHUNK), 0)
    col = lax.broadcasted_iota(I32, (CHUNK, CHUNK), 1)
    tril = col <= row
    for c in range(tt // CHUNK):
        rs = slice(c * CHUNK, (c + 1) * CHUNK)
        u = jax.nn.gelu(gu_ref[rs, :])
        v = jax.nn.gelu(gv_ref[rs, :])
        v = (v * lax.rsqrt(jnp.mean(v * v, axis=-1, keepdims=True) + EPS)) * ng_ref[...]
        vb = v.astype(BF16)
        for g in range(GMLP_GROUPS):
            gs = slice(g * CHUNK, (g + 1) * CHUNK)
            wm = jnp.where(tril, ws_ref[g], 0.0).astype(BF16)
            z = jnp.dot(wm, vb[:, gs], preferred_element_type=F32) + bst_ref[:, g:g + 1]
            yb_scr[rs, gs] = (u[:, gs] * z).astype(BF16)
    ma = jnp.dot(ya_ref[...], wa_ref[...], preferred_element_type=F32)
    mb = jnp.dot(yb_scr[...], wb_ref[...], preferred_element_type=F32)
    o_ref[...] = (jax.nn.sigmoid(ga_ref[...]) * ma + jax.nn.sigmoid(gb_ref[...]) * mb).astype(o_ref.dtype)


def _gmlp_mix(gates, ya, norm_g, w_s, b_s_t, wa_bf, wb_bf, d_model):
    t = ya.shape[0]
    tt = min(MIX_TT, t)
    assert d_model == 2 * GMLP_WIDTH
    return pl.pallas_call(
        _gmlp_mix_kernel,
        out_shape=jax.ShapeDtypeStruct((t, d_model), BF16),
        grid=(t // tt,),
        in_specs=[
            pl.BlockSpec((tt, GMLP_WIDTH), lambda i: (i, 0)),
            pl.BlockSpec((tt, GMLP_WIDTH), lambda i: (i, 1)),
            pl.BlockSpec((tt, d_model), lambda i: (i, 1)),
            pl.BlockSpec((tt, d_model), lambda i: (i, 2)),
            pl.BlockSpec((tt, ATTN_WIDTH), lambda i: (i, 0)),
            pl.BlockSpec((1, GMLP_WIDTH), lambda i: (0, 0)),
            pl.BlockSpec((GMLP_GROUPS, CHUNK, CHUNK), lambda i: (0, 0, 0)),
            pl.BlockSpec((CHUNK, GMLP_GROUPS), lambda i: (0, 0)),
            pl.BlockSpec((ATTN_WIDTH, d_model), lambda i: (0, 0)),
            pl.BlockSpec((GMLP_WIDTH, d_model), lambda i: (0, 0)),
        ],
        out_specs=pl.BlockSpec((tt, d_model), lambda i: (i, 0)),
        scratch_shapes=[pltpu.VMEM((tt, GMLP_WIDTH), BF16)],
        compiler_params=_cparams(("parallel",)),
        name="gmlp_mix",
    )(gates, gates, gates, gates, ya, norm_g.reshape(1, GMLP_WIDTH), w_s, b_s_t, wa_bf, wb_bf)


def _out_proj_kernel(x_ref, m_ref, wo_ref, g_ref, x1_ref, h2_ref):
    x1 = x_ref[...] + jnp.dot(m_ref[...], wo_ref[...], preferred_element_type=F32)
    x1_ref[...] = x1
    ms = jnp.mean(x1 * x1, axis=-1, keepdims=True)
    h2_ref[...] = ((x1 * lax.rsqrt(ms + EPS)) * g_ref[...]).astype(BF16)


def _out_proj(x2, mixed, wo_bf, g):
    t, d = x2.shape
    tt = min(MIX_TT, t)
    return pl.pallas_call(
        _out_proj_kernel,
        out_shape=(jax.ShapeDtypeStruct((t, d), F32), jax.ShapeDtypeStruct((t, d), BF16)),
        grid=(t // tt,),
        in_specs=[
            pl.BlockSpec((tt, d), lambda i: (i, 0)),
            pl.BlockSpec((tt, d), lambda i: (i, 0)),
            pl.BlockSpec((d, d), lambda i: (0, 0)),
            pl.BlockSpec((1, d), lambda i: (0, 0)),
        ],
        out_specs=(pl.BlockSpec((tt, d), lambda i: (i, 0)), pl.BlockSpec((tt, d), lambda i: (i, 0))),
        compiler_params=_cparams(("parallel",)),
        name="out_proj",
    )(x2, mixed, wo_bf, g.reshape(1, d))


def _peer_scores_kernel(h2_ref, wq_ref, sk_ref, st_ref):
    qp = jnp.dot(h2_ref[...], wq_ref[...], preferred_element_type=F32).astype(BF16)
    half = sk_ref.shape[2]
    for hp in range(2 * PEER_HEADS):
        st_ref[hp] = lax.dot_general(sk_ref[hp], qp[:, hp * half:(hp + 1) * half], (((1,), (1,)), ((), ())),
                                     preferred_element_type=F32)


def _peer_scores(h2, wq_bf, sk_bf):
    t, d = h2.shape
    tt = min(MIX_TT, t)
    nq = wq_bf.shape[1]
    half = sk_bf.shape[2]
    return pl.pallas_call(
        _peer_scores_kernel,
        out_shape=jax.ShapeDtypeStruct((2 * PEER_HEADS, N_KEYS, t), F32),
        grid=(t // tt,),
        in_specs=[
            pl.BlockSpec((tt, d), lambda i: (i, 0)),
            pl.BlockSpec((d, nq), lambda i: (0, 0)),
            pl.BlockSpec((2 * PEER_HEADS, N_KEYS, half), lambda i: (0, 0, 0)),
        ],
        out_specs=pl.BlockSpec((2 * PEER_HEADS, N_KEYS, tt), lambda i: (0, 0, i)),
        compiler_params=_cparams(("parallel",)),
        name="peer_scores",
    )(h2, wq_bf, sk_bf)


def _top_values(cur, k):
    n = cur.shape[0]
    idx = lax.broadcasted_iota(I32, cur.shape, 0).astype(F32)
    vals = []
    for _ in range(k):
        mx = jnp.max(cur, axis=0, keepdims=True)
        first = jnp.min(jnp.where(cur == mx, idx, float(n)), axis=0, keepdims=True)
        vals.append(mx)
        cur = jnp.where(idx == first, -jnp.inf, cur)
    return vals


def _staircase_sums(t1, t2):
    a1 = jnp.concatenate(t1, axis=0)
    a2 = jnp.concatenate(t2, axis=0)
    r16 = lax.broadcasted_iota(I32, (PEER_TOPK, LANES), 0)
    r8 = r16[:8]
    ninf = -jnp.inf
    return jnp.concatenate([
        t1[0] + a2,
        t1[1] + a2[:8],
        jnp.where(r16 >= 2, a1 + t2[0], ninf),
        jnp.where(r8 >= 2, a1[:8] + t2[1], ninf),
        jnp.where((r8 >= 2) & (r8 <= 4), t1[2] + a2[:8], ninf),
        jnp.where((r8 >= 2) & (r8 <= 3), t1[3] + a2[:8], ninf),
        jnp.where(r8 == 2, t1[4] + a2[:8], ninf),
    ], axis=0)


def _top_ranked(cur, k):
    n = cur.shape[0]
    idx = lax.broadcasted_iota(I32, cur.shape, 0).astype(F32)
    rank = jnp.full(cur.shape, float(k), F32)
    vals, firsts = [], []
    for r in range(k):
        mx = jnp.max(cur, axis=0, keepdims=True)
        first = jnp.min(jnp.where(cur == mx, idx, float(n)), axis=0, keepdims=True)
        hit = idx == first
        rank = jnp.where(hit, float(r), rank)
        cur = jnp.where(hit, -jnp.inf, cur)
        vals.append(mx)
        firsts.append(first)
    return vals, firsts, rank


def _peer_select_kernel(st_ref, quota_ref, e1_ref, rank_ref, e2_ref):
    tl = st_ref.shape[2]
    idx = lax.broadcasted_iota(I32, (N_KEYS, LANES), 0).astype(F32)

    def lane_group(g, carry):
        ls = pl.ds(pl.multiple_of(g * LANES, LANES), LANES)
        for h in range(PEER_HEADS):
            s1 = st_ref[2 * h, :, ls]
            s2 = st_ref[2 * h + 1, :, ls]
            t1, first1, _ = _top_ranked(s1, PEER_TOPK)
            t2, _, rank2 = _top_ranked(s2, PEER_TOPK)
            best = _top_values(_staircase_sums(t1, t2), PEER_TOPK)
            m = best[0]
            theta = best[PEER_TOPK - 1]
            z = functools.reduce(lambda a, b: a + b, [jnp.exp(bs - m) for bs in best])
            a2 = jnp.concatenate(t2, axis=0)
            quota = jnp.zeros((N_KEYS, LANES), F32)
            for a in range(PEER_TOPK):
                n_ok = jnp.sum(jnp.where(t1[a] + a2 >= theta, 1.0, 0.0), axis=0, keepdims=True)
                quota = jnp.where(idx == first1[a], n_ok, quota)
            quota_ref[h, :, ls] = quota
            e1_ref[h, :, ls] = jnp.exp(s1 - t1[0]) / z
            rank_ref[h, :, ls] = rank2.astype(BF16)
            e2_ref[h, :, ls] = jnp.exp(s2 - t2[0]).astype(BF16)
        return carry

    lax.fori_loop(0, tl // LANES, lane_group, 0)


def _peer_select(st):
    t = st.shape[2]
    tl = min(SEL_TL, t)
    f_shape = jax.ShapeDtypeStruct((PEER_HEADS, N_KEYS, t), F32)
    b_shape = jax.ShapeDtypeStruct((PEER_HEADS, N_KEYS, t), BF16)
    spec = pl.BlockSpec((PEER_HEADS, N_KEYS, tl), lambda i: (0, 0, i))
    return pl.pallas_call(
        _peer_select_kernel,
        out_shape=(f_shape, f_shape, b_shape, b_shape),
        grid=(t // tl,),
        in_specs=[pl.BlockSpec((2 * PEER_HEADS, N_KEYS, tl), lambda i: (0, 0, i))],
        out_specs=(spec, spec, spec, spec),
        compiler_params=_cparams(("parallel",)),
        name="peer_select",
    )(st)


def _peer_dense_kernel(h2_ref, u_ref, vt_ref, quota_ref, e1row_ref, rank_ref, e2_ref, o_ref,
                       rank_scr, e2_scr, act_scr, p_scr, acc_scr):
    ei = pl.program_id(1)
    tt = h2_ref.shape[0]
    pack = 16

    @pl.when(ei == 0)
    def _():
        acc_scr[...] = jnp.zeros(acc_scr.shape, F32)
        rank_scr[:, :, :tt] = rank_ref[...]
        e2_scr[:, :, :tt] = e2_ref[...]

    act_scr[:, :tt] = lax.dot_general(u_ref[...], h2_ref[...], (((1,), (1,)), ((), ())),
                                      preferred_element_type=F32)
    for ii in range(PEER_ROWS):
        for lc in range(tt // LANES):
            ls = slice(lc * LANES, (lc + 1) * LANES)
            coef = [None] * (N_KEYS // pack)
            for h in range(PEER_HEADS):
                quota = jnp.broadcast_to(quota_ref[h, ii:ii + 1, ls], (pack, LANES)).astype(BF16)
                e1row = jnp.broadcast_to(e1row_ref[h, ii:ii + 1, ls], (pack, LANES)).astype(BF16)
                for k in range(N_KEYS // pack):
                    ks = slice(k * pack, (k + 1) * pack)
                    gate = jnp.where(rank_scr[h, ks, ls] < quota, e2_scr[h, ks, ls] * e1row, 0.0)
                    coef[k] = gate if coef[k] is None else coef[k] + gate
            for k in range(N_KEYS // pack):
                rs = slice(ii * N_KEYS + k * pack, ii * N_KEYS + (k + 1) * pack)
                p_scr[rs, ls] = (coef[k].astype(F32) * jax.nn.gelu(act_scr[rs, ls])).astype(BF16)
    acc_scr[...] += jnp.dot(vt_ref[...], p_scr[:, :tt], preferred_element_type=F32)

    @pl.when(ei == pl.num_programs(1) - 1)
    def _():
        o_ref[...] = acc_scr[...].T


def _peer_dense(h2, u_bf, vt_bf, quota, e1, rank2, e2):
    t, d = h2.shape
    n_exp = u_bf.shape[0]
    tt = min(PEER_TT, t)
    te = PEER_ROWS * N_KEYS
    row_spec = pl.BlockSpec((PEER_HEADS, PEER_ROWS, tt), lambda i, e: (0, e, i))
    tok_spec = pl.BlockSpec((PEER_HEADS, N_KEYS, tt), lambda i, e: (0, 0, i))
    return pl.pallas_call(
        _peer_dense_kernel,
        out_shape=jax.ShapeDtypeStruct((t, d), F32),
        grid=(t // tt, n_exp // te),
        in_specs=[
            pl.BlockSpec((tt, d), lambda i, e: (i, 0)),
            pl.BlockSpec((te, d), lambda i, e: (e, 0)),
            pl.BlockSpec((d, te), lambda i, e: (0, e)),
            row_spec,
            row_spec,
            tok_spec,
            tok_spec,
        ],
        out_specs=pl.BlockSpec((tt, d), lambda i, e: (i, 0)),
        scratch_shapes=[
            pltpu.VMEM((PEER_HEADS, N_KEYS, tt + LANES), BF16),
            pltpu.VMEM((PEER_HEADS, N_KEYS, tt + LANES), BF16),
            pltpu.VMEM((te, tt + LANES), F32),
            pltpu.VMEM((te, tt + LANES), BF16),
            pltpu.VMEM((d, tt), F32),
        ],
        compiler_params=_cparams(("parallel", "arbitrary")),
        name="peer_dense",
    )(h2, u_bf, vt_bf, quota, e1, rank2, e2)


def _final_norm_kernel(x1_ref, p_ref, g_ref, o_ref):
    y = x1_ref[...] + p_ref[...]
    ms = jnp.mean(y * y, axis=-1, keepdims=True)
    o_ref[...] = (y * lax.rsqrt(ms + EPS)) * g_ref[...]


def _final_norm(x1, peer_out, g):
    t, d = x1.shape
    tt = min(NORM_TT, t)
    return pl.pallas_call(
        _final_norm_kernel,
        out_shape=jax.ShapeDtypeStruct((t, d), F32),
        grid=(t // tt,),
        in_specs=[
            pl.BlockSpec((tt, d), lambda i: (i, 0)),
            pl.BlockSpec((tt, d), lambda i: (i, 0)),
            pl.BlockSpec((1, d), lambda i: (0, 0)),
        ],
        out_specs=pl.BlockSpec((tt, d), lambda i: (i, 0)),
        compiler_params=_cparams(("parallel",)),
        name="final_norm",
    )(x1, peer_out, g.reshape(1, d))


def _layer(x2, b, s, ln_mix_g, w_in, gmlp_norm_g, w_spatial, b_spatial, w_branch_attn, w_branch_gmlp,
           w_out, ln_ffn_g, peer_w_q, peer_sub_keys, peer_u, peer_v):
    t, d = x2.shape
    n_qi = IDX_HEADS * IDX_DIM
    o_qi = 3 * ATTN_WIDTH
    o_ki = o_qi + n_qi
    o_wi = o_ki + IDX_DIM
    o_gate = o_wi + IDX_HEADS

    w_attn = w_in[:, :o_ki].astype(BF16)
    w_idx = jnp.pad(w_in[:, o_ki:o_gate], ((0, 0), (0, LANES - IDX_DIM - IDX_HEADS))).astype(BF16)
    w_gate = w_in[:, o_gate:].astype(BF16)

    attn_in = _norm_proj(x2, ln_mix_g, w_attn, BF16, ATTN_TN)
    idx_in = _norm_proj(x2, ln_mix_g, w_idx, F32, LANES)
    gates = _norm_proj(x2, ln_mix_g, w_gate, F32, GATE_TN)

    q = attn_in[:, :ATTN_WIDTH].reshape(b, s, ATTN_WIDTH)
    k = attn_in[:, ATTN_WIDTH:2 * ATTN_WIDTH].reshape(b, s, ATTN_WIDTH)
    v = attn_in[:, 2 * ATTN_WIDTH:o_qi].reshape(b, s, ATTN_WIDTH)
    nq = s // TQ
    qi_r = attn_in[:, o_qi:o_ki].reshape(b, nq, TQ, IDX_HEADS, IDX_DIM)
    qi_r = qi_r.transpose(0, 1, 3, 2, 4).reshape(b, nq, IDX_HEADS * TQ, IDX_DIM)
    kit = idx_in[:, :IDX_DIM].astype(BF16).reshape(b, s, IDX_DIM).transpose(0, 2, 1)
    wi_r = idx_in[:, IDX_DIM:IDX_DIM + IDX_HEADS].reshape(b, nq, TQ, IDX_HEADS)
    wi_r = wi_r.transpose(0, 1, 3, 2).reshape(b, nq, IDX_HEADS * TQ, 1)

    n_sel = min(DSA_TOPK, s // 4)
    bias = _dsa_select(qi_r, wi_r, kit, n_sel)
    ya = _dsa_attend(q, k, v, bias).reshape(t, ATTN_WIDTH)

    mixed = _gmlp_mix(gates, ya, gmlp_norm_g, w_spatial, b_spatial.T, w_branch_attn.astype(BF16),
                      w_branch_gmlp.astype(BF16), d)
    x1, h2 = _out_proj(x2, mixed, w_out.astype(BF16), ln_ffn_g)

    half = peer_sub_keys.shape[-1]
    sk = peer_sub_keys.reshape(2 * PEER_HEADS, N_KEYS, half).astype(BF16)
    st = _peer_scores(h2, peer_w_q.astype(BF16), sk)
    quota, e1, rank2, e2 = _peer_select(st)
    peer_out = _peer_dense(h2, peer_u.astype(BF16), peer_v.astype(BF16).T, quota, e1, rank2, e2)
    return x1, peer_out


def kernel(x, ln_mix_g, w_in, gmlp_norm_g, w_spatial, b_spatial, w_branch_attn, w_branch_gmlp, w_out, ln_ffn_g, peer_w_q, peer_sub_keys, peer_u, peer_v, ln_final_g):
    b, s, d = x.shape
    depth = w_in.shape[0]
    x2 = x.reshape(b * s, d)
    for l in range(depth):
        x1, peer_out = _layer(x2, b, s, ln_mix_g[l], w_in[l], gmlp_norm_g[l], w_spatial[l], b_spatial[l],
                              w_branch_attn[l], w_branch_gmlp[l], w_out[l], ln_ffn_g[l], peer_w_q[l],
                              peer_sub_keys[l], peer_u[l], peer_v[l])
        if l + 1 < depth:
            x2 = x1 + peer_out
    return _final_norm(x1, peer_out, ln_final_g).reshape(b, s, d)
```

```python
import functools
import math

import jax
import jax.numpy as jnp
from jax import lax
from jax.experimental import pallas as pl
from jax.experimental.pallas import tpu as pltpu

F32 = jnp.float32
BF16 = jnp.bfloat16
I32 = jnp.int32

ATTN_HEADS = 8
HEAD_DIM = 128
ATTN_WIDTH = ATTN_HEADS * HEAD_DIM
IDX_HEADS = 8
IDX_DIM = 64
DSA_TOPK = 256
GMLP_GROUPS = 8
CHUNK = 128
GMLP_WIDTH = GMLP_GROUPS * CHUNK
N_KEYS = 128
PEER_HEADS = 8
PEER_TOPK = 16
EPS = 1e-6

LANES = 128
MXU_COLS = 256
VMEM_LIMIT_BYTES = 56 * 1024 * 1024

INT_MIN = -(2 ** 31)
NEG_BIG = -1e30

TQ = 128
TK = 512
SCAN_KEYS = 2048
ATQ = 512
ATK = 512
PROJ_TM = 1024
ATTN_TN = 896
GATE_TN = 1024
NORM_TT = 512
MIX_TT = 256
RES_TT = 512
PEER_TT = 512
PEER_ROWS = 8
SEL_TL = 512


def _cparams(sem):
    return pltpu.CompilerParams(dimension_semantics=sem, vmem_limit_bytes=VMEM_LIMIT_BYTES)


def _norm_proj_kernel(x_ref, g_ref, w_ref, o_ref, h_scr):
    @pl.when(pl.program_id(1) == 0)
    def _():
        x = x_ref[...]
        ms = jnp.mean(x * x, axis=-1, keepdims=True)
        h_scr[...] = ((x * lax.rsqrt(ms + EPS)) * g_ref[...]).astype(BF16)

    o_ref[...] = jnp.dot(h_scr[...], w_ref[...], preferred_element_type=F32).astype(o_ref.dtype)


def _norm_proj(x2, g, w_bf, out_dtype, tn):
    t, d = x2.shape
    n = w_bf.shape[1]
    tm = min(PROJ_TM, t)
    return pl.pallas_call(
        _norm_proj_kernel,
        out_shape=jax.ShapeDtypeStruct((t, n), out_dtype),
        grid=(t // tm, n // tn),
        in_specs=[
            pl.BlockSpec((tm, d), lambda i, j: (i, 0)),
            pl.BlockSpec((1, d), lambda i, j: (0, 0)),
            pl.BlockSpec((d, tn), lambda i, j: (0, j)),
        ],
        out_specs=pl.BlockSpec((tm, tn), lambda i, j: (i, j)),
        scratch_shapes=[pltpu.VMEM((tm, d), BF16)],
        compiler_params=_cparams(("parallel", "arbitrary")),
        name="norm_proj",
    )(x2, g.reshape(1, d), w_bf)


def _sort_key(x):
    bits = pltpu.bitcast(x, I32)
    return bits ^ ((bits >> 31) & 0x7FFFFFFF)


def _dsa_select_kernel(qi_ref, wi_ref, kit_ref, bias_ref, keys_scr, wb_scr, *, n_sel, seq):
    i = pl.program_id(1)
    n_kt = (i * TQ + TQ + TK - 1) // TK
    wb_scr[...] = jnp.broadcast_to((wi_ref[...] * (IDX_HEADS ** -0.5)) * (IDX_DIM ** -0.5), wb_scr.shape)
    q_pos = i * TQ + lax.broadcasted_iota(I32, (TQ, LANES), 0)
    lane = lax.broadcasted_iota(I32, (TQ, LANES), 1)

    def chunk_at(off):
        return pl.ds(pl.multiple_of(off, LANES), LANES)

    def score_body(kt, carry):
        m1, m2 = carry
        off = pl.multiple_of(kt * TK, TK)
        accs = [None] * (TK // LANES)
        for half in range(TK // MXU_COLS):
            kit_tile = kit_ref[:, pl.ds(pl.multiple_of(off + half * MXU_COLS, MXU_COLS), MXU_COLS)]
            for h in range(IDX_HEADS):
                hs = slice(h * TQ, (h + 1) * TQ)
                dots = jnp.maximum(jnp.dot(qi_ref[hs, :], kit_tile, preferred_element_type=F32), 0.0)
                for cc in range(MXU_COLS // LANES):
                    c = half * (MXU_COLS // LANES) + cc
                    term = dots[:, cc * LANES:(cc + 1) * LANES] * wb_scr[hs, :]
                    accs[c] = term if accs[c] is None else accs[c] + term
        for c in range(TK // LANES):
            sc = accs[c] + 0.0
            causal = (off + c * LANES + lane) <= q_pos
            scm = jnp.where(causal, sc, -jnp.inf)
            m2 = jnp.maximum(m2, jnp.minimum(m1, scm))
            m1 = jnp.maximum(m1, scm)
            keys_scr[:, chunk_at(off + c * LANES)] = jnp.where(causal, _sort_key(sc), INT_MIN)
        return m1, m2

    neg_inf = jnp.full((TQ, LANES), -jnp.inf, F32)
    m1, m2 = lax.fori_loop(0, n_kt, score_body, (neg_inf, neg_inf))

    scan = SCAN_KEYS if seq % SCAN_KEYS == 0 else TK
    n_st = (n_kt * TK + scan - 1) // scan
    kt_end = n_st * (scan // TK)

    def pad_body(kt, carry):
        keys_scr[:, pl.ds(pl.multiple_of(kt * TK, TK), TK)] = jnp.full((TQ, TK), INT_MIN, I32)
        return carry

    lax.fori_loop(n_kt, kt_end, pad_body, 0)

    def count(preds):
        def body(st, accs):
            off = pl.multiple_of(st * scan, scan)
            for c in range(scan // LANES):
                blk = keys_scr[:, chunk_at(off + c * LANES)]
                accs = tuple(a + p(blk, off + c * LANES) for a, p in zip(accs, preds))
            return accs
        accs = lax.fori_loop(0, n_st, body, tuple(jnp.zeros((TQ, LANES), I32) for _ in preds))
        return [jnp.sum(a.astype(F32), axis=1, keepdims=True).astype(I32) for a in accs]

    low_f = jnp.min(m2, axis=1, keepdims=True)
    low_key = _sort_key(jnp.broadcast_to(low_f, (TQ, LANES)))[:, :1]
    lo0 = jnp.where(low_f == -jnp.inf, INT_MIN, low_key)
    top_f = jnp.max(m2 if n_sel > LANES else m1, axis=1, keepdims=True)
    hi0 = _sort_key(jnp.broadcast_to(top_f, (TQ, LANES)))[:, :1] + 1

    def bisect_body(state):
        lo, hi, _ = state
        mid = (lo >> 1) + (hi >> 1) + (lo & hi & 1)
        mb = jnp.broadcast_to(mid, (TQ, LANES))
        c, = count([lambda blk, off: (blk >= mb).astype(I32)])
        ge = c >= n_sel
        lo_n = jnp.where(ge, mid, lo)
        hi_n = jnp.where(c == n_sel, mid + 1, jnp.where(ge, hi, mid))
        return lo_n, hi_n, jnp.max(jnp.where(hi_n != lo_n + 1, 1.0, 0.0)).astype(F32)

    t, _, _ = lax.while_loop(lambda st: st[2] > 0.0, bisect_body, (lo0, hi0, jnp.float32(1.0)))
    tb = jnp.broadcast_to(t, (TQ, LANES))

    n_gt, n_eq = count([lambda blk, off: (blk > tb).astype(I32), lambda blk, off: (blk == tb).astype(I32)])
    need = n_sel - n_gt
    ambiguous = jnp.max(jnp.where((n_eq > need) & (t > INT_MIN), 1.0, 0.0))

    def write_bias(keep_fn):
        def body(st, carry):
            off = pl.multiple_of(st * scan, scan)
            for c in range(scan // LANES):
                blk = keys_scr[:, chunk_at(off + c * LANES)]
                bias_ref[:, chunk_at(off + c * LANES)] = jnp.where(
                    keep_fn(blk, off + c * LANES), 0.0, NEG_BIG).astype(BF16)
            return carry
        lax.fori_loop(0, n_st, body, 0)

    @pl.when(ambiguous <= 0)
    def _():
        thr = jnp.broadcast_to(jnp.maximum(t, INT_MIN + 1), (TQ, LANES))
        write_bias(lambda blk, off: blk >= thr)

    @pl.when(ambiguous > 0)
    def _():
        n_bits = max(1, (seq - 1).bit_length())

        def jbit_body(k, x):
            cand = x + lax.shift_left(jnp.int32(1), jnp.int32(n_bits - 1) - k)
            xb = jnp.broadcast_to(cand, (TQ, LANES))
            below, = count([lambda blk, off: jnp.where(blk == tb, ((off + lane) < xb).astype(I32), 0)])
            return jnp.where(below < need, cand, x)

        x = lax.fori_loop(0, n_bits, jbit_body, jnp.zeros((TQ, 1), I32))
        cut = jnp.where(t == INT_MIN, -1, jnp.where(n_eq > need, x, seq))
        jb = jnp.broadcast_to(cut, (TQ, LANES))
        write_bias(lambda blk, off: jnp.where(blk > tb, 1, jnp.where(blk == tb, ((off + lane) <= jb).astype(I32), 0)) > 0)

    def fill_body(kt, carry):
        bias_ref[:, pl.ds(pl.multiple_of(kt * TK, TK), TK)] = jnp.full((TQ, TK), NEG_BIG, BF16)
        return carry

    lax.fori_loop(kt_end, seq // TK, fill_body, 0)


def _dsa_select(qi_r, wi_r, kit, n_sel):
    b, nq = qi_r.shape[0], qi_r.shape[1]
    s = kit.shape[2]
    assert n_sel <= 2 * LANES and s % TK == 0
    return pl.pallas_call(
        functools.partial(_dsa_select_kernel, n_sel=n_sel, seq=s),
        out_shape=jax.ShapeDtypeStruct((b, s, s), BF16),
        grid=(b, nq),
        in_specs=[
            pl.BlockSpec((None, None, IDX_HEADS * TQ, IDX_DIM), lambda bb, i: (bb, i, 0, 0)),
            pl.BlockSpec((None, None, IDX_HEADS * TQ, 1), lambda bb, i: (bb, i, 0, 0)),
            pl.BlockSpec((None, IDX_DIM, s), lambda bb, i: (bb, 0, 0)),
        ],
        out_specs=pl.BlockSpec((None, TQ, s), lambda bb, i: (bb, i, 0)),
        scratch_shapes=[
            pltpu.VMEM((TQ, s + LANES), I32),
            pltpu.VMEM((IDX_HEADS * TQ, LANES), F32),
        ],
        compiler_params=_cparams(("parallel", "arbitrary")),
        name="dsa_select",
    )(qi_r, wi_r, kit)


def _dsa_attend_kernel(qtab_ref, ktab_ref, q_ref, k_ref, v_ref, bias_ref, o_ref, m_scr, acc_scr):
    step = pl.program_id(1)
    i = qtab_ref[step]
    kt = ktab_ref[step]
    last_kt = (i * ATQ + ATQ - 1) // ATK

    @pl.when(kt == 0)
    def _():
        m_scr[...] = jnp.full(m_scr.shape, NEG_BIG, F32)
        acc_scr[...] = jnp.zeros(acc_scr.shape, F32)

    bias = bias_ref[...].astype(F32)
    log2e_scale = (HEAD_DIM ** -0.5) * math.log2(math.e)
    ones = jnp.ones((ATK, HEAD_DIM), BF16)
    for h in range(ATTN_HEADS):
        hs = slice(h * HEAD_DIM, (h + 1) * HEAD_DIM)
        logits = lax.dot_general(q_ref[:, hs], k_ref[:, hs], (((1,), (1,)), ((), ())),
                                 preferred_element_type=F32)
        s = logits * log2e_scale + bias
        m_old = m_scr[h]
        m_new = jnp.maximum(m_old, jnp.max(s, axis=1, keepdims=True))
        alpha = jnp.exp2(m_old - m_new)
        p = jnp.concatenate([jnp.exp2(s[:, c * LANES:(c + 1) * LANES] - m_new) for c in range(ATK // LANES)],
                            axis=1).astype(BF16)
        v_ext = jnp.concatenate([v_ref[:, hs], ones], axis=1)
        pv = jnp.dot(p, v_ext, preferred_element_type=F32)
        acc_scr[h] = jnp.concatenate([alpha, alpha], axis=1) * acc_scr[h] + pv
        m_scr[h] = m_new

    @pl.when(kt == last_kt)
    def _():
        for h in range(ATTN_HEADS):
            acc = acc_scr[h]
            o_ref[:, h * HEAD_DIM:(h + 1) * HEAD_DIM] = (acc[:, :HEAD_DIM] / acc[:, HEAD_DIM:]).astype(o_ref.dtype)


def _dsa_attend(q, k, v, bias):
    b, s, _ = q.shape
    atq = min(ATQ, s)
    assert atq == ATQ and s % ATQ == 0
    qtab, ktab = [], []
    for i in range(s // ATQ):
        for kt in range((i * ATQ + ATQ - 1) // ATK + 1):
            qtab.append(i)
            ktab.append(kt)
    grid_spec = pltpu.PrefetchScalarGridSpec(
        num_scalar_prefetch=2,
        grid=(b, len(qtab)),
        in_specs=[
            pl.BlockSpec((None, ATQ, ATTN_WIDTH), lambda bb, st, qt, kk: (bb, qt[st], 0)),
            pl.BlockSpec((None, ATK, ATTN_WIDTH), lambda bb, st, qt, kk: (bb, kk[st], 0)),
            pl.BlockSpec((None, ATK, ATTN_WIDTH), lambda bb, st, qt, kk: (bb, kk[st], 0)),
            pl.BlockSpec((None, ATQ, ATK), lambda bb, st, qt, kk: (bb, qt[st], kk[st])),
        ],
        out_specs=pl.BlockSpec((None, ATQ, ATTN_WIDTH), lambda bb, st, qt, kk: (bb, qt[st], 0)),
        scratch_shapes=[
            pltpu.VMEM((ATTN_HEADS, ATQ, LANES), F32),
            pltpu.VMEM((ATTN_HEADS, ATQ, 2 * HEAD_DIM), F32),
        ],
    )
    return pl.pallas_call(
        _dsa_attend_kernel,
        out_shape=jax.ShapeDtypeStruct((b, s, ATTN_WIDTH), BF16),
        grid_spec=grid_spec,
        compiler_params=_cparams(("parallel", "arbitrary")),
        name="dsa_attend",
    )(jnp.asarray(qtab, I32), jnp.asarray(ktab, I32), q, k, v, bias)


def _gmlp_mix_kernel(gu_ref, gv_ref, ga_ref, gb_ref, ya_ref, ng_ref, ws_ref, bst_ref, wa_ref, wb_ref,
                     o_ref, yb_scr):
    tt = gu_ref.shape[0]
    row = lax.broadcasted_iota(I32, (CHUNK, CHUNK), 0)
    col = lax.broadcasted_iota(I32, (CHUNK, CHUNK), 1)
    tril = col <= row
    for c in range(tt // CHUNK):
        rs = slice(c * CHUNK, (c + 1) * CHUNK)
        u = jax.nn.gelu(gu_ref[rs, :])
        v = jax.nn.gelu(gv_ref[rs, :])
        v = (v * lax.rsqrt(jnp.mean(v * v, axis=-1, keepdims=True) + EPS)) * ng_ref[...]
        vb = v.astype(BF16)
        for g in range(GMLP_GROUPS):
            gs = slice(g * CHUNK, (g + 1) * CHUNK)
            wm = jnp.where(tril, ws_ref[g], 0.0).astype(BF16)
            z = jnp.dot(wm, vb[:, gs], preferred_element_type=F32) + bst_ref[:, g:g + 1]
            yb_scr[rs, gs] = (u[:, gs] * z).astype(BF16)
    ma = jnp.dot(ya_ref[...], wa_ref[...], preferred_element_type=F32)
    mb = jnp.dot(yb_scr[...], wb_ref[...], preferred_element_type=F32)
    o_ref[...] = (jax.nn.sigmoid(ga_ref[...]) * ma + jax.nn.sigmoid(gb_ref[...]) * mb).astype(o_ref.dtype)


def _gmlp_mix(gates, ya, norm_g, w_s, b_s_t, wa_bf, wb_bf, d_model):
    t = ya.shape[0]
    tt = min(MIX_TT, t)
    assert d_model == 2 * GMLP_WIDTH
    return pl.pallas_call(
        _gmlp_mix_kernel,
        out_shape=jax.ShapeDtypeStruct((t, d_model), BF16),
        grid=(t // tt,),
        in_specs=[
            pl.BlockSpec((tt, GMLP_WIDTH), lambda i: (i, 0)),
            pl.BlockSpec((tt, GMLP_WIDTH), lambda i: (i, 1)),
            pl.BlockSpec((tt, d_model), lambda i: (i, 1)),
            pl.BlockSpec((tt, d_model), lambda i: (i, 2)),
            pl.BlockSpec((tt, ATTN_WIDTH), lambda i: (i, 0)),
            pl.BlockSpec((1, GMLP_WIDTH), lambda i: (0, 0)),
            pl.BlockSpec((GMLP_GROUPS, CHUNK, CHUNK), lambda i: (0, 0, 0)),
            pl.BlockSpec((CHUNK, GMLP_GROUPS), lambda i: (0, 0)),
            pl.BlockSpec((ATTN_WIDTH, d_model), lambda i: (0, 0)),
            pl.BlockSpec((GMLP_WIDTH, d_model), lambda i: (0, 0)),
        ],
        out_specs=pl.BlockSpec((tt, d_model), lambda i: (i, 0)),
        scratch_shapes=[pltpu.VMEM((tt, GMLP_WIDTH), BF16)],
        compiler_params=_cparams(("parallel",)),
        name="gmlp_mix",
    )(gates, gates, gates, gates, ya, norm_g.reshape(1, GMLP_WIDTH), w_s, b_s_t, wa_bf, wb_bf)


def _out_proj_kernel(x_ref, m_ref, wo_ref, g_ref, x1_ref, h2_ref):
    x1 = x_ref[...] + jnp.dot(m_ref[...], wo_ref[...], preferred_element_type=F32)
    x1_ref[...] = x1
    ms = jnp.mean(x1 * x1, axis=-1, keepdims=True)
    h2_ref[...] = ((x1 * lax.rsqrt(ms + EPS)) * g_ref[...]).astype(BF16)


def _out_proj(x2, mixed, wo_bf, g):
    t, d = x2.shape
    tt = min(RES_TT, t)
    return pl.pallas_call(
        _out_proj_kernel,
        out_shape=(jax.ShapeDtypeStruct((t, d), F32), jax.ShapeDtypeStruct((t, d), BF16)),
        grid=(t // tt,),
        in_specs=[
            pl.BlockSpec((tt, d), lambda i: (i, 0)),
            pl.BlockSpec((tt, d), lambda i: (i, 0)),
            pl.BlockSpec((d, d), lambda i: (0, 0)),
            pl.BlockSpec((1, d), lambda i: (0, 0)),
        ],
        out_specs=(pl.BlockSpec((tt, d), lambda i: (i, 0)), pl.BlockSpec((tt, d), lambda i: (i, 0))),
        compiler_params=_cparams(("parallel",)),
        name="out_proj",
    )(x2, mixed, wo_bf, g.reshape(1, d))


def _peer_scores_kernel(h2_ref, wq_ref, sk_ref, st_ref):
    qp = jnp.dot(h2_ref[...], wq_ref[...], preferred_element_type=F32).astype(BF16)
    half = sk_ref.shape[2]
    for hp in range(2 * PEER_HEADS):
        st_ref[hp] = lax.dot_general(sk_ref[hp], qp[:, hp * half:(hp + 1) * half], (((1,), (1,)), ((), ())),
                                     preferred_element_type=F32)


def _peer_scores(h2, wq_bf, sk_bf):
    t, d = h2.shape
    tt = min(RES_TT, t)
    nq = wq_bf.shape[1]
    half = sk_bf.shape[2]
    return pl.pallas_call(
        _peer_scores_kernel,
        out_shape=jax.ShapeDtypeStruct((2 * PEER_HEADS, N_KEYS, t), F32),
        grid=(t // tt,),
        in_specs=[
            pl.BlockSpec((tt, d), lambda i: (i, 0)),
            pl.BlockSpec((d, nq), lambda i: (0, 0)),
            pl.BlockSpec((2 * PEER_HEADS, N_KEYS, half), lambda i: (0, 0, 0)),
        ],
        out_specs=pl.BlockSpec((2 * PEER_HEADS, N_KEYS, tt), lambda i: (0, 0, i)),
        compiler_params=_cparams(("parallel",)),
        name="peer_scores",
    )(h2, wq_bf, sk_bf)


def _top_values(cur, k):
    n = cur.shape[0]
    idx = lax.broadcasted_iota(I32, cur.shape, 0).astype(F32)
    vals = []
    for _ in range(k):
        mx = jnp.max(cur, axis=0, keepdims=True)
        first = jnp.min(jnp.where(cur == mx, idx, float(n)), axis=0, keepdims=True)
        vals.append(mx)
        cur = jnp.where(idx == first, -jnp.inf, cur)
    return vals


def _staircase_sums(t1, t2):
    a1 = jnp.concatenate(t1, axis=0)
    a2 = jnp.concatenate(t2, axis=0)
    r16 = lax.broadcasted_iota(I32, (PEER_TOPK, LANES), 0)
    r8 = r16[:8]
    ninf = -jnp.inf
    return jnp.concatenate([
        t1[0] + a2,
        t1[1] + a2[:8],
        jnp.where(r16 >= 2, a1 + t2[0], ninf),
        jnp.where(r8 >= 2, a1[:8] + t2[1], ninf),
        jnp.where((r8 >= 2) & (r8 <= 4), t1[2] + a2[:8], ninf),
        jnp.where((r8 >= 2) & (r8 <= 3), t1[3] + a2[:8], ninf),
        jnp.where(r8 == 2, t1[4] + a2[:8], ninf),
    ], axis=0)


def _top_ranked(cur, k):
    n = cur.shape[0]
    idx = lax.broadcasted_iota(I32, cur.shape, 0).astype(F32)
    rank = jnp.full(cur.shape, float(k), F32)
    vals, firsts = [], []
    for r in range(k):
        mx = jnp.max(cur, axis=0, keepdims=True)
        first = jnp.min(jnp.where(cur == mx, idx, float(n)), axis=0, keepdims=True)
        hit = idx == first
        rank = jnp.where(hit, float(r), rank)
        cur = jnp.where(hit, -jnp.inf, cur)
        vals.append(mx)
        firsts.append(first)
    return vals, firsts, rank


def _peer_select_kernel(st_ref, quota_ref, e1_ref, rank_ref, e2_ref):
    tl = st_ref.shape[2]
    idx = lax.broadcasted_iota(I32, (N_KEYS, LANES), 0).astype(F32)

    def lane_group(g, carry):
        ls = pl.ds(pl.multiple_of(g * LANES, LANES), LANES)
        for h in range(PEER_HEADS):
            s1 = st_ref[2 * h, :, ls]
            s2 = st_ref[2 * h + 1, :, ls]
            t1, first1, _ = _top_ranked(s1, PEER_TOPK)
            t2, _, rank2 = _top_ranked(s2, PEER_TOPK)
            best = _top_values(_staircase_sums(t1, t2), PEER_TOPK)
            m = best[0]
            theta = best[PEER_TOPK - 1]
            z = functools.reduce(lambda a, b: a + b, [jnp.exp(bs - m) for bs in best])
            a2 = jnp.concatenate(t2, axis=0)
            quota = jnp.zeros((N_KEYS, LANES), F32)
            for a in range(PEER_TOPK):
                n_ok = jnp.sum(jnp.where(t1[a] + a2 >= theta, 1.0, 0.0), axis=0, keepdims=True)
                quota = jnp.where(idx == first1[a], n_ok, quota)
            quota_ref[h, :, ls] = quota
            e1_ref[h, :, ls] = jnp.exp(s1 - t1[0]) / z
            rank_ref[h, :, ls] = rank2.astype(BF16)
            e2_ref[h, :, ls] = jnp.exp(s2 - t2[0]).astype(BF16)
        return carry

    lax.fori_loop(0, tl // LANES, lane_group, 0)


def _peer_select(st):
    t = st.shape[2]
    tl = min(SEL_TL, t)
    f_shape = jax.ShapeDtypeStruct((PEER_HEADS, N_KEYS, t), F32)
    b_shape = jax.ShapeDtypeStruct((PEER_HEADS, N_KEYS, t), BF16)
    spec = pl.BlockSpec((PEER_HEADS, N_KEYS, tl), lambda i: (0, 0, i))
    return pl.pallas_call(
        _peer_select_kernel,
        out_shape=(f_shape, f_shape, b_shape, b_shape),
        grid=(t // tl,),
        in_specs=[pl.BlockSpec((2 * PEER_HEADS, N_KEYS, tl), lambda i: (0, 0, i))],
        out_specs=(spec, spec, spec, spec),
        compiler_params=_cparams(("parallel",)),
        name="peer_select",
    )(st)


def _peer_dense_kernel(h2_ref, u_ref, vt_ref, quota_ref, e1row_ref, rank_ref, e2_ref, o_ref,
                       rank_scr, e2_scr, act_scr, p_scr, acc_scr):
    ei = pl.program_id(1)
    tt = h2_ref.shape[0]
    pack = 16

    @pl.when(ei == 0)
    def _():
        acc_scr[...] = jnp.zeros(acc_scr.shape, F32)
        rank_scr[:, :, :tt] = rank_ref[...]
        e2_scr[:, :, :tt] = e2_ref[...]

    act_scr[:, :tt] = lax.dot_general(u_ref[...], h2_ref[...], (((1,), (1,)), ((), ())),
                                      preferred_element_type=F32)
    for ii in range(PEER_ROWS):
        for lc in range(tt // LANES):
            ls = slice(lc * LANES, (lc + 1) * LANES)
            coef = [None] * (N_KEYS // pack)
            for h in range(PEER_HEADS):
                quota = jnp.broadcast_to(quota_ref[h, ii:ii + 1, ls], (pack, LANES)).astype(BF16)
                e1row = jnp.broadcast_to(e1row_ref[h, ii:ii + 1, ls], (pack, LANES)).astype(BF16)
                for k in range(N_KEYS // pack):
                    ks = slice(k * pack, (k + 1) * pack)
                    gate = jnp.where(rank_scr[h, ks, ls] < quota, e2_scr[h, ks, ls] * e1row, 0.0)
                    coef[k] = gate if coef[k] is None else coef[k] + gate
            for k in range(N_KEYS // pack):
                rs = slice(ii * N_KEYS + k * pack, ii * N_KEYS + (k + 1) * pack)
                p_scr[rs, ls] = (coef[k].astype(F32) * jax.nn.gelu(act_scr[rs, ls])).astype(BF16)
    acc_scr[...] += jnp.dot(vt_ref[...], p_scr[:, :tt], preferred_element_type=F32)

    @pl.when(ei == pl.num_programs(1) - 1)
    def _():
        o_ref[...] = acc_scr[...].T


def _peer_dense(h2, u_bf, vt_bf, quota, e1, rank2, e2):
    t, d = h2.shape
    n_exp = u_bf.shape[0]
    tt = min(PEER_TT, t)
    te = PEER_ROWS * N_KEYS
    row_spec = pl.BlockSpec((PEER_HEADS, PEER_ROWS, tt), lambda i, e: (0, e, i))
    tok_spec = pl.BlockSpec((PEER_HEADS, N_KEYS, tt), lambda i, e: (0, 0, i))
    return pl.pallas_call(
        _peer_dense_kernel,
        out_shape=jax.ShapeDtypeStruct((t, d), F32),
        grid=(t // tt, n_exp // te),
        in_specs=[
            pl.BlockSpec((tt, d), lambda i, e: (i, 0)),
            pl.BlockSpec((te, d), lambda i, e: (e, 0)),
            pl.BlockSpec((d, te), lambda i, e: (0, e)),
            row_spec,
            row_spec,
            tok_spec,
            tok_spec,
        ],
        out_specs=pl.BlockSpec((tt, d), lambda i, e: (i, 0)),
        scratch_shapes=[
            pltpu.VMEM((PEER_HEADS, N_KEYS, tt + LANES), BF16),
            pltpu.VMEM((PEER_HEADS, N_KEYS, tt + LANES), BF16),
            pltpu.VMEM((te, tt + LANES), F32),
            pltpu.VMEM((te, tt + LANES), BF16),
            pltpu.VMEM((d, tt), F32),
        ],
        compiler_params=_cparams(("parallel", "arbitrary")),
        name="peer_dense",
    )(h2, u_bf, vt_bf, quota, e1, rank2, e2)


def _residual_norm_kernel(x1_ref, p_ref, g_ref, o_ref, *, final):
    y = x1_ref[...] + p_ref[...]
    if final:
        ms = jnp.mean(y * y, axis=-1, keepdims=True)
        y = (y * lax.rsqrt(ms + EPS)) * g_ref[...]
    o_ref[...] = y


def _residual_norm(x1, peer_out, g, final):
    t, d = x1.shape
    tt = min(NORM_TT, t)
    return pl.pallas_call(
        functools.partial(_residual_norm_kernel, final=final),
        out_shape=jax.ShapeDtypeStruct((t, d), F32),
        grid=(t // tt,),
        in_specs=[
            pl.BlockSpec((tt, d), lambda i: (i, 0)),
            pl.BlockSpec((tt, d), lambda i: (i, 0)),
            pl.BlockSpec((1, d), lambda i: (0, 0)),
        ],
        out_specs=pl.BlockSpec((tt, d), lambda i: (i, 0)),
        compiler_params=_cparams(("parallel",)),
        name="residual_norm",
    )(x1, peer_out, g.reshape(1, d))


def _layer(x2, b, s, ln_mix_g, w_in, gmlp_norm_g, w_spatial, b_spatial, w_branch_attn, w_branch_gmlp,
           w_out, ln_ffn_g, peer_w_q, peer_sub_keys, peer_u, peer_v, out_g, final):
    t, d = x2.shape
    n_qi = IDX_HEADS * IDX_DIM
    o_qi = 3 * ATTN_WIDTH
    o_ki = o_qi + n_qi
    o_wi = o_ki + IDX_DIM
    o_gate = o_wi + IDX_HEADS

    w_attn = w_in[:, :o_ki].astype(BF16)
    w_idx = jnp.pad(w_in[:, o_ki:o_gate], ((0, 0), (0, LANES - IDX_DIM - IDX_HEADS))).astype(BF16)
    w_gate = w_in[:, o_gate:].astype(BF16)

    attn_in = _norm_proj(x2, ln_mix_g, w_attn, BF16, ATTN_TN)
    idx_in = _norm_proj(x2, ln_mix_g, w_idx, F32, LANES)
    gates = _norm_proj(x2, ln_mix_g, w_gate, F32, GATE_TN)

    q = attn_in[:, :ATTN_WIDTH].reshape(b, s, ATTN_WIDTH)
    k = attn_in[:, ATTN_WIDTH:2 * ATTN_WIDTH].reshape(b, s, ATTN_WIDTH)
    v = attn_in[:, 2 * ATTN_WIDTH:o_qi].reshape(b, s, ATTN_WIDTH)
    nq = s // TQ
    qi_r = attn_in[:, o_qi:o_ki].reshape(b, nq, TQ, IDX_HEADS, IDX_DIM)
    qi_r = qi_r.transpose(0, 1, 3, 2, 4).reshape(b, nq, IDX_HEADS * TQ, IDX_DIM)
    kit = idx_in[:, :IDX_DIM].astype(BF16).reshape(b, s, IDX_DIM).transpose(0, 2, 1)
    wi_r = idx_in[:, IDX_DIM:IDX_DIM + IDX_HEADS].reshape(b, nq, TQ, IDX_HEADS)
    wi_r = wi_r.transpose(0, 1, 3, 2).reshape(b, nq, IDX_HEADS * TQ, 1)

    n_sel = min(DSA_TOPK, s // 4)
    bias = _dsa_select(qi_r, wi_r, kit, n_sel)
    ya = _dsa_attend(q, k, v, bias).reshape(t, ATTN_WIDTH)

    mixed = _gmlp_mix(gates, ya, gmlp_norm_g, w_spatial, b_spatial.T, w_branch_attn.astype(BF16),
                      w_branch_gmlp.astype(BF16), d)
    x1, h2 = _out_proj(x2, mixed, w_out.astype(BF16), ln_ffn_g)

    half = peer_sub_keys.shape[-1]
    sk = peer_sub_keys.reshape(2 * PEER_HEADS, N_KEYS, half).astype(BF16)
    st = _peer_scores(h2, peer_w_q.astype(BF16), sk)
    quota, e1, rank2, e2 = _peer_select(st)
    peer_out = _peer_dense(h2, peer_u.astype(BF16), peer_v.astype(BF16).T, quota, e1, rank2, e2)
    return _residual_norm(x1, peer_out, out_g, final)


def kernel(x, ln_mix_g, w_in, gmlp_norm_g, w_spatial, b_spatial, w_branch_attn, w_branch_gmlp, w_out, ln_ffn_g, peer_w_q, peer_sub_keys, peer_u, peer_v, ln_final_g):
    b, s, d = x.shape
    depth = w_in.shape[0]
    x2 = x.reshape(b * s, d)
    for l in range(depth):
        x2 = _layer(x2, b, s, ln_mix_g[l], w_in[l], gmlp_norm_g[l], w_spatial[l], b_spatial[l],
                    w_branch_attn[l], w_branch_gmlp[l], w_out[l], ln_ffn_g[l], peer_w_q[l],
                    peer_sub_keys[l], peer_u[l], peer_v[l], ln_final_g, l + 1 == depth)
    return x2.reshape(b, s, d)
```

```python
import functools
import math

import jax
import jax.numpy as jnp
from jax import lax
from jax.experimental import pallas as pl
from jax.experimental.pallas import tpu as pltpu

F32 = jnp.float32
BF16 = jnp.bfloat16
I32 = jnp.int32

ATTN_HEADS = 8
HEAD_DIM = 128
ATTN_WIDTH = ATTN_HEADS * HEAD_DIM
IDX_HEADS = 8
IDX_DIM = 64
DSA_TOPK = 256
GMLP_GROUPS = 8
CHUNK = 128
GMLP_WIDTH = GMLP_GROUPS * CHUNK
N_KEYS = 128
PEER_HEADS = 8
PEER_TOPK = 16
EPS = 1e-6

LANES = 128
MXU_COLS = 256
VMEM_LIMIT_BYTES = 56 * 1024 * 1024

INT_MIN = -(2 ** 31)
NEG_BIG = -1e30

TQ = 128
TK = 512
SCAN_KEYS = 2048
ATQ = 512
ATK = 512
PROJ_TM = 1024
ATTN_TN = 896
GATE_TN = 1024
NORM_TT = 512
MIX_TT = 256
RES_TT = 512
PEER_TT = 512
PEER_ROWS = 8
SEL_TL = 512


def _cparams(sem):
    return pltpu.CompilerParams(dimension_semantics=sem, vmem_limit_bytes=VMEM_LIMIT_BYTES)


def _norm_proj_kernel(x_ref, g_ref, w_ref, o_ref, h_scr):
    @pl.when(pl.program_id(1) == 0)
    def _():
        x = x_ref[...]
        ms = jnp.mean(x * x, axis=-1, keepdims=True)
        h_scr[...] = ((x * lax.rsqrt(ms + EPS)) * g_ref[...]).astype(BF16)

    o_ref[...] = jnp.dot(h_scr[...], w_ref[...], preferred_element_type=F32).astype(o_ref.dtype)


def _norm_proj(x2, g, w_bf, out_dtype, tn):
    t, d = x2.shape
    n = w_bf.shape[1]
    tm = min(PROJ_TM, t)
    return pl.pallas_call(
        _norm_proj_kernel,
        out_shape=jax.ShapeDtypeStruct((t, n), out_dtype),
        grid=(t // tm, n // tn),
        in_specs=[
            pl.BlockSpec((tm, d), lambda i, j: (i, 0)),
            pl.BlockSpec((1, d), lambda i, j: (0, 0)),
            pl.BlockSpec((d, tn), lambda i, j: (0, j)),
        ],
        out_specs=pl.BlockSpec((tm, tn), lambda i, j: (i, j)),
        scratch_shapes=[pltpu.VMEM((tm, d), BF16)],
        compiler_params=_cparams(("parallel", "arbitrary")),
        name="norm_proj",
    )(x2, g.reshape(1, d), w_bf)


def _sort_key(x):
    bits = pltpu.bitcast(x, I32)
    return bits ^ ((bits >> 31) & 0x7FFFFFFF)


def _dsa_select_kernel(qi_ref, wi_ref, kit_ref, bias_ref, keys_scr, wb_scr, *, n_sel, seq):
    i = pl.program_id(1)
    n_kt = (i * TQ + TQ + TK - 1) // TK
    wb_scr[...] = jnp.broadcast_to((wi_ref[...] * (IDX_HEADS ** -0.5)) * (IDX_DIM ** -0.5), wb_scr.shape)
    q_pos = i * TQ + lax.broadcasted_iota(I32, (TQ, LANES), 0)
    lane = lax.broadcasted_iota(I32, (TQ, LANES), 1)

    def chunk_at(off):
        return pl.ds(pl.multiple_of(off, LANES), LANES)

    def score_body(kt, carry):
        m1, m2 = carry
        off = pl.multiple_of(kt * TK, TK)
        accs = [None] * (TK // LANES)
        for half in range(TK // MXU_COLS):
            kit_tile = kit_ref[:, pl.ds(pl.multiple_of(off + half * MXU_COLS, MXU_COLS), MXU_COLS)]
            for h in range(IDX_HEADS):
                hs = slice(h * TQ, (h + 1) * TQ)
                dots = jnp.maximum(jnp.dot(qi_ref[hs, :], kit_tile, preferred_element_type=F32), 0.0)
                for cc in range(MXU_COLS // LANES):
                    c = half * (MXU_COLS // LANES) + cc
                    term = dots[:, cc * LANES:(cc + 1) * LANES] * wb_scr[hs, :]
                    accs[c] = term if accs[c] is None else accs[c] + term
        for c in range(TK // LANES):
            sc = accs[c] + 0.0
            causal = (off + c * LANES + lane) <= q_pos
            scm = jnp.where(causal, sc, -jnp.inf)
            m2 = jnp.maximum(m2, jnp.minimum(m1, scm))
            m1 = jnp.maximum(m1, scm)
            keys_scr[:, chunk_at(off + c * LANES)] = jnp.where(causal, _sort_key(sc), INT_MIN)
        return m1, m2

    neg_inf = jnp.full((TQ, LANES), -jnp.inf, F32)
    m1, m2 = lax.fori_loop(0, n_kt, score_body, (neg_inf, neg_inf))

    scan = SCAN_KEYS if seq % SCAN_KEYS == 0 else TK
    n_st = (n_kt * TK + scan - 1) // scan
    kt_end = n_st * (scan // TK)

    def pad_body(kt, carry):
        keys_scr[:, pl.ds(pl.multiple_of(kt * TK, TK), TK)] = jnp.full((TQ, TK), INT_MIN, I32)
        return carry

    lax.fori_loop(n_kt, kt_end, pad_body, 0)

    def count(preds):
        def body(st, accs):
            off = pl.multiple_of(st * scan, scan)
            for c in range(scan // LANES):
                blk = keys_scr[:, chunk_at(off + c * LANES)]
                accs = tuple(a + p(blk, off + c * LANES) for a, p in zip(accs, preds))
            return accs
        accs = lax.fori_loop(0, n_st, body, tuple(jnp.zeros((TQ, LANES), I32) for _ in preds))
        return [jnp.sum(a.astype(F32), axis=1, keepdims=True).astype(I32) for a in accs]

    low_f = jnp.min(m2, axis=1, keepdims=True)
    low_key = _sort_key(jnp.broadcast_to(low_f, (TQ, LANES)))[:, :1]
    lo0 = jnp.where(low_f == -jnp.inf, INT_MIN, low_key)
    top_f = jnp.max(m2 if n_sel > LANES else m1, axis=1, keepdims=True)
    hi0 = _sort_key(jnp.broadcast_to(top_f, (TQ, LANES)))[:, :1] + 1

    def bisect_body(state):
        lo, hi, _ = state
        mid = (lo >> 1) + (hi >> 1) + (lo & hi & 1)
        mb = jnp.broadcast_to(mid, (TQ, LANES))
        c, = count([lambda blk, off: (blk >= mb).astype(I32)])
        ge = c >= n_sel
        lo_n = jnp.where(ge, mid, lo)
        hi_n = jnp.where(c == n_sel, mid + 1, jnp.where(ge, hi, mid))
        return lo_n, hi_n, jnp.max(jnp.where(hi_n != lo_n + 1, 1.0, 0.0)).astype(F32)

    t, _, _ = lax.while_loop(lambda st: st[2] > 0.0, bisect_body, (lo0, hi0, jnp.float32(1.0)))
    tb = jnp.broadcast_to(t, (TQ, LANES))

    n_gt, n_eq = count([lambda blk, off: (blk > tb).astype(I32), lambda blk, off: (blk == tb).astype(I32)])
    need = n_sel - n_gt
    ambiguous = jnp.max(jnp.where((n_eq > need) & (t > INT_MIN), 1.0, 0.0))

    def write_bias(keep_fn):
        def body(st, carry):
            off = pl.multiple_of(st * scan, scan)
            for c in range(scan // LANES):
                blk = keys_scr[:, chunk_at(off + c * LANES)]
                bias_ref[:, chunk_at(off + c * LANES)] = jnp.where(
                    keep_fn(blk, off + c * LANES), 0.0, NEG_BIG).astype(BF16)
            return carry
        lax.fori_loop(0, n_st, body, 0)

    @pl.when(ambiguous <= 0)
    def _():
        thr = jnp.broadcast_to(jnp.maximum(t, INT_MIN + 1), (TQ, LANES))
        write_bias(lambda blk, off: blk >= thr)

    @pl.when(ambiguous > 0)
    def _():
        n_bits = max(1, (seq - 1).bit_length())

        def jbit_body(k, x):
            cand = x + lax.shift_left(jnp.int32(1), jnp.int32(n_bits - 1) - k)
            xb = jnp.broadcast_to(cand, (TQ, LANES))
            below, = count([lambda blk, off: jnp.where(blk == tb, ((off + lane) < xb).astype(I32), 0)])
            return jnp.where(below < need, cand, x)

        x = lax.fori_loop(0, n_bits, jbit_body, jnp.zeros((TQ, 1), I32))
        cut = jnp.where(t == INT_MIN, -1, jnp.where(n_eq > need, x, seq))
        jb = jnp.broadcast_to(cut, (TQ, LANES))
        write_bias(lambda blk, off: jnp.where(blk > tb, 1, jnp.where(blk == tb, ((off + lane) <= jb).astype(I32), 0)) > 0)

    def fill_body(kt, carry):
        bias_ref[:, pl.ds(pl.multiple_of(kt * TK, TK), TK)] = jnp.full((TQ, TK), NEG_BIG, BF16)
        return carry

    lax.fori_loop(kt_end, seq // TK, fill_body, 0)


def _dsa_select(qi_r, wi_r, kit, n_sel):
    b, nq = qi_r.shape[0], qi_r.shape[1]
    s = kit.shape[2]
    assert n_sel <= 2 * LANES and s % TK == 0
    return pl.pallas_call(
        functools.partial(_dsa_select_kernel, n_sel=n_sel, seq=s),
        out_shape=jax.ShapeDtypeStruct((b, s, s), BF16),
        grid=(b, nq),
        in_specs=[
            pl.BlockSpec((None, None, IDX_HEADS * TQ, IDX_DIM), lambda bb, i: (bb, i, 0, 0)),
            pl.BlockSpec((None, None, IDX_HEADS * TQ, 1), lambda bb, i: (bb, i, 0, 0)),
            pl.BlockSpec((None, IDX_DIM, s), lambda bb, i: (bb, 0, 0)),
        ],
        out_specs=pl.BlockSpec((None, TQ, s), lambda bb, i: (bb, i, 0)),
        scratch_shapes=[
            pltpu.VMEM((TQ, s + LANES), I32),
            pltpu.VMEM((IDX_HEADS * TQ, LANES), F32),
        ],
        compiler_params=_cparams(("parallel", "arbitrary")),
        name="dsa_select",
    )(qi_r, wi_r, kit)


def _dsa_attend_kernel(qtab_ref, ktab_ref, q_ref, k_ref, v_ref, bias_ref, o_ref, m_scr, acc_scr):
    step = pl.program_id(1)
    i = qtab_ref[step]
    kt = ktab_ref[step]
    last_kt = (i * ATQ + ATQ - 1) // ATK

    @pl.when(kt == 0)
    def _():
        m_scr[...] = jnp.full(m_scr.shape, NEG_BIG, F32)
        acc_scr[...] = jnp.zeros(acc_scr.shape, F32)

    bias = bias_ref[...].astype(F32)
    log2e_scale = (HEAD_DIM ** -0.5) * math.log2(math.e)
    ones = jnp.ones((ATK, HEAD_DIM), BF16)
    for h in range(ATTN_HEADS):
        hs = slice(h * HEAD_DIM, (h + 1) * HEAD_DIM)
        logits = lax.dot_general(q_ref[:, hs], k_ref[:, hs], (((1,), (1,)), ((), ())),
                                 preferred_element_type=F32)
        s = logits * log2e_scale + bias
        m_old = m_scr[h]
        m_new = jnp.maximum(m_old, jnp.max(s, axis=1, keepdims=True))
        alpha = jnp.exp2(m_old - m_new)
        p = jnp.concatenate([jnp.exp2(s[:, c * LANES:(c + 1) * LANES] - m_new) for c in range(ATK // LANES)],
                            axis=1).astype(BF16)
        v_ext = jnp.concatenate([v_ref[:, hs], ones], axis=1)
        pv = jnp.dot(p, v_ext, preferred_element_type=F32)
        acc_scr[h] = jnp.concatenate([alpha, alpha], axis=1) * acc_scr[h] + pv
        m_scr[h] = m_new

    @pl.when(kt == last_kt)
    def _():
        for h in range(ATTN_HEADS):
            acc = acc_scr[h]
            o_ref[:, h * HEAD_DIM:(h + 1) * HEAD_DIM] = (acc[:, :HEAD_DIM] / acc[:, HEAD_DIM:]).astype(o_ref.dtype)


def _dsa_attend(q, k, v, bias):
    b, s, _ = q.shape
    atq = min(ATQ, s)
    assert atq == ATQ and s % ATQ == 0
    qtab, ktab = [], []
    for i in range(s // ATQ):
        for kt in range((i * ATQ + ATQ - 1) // ATK + 1):
            qtab.append(i)
            ktab.append(kt)
    grid_spec = pltpu.PrefetchScalarGridSpec(
        num_scalar_prefetch=2,
        grid=(b, len(qtab)),
        in_specs=[
            pl.BlockSpec((None, ATQ, ATTN_WIDTH), lambda bb, st, qt, kk: (bb, qt[st], 0)),
            pl.BlockSpec((None, ATK, ATTN_WIDTH), lambda bb, st, qt, kk: (bb, kk[st], 0)),
            pl.BlockSpec((None, ATK, ATTN_WIDTH), lambda bb, st, qt, kk: (bb, kk[st], 0)),
            pl.BlockSpec((None, ATQ, ATK), lambda bb, st, qt, kk: (bb, qt[st], kk[st])),
        ],
        out_specs=pl.BlockSpec((None, ATQ, ATTN_WIDTH), lambda bb, st, qt, kk: (bb, qt[st], 0)),
        scratch_shapes=[
            pltpu.VMEM((ATTN_HEADS, ATQ, LANES), F32),
            pltpu.VMEM((ATTN_HEADS, ATQ, 2 * HEAD_DIM), F32),
        ],
    )
    return pl.pallas_call(
        _dsa_attend_kernel,
        out_shape=jax.ShapeDtypeStruct((b, s, ATTN_WIDTH), BF16),
        grid_spec=grid_spec,
        compiler_params=_cparams(("parallel", "arbitrary")),
        name="dsa_attend",
    )(jnp.asarray(qtab, I32), jnp.asarray(ktab, I32), q, k, v, bias)


def _gmlp_mix_kernel(gu_ref, gv_ref, ga_ref, gb_ref, ya_ref, ng_ref, ws_ref, bst_ref, wa_ref, wb_ref,
                     o_ref, yb_scr):
    tt = gu_ref.shape[0]
    row = lax.broadcasted_iota(I32, (CHUNK, CHUNK), 0)
    col = lax.broadcasted_iota(I32, (CHUNK, CHUNK), 1)
    tril = col <= row
    for c in range(tt // CHUNK):
        rs = slice(c * CHUNK, (c + 1) * CHUNK)
        u = jax.nn.gelu(gu_ref[rs, :])
        v = jax.nn.gelu(gv_ref[rs, :])
        v = (v * lax.rsqrt(jnp.mean(v * v, axis=-1, keepdims=True) + EPS)) * ng_ref[...]
        vb = v.astype(BF16)
        for g in range(GMLP_GROUPS):
            gs = slice(g * CHUNK, (g + 1) * CHUNK)
            wm = jnp.where(tril, ws_ref[g], 0.0).astype(BF16)
            z = jnp.dot(wm, vb[:, gs], preferred_element_type=F32) + bst_ref[:, g:g + 1]
            yb_scr[rs, gs] = (u[:, gs] * z).astype(BF16)
    ma = jnp.dot(ya_ref[...], wa_ref[...], preferred_element_type=F32)
    mb = jnp.dot(yb_scr[...], wb_ref[...], preferred_element_type=F32)
    o_ref[...] = (jax.nn.sigmoid(ga_ref[...]) * ma + jax.nn.sigmoid(gb_ref[...]) * mb).astype(o_ref.dtype)


def _gmlp_mix(gates, ya, norm_g, w_s, b_s_t, wa_bf, wb_bf, d_model):
    t = ya.shape[0]
    tt = min(MIX_TT, t)
    assert d_model == 2 * GMLP_WIDTH
    return pl.pallas_call(
        _gmlp_mix_kernel,
        out_shape=jax.ShapeDtypeStruct((t, d_model), BF16),
        grid=(t // tt,),
        in_specs=[
            pl.BlockSpec((tt, GMLP_WIDTH), lambda i: (i, 0)),
            pl.BlockSpec((tt, GMLP_WIDTH), lambda i: (i, 1)),
            pl.BlockSpec((tt, d_model), lambda i: (i, 1)),
            pl.BlockSpec((tt, d_model), lambda i: (i, 2)),
            pl.BlockSpec((tt, ATTN_WIDTH), lambda i: (i, 0)),
            pl.BlockSpec((1, GMLP_WIDTH), lambda i: (0, 0)),
            pl.BlockSpec((GMLP_GROUPS, CHUNK, CHUNK), lambda i: (0, 0, 0)),
            pl.BlockSpec((CHUNK, GMLP_GROUPS), lambda i: (0, 0)),
            pl.BlockSpec((ATTN_WIDTH, d_model), lambda i: (0, 0)),
            pl.BlockSpec((GMLP_WIDTH, d_model), lambda i: (0, 0)),
        ],
        out_specs=pl.BlockSpec((tt, d_model), lambda i: (i, 0)),
        scratch_shapes=[pltpu.VMEM((tt, GMLP_WIDTH), BF16)],
        compiler_params=_cparams(("parallel",)),
        name="gmlp_mix",
    )(gates, gates, gates, gates, ya, norm_g.reshape(1, GMLP_WIDTH), w_s, b_s_t, wa_bf, wb_bf)


def _out_proj_kernel(x_ref, m_ref, wo_ref, g_ref, x1_ref, h2_ref):
    x1 = x_ref[...] + jnp.dot(m_ref[...], wo_ref[...], preferred_element_type=F32)
    x1_ref[...] = x1
    ms = jnp.mean(x1 * x1, axis=-1, keepdims=True)
    h2_ref[...] = ((x1 * lax.rsqrt(ms + EPS)) * g_ref[...]).astype(BF16)


def _out_proj(x2, mixed, wo_bf, g):
    t, d = x2.shape
    tt = min(RES_TT, t)
    return pl.pallas_call(
        _out_proj_kernel,
        out_shape=(jax.ShapeDtypeStruct((t, d), F32), jax.ShapeDtypeStruct((t, d), BF16)),
        grid=(t // tt,),
        in_specs=[
            pl.BlockSpec((tt, d), lambda i: (i, 0)),
            pl.BlockSpec((tt, d), lambda i: (i, 0)),
            pl.BlockSpec((d, d), lambda i: (0, 0)),
            pl.BlockSpec((1, d), lambda i: (0, 0)),
        ],
        out_specs=(pl.BlockSpec((tt, d), lambda i: (i, 0)), pl.BlockSpec((tt, d), lambda i: (i, 0))),
        compiler_params=_cparams(("parallel",)),
        name="out_proj",
    )(x2, mixed, wo_bf, g.reshape(1, d))


def _peer_scores_kernel(h2_ref, wq_ref, sk_ref, st_ref):
    qp = jnp.dot(h2_ref[...], wq_ref[...], preferred_element_type=F32).astype(BF16)
    half = sk_ref.shape[2]
    for hp in range(2 * PEER_HEADS):
        st_ref[hp] = lax.dot_general(sk_ref[hp], qp[:, hp * half:(hp + 1) * half], (((1,), (1,)), ((), ())),
                                     preferred_element_type=F32)


def _peer_scores(h2, wq_bf, sk_bf):
    t, d = h2.shape
    tt = min(RES_TT, t)
    nq = wq_bf.shape[1]
    half = sk_bf.shape[2]
    return pl.pallas_call(
        _peer_scores_kernel,
        out_shape=jax.ShapeDtypeStruct((2 * PEER_HEADS, N_KEYS, t), F32),
        grid=(t // tt,),
        in_specs=[
            pl.BlockSpec((tt, d), lambda i: (i, 0)),
            pl.BlockSpec((d, nq), lambda i: (0, 0)),
            pl.BlockSpec((2 * PEER_HEADS, N_KEYS, half), lambda i: (0, 0, 0)),
        ],
        out_specs=pl.BlockSpec((2 * PEER_HEADS, N_KEYS, tt), lambda i: (0, 0, i)),
        compiler_params=_cparams(("parallel",)),
        name="peer_scores",
    )(h2, wq_bf, sk_bf)


def _top_values(cur, k):
    n = cur.shape[0]
    idx = lax.broadcasted_iota(I32, cur.shape, 0).astype(F32)
    vals = []
    for _ in range(k):
        mx = jnp.max(cur, axis=0, keepdims=True)
        first = jnp.min(jnp.where(cur == mx, idx, float(n)), axis=0, keepdims=True)
        vals.append(mx)
        cur = jnp.where(idx == first, -jnp.inf, cur)
    return vals


def _staircase_sums(t1, t2):
    a1 = jnp.concatenate(t1, axis=0)
    a2 = jnp.concatenate(t2, axis=0)
    r16 = lax.broadcasted_iota(I32, (PEER_TOPK, LANES), 0)
    r8 = r16[:8]
    ninf = -jnp.inf
    return jnp.concatenate([
        t1[0] + a2,
        t1[1] + a2[:8],
        jnp.where(r16 >= 2, a1 + t2[0], ninf),
        jnp.where(r8 >= 2, a1[:8] + t2[1], ninf),
        jnp.where((r8 >= 2) & (r8 <= 4), t1[2] + a2[:8], ninf),
        jnp.where((r8 >= 2) & (r8 <= 3), t1[3] + a2[:8], ninf),
        jnp.where(r8 == 2, t1[4] + a2[:8], ninf),
    ], axis=0)


def _top_ranked(cur, k):
    n = cur.shape[0]
    idx = lax.broadcasted_iota(I32, cur.shape, 0).astype(F32)
    rank = jnp.full(cur.shape, float(k), F32)
    vals, firsts = [], []
    for r in range(k):
        mx = jnp.max(cur, axis=0, keepdims=True)
        first = jnp.min(jnp.where(cur == mx, idx, float(n)), axis=0, keepdims=True)
        hit = idx == first
        rank = jnp.where(hit, float(r), rank)
        cur = jnp.where(hit, -jnp.inf, cur)
        vals.append(mx)
        firsts.append(first)
    return vals, firsts, rank


def _peer_select_kernel(st_ref, quota_ref, e1_ref, rank_ref, e2_ref):
    tl = st_ref.shape[2]
    idx = lax.broadcasted_iota(I32, (N_KEYS, LANES), 0).astype(F32)

    def lane_group(g, carry):
        ls = pl.ds(pl.multiple_of(g * LANES, LANES), LANES)
        for h in range(PEER_HEADS):
            s1 = st_ref[2 * h, :, ls]
            s2 = st_ref[2 * h + 1, :, ls]
            t1, first1, _ = _top_ranked(s1, PEER_TOPK)
            t2, _, rank2 = _top_ranked(s2, PEER_TOPK)
            best = _top_values(_staircase_sums(t1, t2), PEER_TOPK)
            m = best[0]
            theta = best[PEER_TOPK - 1]
            z = functools.reduce(lambda a, b: a + b, [jnp.exp(bs - m) for bs in best])
            a2 = jnp.concatenate(t2, axis=0)
            quota = jnp.zeros((N_KEYS, LANES), F32)
            for a in range(PEER_TOPK):
                n_ok = jnp.sum(jnp.where(t1[a] + a2 >= theta, 1.0, 0.0), axis=0, keepdims=True)
                quota = jnp.where(idx == first1[a], n_ok, quota)
            quota_ref[h, :, ls] = quota
            e1_ref[h, :, ls] = jnp.exp(s1 - t1[0]) / z
            rank_ref[h, :, ls] = rank2.astype(BF16)
            e2_ref[h, :, ls] = jnp.exp(s2 - t2[0]).astype(BF16)
        return carry

    lax.fori_loop(0, tl // LANES, lane_group, 0)


def _peer_select(st):
    t = st.shape[2]
    tl = min(SEL_TL, t)
    f_shape = jax.ShapeDtypeStruct((PEER_HEADS, N_KEYS, t), F32)
    b_shape = jax.ShapeDtypeStruct((PEER_HEADS, N_KEYS, t), BF16)
    spec = pl.BlockSpec((PEER_HEADS, N_KEYS, tl), lambda i: (0, 0, i))
    return pl.pallas_call(
        _peer_select_kernel,
        out_shape=(f_shape, f_shape, b_shape, b_shape),
        grid=(t // tl,),
        in_specs=[pl.BlockSpec((2 * PEER_HEADS, N_KEYS, tl), lambda i: (0, 0, i))],
        out_specs=(spec, spec, spec, spec),
        compiler_params=_cparams(("parallel",)),
        name="peer_select",
    )(st)


def _peer_dense_kernel(h2_ref, u_ref, vt_ref, quota_ref, e1row_ref, rank_ref, e2_ref, o_ref,
                       rank_scr, e2_scr, act_scr, p_scr, acc_scr):
    ei = pl.program_id(1)
    tt = h2_ref.shape[0]
    pack = 16

    @pl.when(ei == 0)
    def _():
        acc_scr[...] = jnp.zeros(acc_scr.shape, F32)
        rank_scr[:, :, :tt] = rank_ref[...]
        e2_scr[:, :, :tt] = e2_ref[...]

    act_scr[:, :tt] = lax.dot_general(u_ref[...], h2_ref[...], (((1,), (1,)), ((), ())),
                                      preferred_element_type=F32)
    for ii in range(PEER_ROWS):
        for lc in range(tt // LANES):
            ls = slice(lc * LANES, (lc + 1) * LANES)
            coef = [None] * (N_KEYS // pack)
            for h in range(PEER_HEADS):
                quota = jnp.broadcast_to(quota_ref[h, ii:ii + 1, ls], (pack, LANES)).astype(BF16)
                e1row = jnp.broadcast_to(e1row_ref[h, ii:ii + 1, ls], (pack, LANES)).astype(BF16)
                for k in range(N_KEYS // pack):
                    ks = slice(k * pack, (k + 1) * pack)
                    gate = jnp.where(rank_scr[h, ks, ls] < quota, e2_scr[h, ks, ls] * e1row, 0.0)
                    coef[k] = gate if coef[k] is None else coef[k] + gate
            for k in range(N_KEYS // pack):
                rs = slice(ii * N_KEYS + k * pack, ii * N_KEYS + (k + 1) * pack)
                p_scr[rs, ls] = coef[k] * jax.nn.gelu(act_scr[rs, ls].astype(BF16))
    acc_scr[...] += jnp.dot(vt_ref[...], p_scr[:, :tt], preferred_element_type=F32)

    @pl.when(ei == pl.num_programs(1) - 1)
    def _():
        o_ref[...] = acc_scr[...].T


def _peer_dense(h2, u_bf, vt_bf, quota, e1, rank2, e2):
    t, d = h2.shape
    n_exp = u_bf.shape[0]
    tt = min(PEER_TT, t)
    te = PEER_ROWS * N_KEYS
    row_spec = pl.BlockSpec((PEER_HEADS, PEER_ROWS, tt), lambda i, e: (0, e, i))
    tok_spec = pl.BlockSpec((PEER_HEADS, N_KEYS, tt), lambda i, e: (0, 0, i))
    return pl.pallas_call(
        _peer_dense_kernel,
        out_shape=jax.ShapeDtypeStruct((t, d), F32),
        grid=(t // tt, n_exp // te),
        in_specs=[
            pl.BlockSpec((tt, d), lambda i, e: (i, 0)),
            pl.BlockSpec((te, d), lambda i, e: (e, 0)),
            pl.BlockSpec((d, te), lambda i, e: (0, e)),
            row_spec,
            row_spec,
            tok_spec,
            tok_spec,
        ],
        out_specs=pl.BlockSpec((tt, d), lambda i, e: (i, 0)),
        scratch_shapes=[
            pltpu.VMEM((PEER_HEADS, N_KEYS, tt + LANES), BF16),
            pltpu.VMEM((PEER_HEADS, N_KEYS, tt + LANES), BF16),
            pltpu.VMEM((te, tt + LANES), F32),
            pltpu.VMEM((te, tt + LANES), BF16),
            pltpu.VMEM((d, tt), F32),
        ],
        compiler_params=_cparams(("parallel", "arbitrary")),
        name="peer_dense",
    )(h2, u_bf, vt_bf, quota, e1, rank2, e2)


def _residual_norm_kernel(x1_ref, p_ref, g_ref, o_ref, *, final):
    y = x1_ref[...] + p_ref[...]
    if final:
        ms = jnp.mean(y * y, axis=-1, keepdims=True)
        y = (y * lax.rsqrt(ms + EPS)) * g_ref[...]
    o_ref[...] = y


def _residual_norm(x1, peer_out, g, final):
    t, d = x1.shape
    tt = min(NORM_TT, t)
    return pl.pallas_call(
        functools.partial(_residual_norm_kernel, final=final),
        out_shape=jax.ShapeDtypeStruct((t, d), F32),
        grid=(t // tt,),
        in_specs=[
            pl.BlockSpec((tt, d), lambda i: (i, 0)),
            pl.BlockSpec((tt, d), lambda i: (i, 0)),
            pl.BlockSpec((1, d), lambda i: (0, 0)),
        ],
        out_specs=pl.BlockSpec((tt, d), lambda i: (i, 0)),
        compiler_params=_cparams(("parallel",)),
        name="residual_norm",
    )(x1, peer_out, g.reshape(1, d))


def _layer(x2, b, s, ln_mix_g, w_in, gmlp_norm_g, w_spatial, b_spatial, w_branch_attn, w_branch_gmlp,
           w_out, ln_ffn_g, peer_w_q, peer_sub_keys, peer_u, peer_v, out_g, final):
    t, d = x2.shape
    n_qi = IDX_HEADS * IDX_DIM
    o_qi = 3 * ATTN_WIDTH
    o_ki = o_qi + n_qi
    o_wi = o_ki + IDX_DIM
    o_gate = o_wi + IDX_HEADS

    w_attn = w_in[:, :o_ki].astype(BF16)
    w_idx = jnp.pad(w_in[:, o_ki:o_gate], ((0, 0), (0, LANES - IDX_DIM - IDX_HEADS))).astype(BF16)
    w_gate = w_in[:, o_gate:].astype(BF16)

    attn_in = _norm_proj(x2, ln_mix_g, w_attn, BF16, ATTN_TN)
    idx_in = _norm_proj(x2, ln_mix_g, w_idx, F32, LANES)
    gates = _norm_proj(x2, ln_mix_g, w_gate, F32, GATE_TN)

    q = attn_in[:, :ATTN_WIDTH].reshape(b, s, ATTN_WIDTH)
    k = attn_in[:, ATTN_WIDTH:2 * ATTN_WIDTH].reshape(b, s, ATTN_WIDTH)
    v = attn_in[:, 2 * ATTN_WIDTH:o_qi].reshape(b, s, ATTN_WIDTH)
    nq = s // TQ
    qi_r = attn_in[:, o_qi:o_ki].reshape(b, nq, TQ, IDX_HEADS, IDX_DIM)
    qi_r = qi_r.transpose(0, 1, 3, 2, 4).reshape(b, nq, IDX_HEADS * TQ, IDX_DIM)
    kit = idx_in[:, :IDX_DIM].astype(BF16).reshape(b, s, IDX_DIM).transpose(0, 2, 1)
    wi_r = idx_in[:, IDX_DIM:IDX_DIM + IDX_HEADS].reshape(b, nq, TQ, IDX_HEADS)
    wi_r = wi_r.transpose(0, 1, 3, 2).reshape(b, nq, IDX_HEADS * TQ, 1)

    n_sel = min(DSA_TOPK, s // 4)
    bias = _dsa_select(qi_r, wi_r, kit, n_sel)
    ya = _dsa_attend(q, k, v, bias).reshape(t, ATTN_WIDTH)

    mixed = _gmlp_mix(gates, ya, gmlp_norm_g, w_spatial, b_spatial.T, w_branch_attn.astype(BF16),
                      w_branch_gmlp.astype(BF16), d)
    x1, h2 = _out_proj(x2, mixed, w_out.astype(BF16), ln_ffn_g)

    half = peer_sub_keys.shape[-1]
    sk = peer_sub_keys.reshape(2 * PEER_HEADS, N_KEYS, half).astype(BF16)
    st = _peer_scores(h2, peer_w_q.astype(BF16), sk)
    quota, e1, rank2, e2 = _peer_select(st)
    peer_out = _peer_dense(h2, peer_u.astype(BF16), peer_v.astype(BF16).T, quota, e1, rank2, e2)
    return _residual_norm(x1, peer_out, out_g, final)


def kernel(x, ln_mix_g, w_in, gmlp_norm_g, w_spatial, b_spatial, w_branch_attn, w_branch_gmlp, w_out, ln_ffn_g, peer_w_q, peer_sub_keys, peer_u, peer_v, ln_final_g):
    b, s, d = x.shape
    depth = w_in.shape[0]
    x2 = x.reshape(b * s, d)
    for l in range(depth):
        x2 = _layer(x2, b, s, ln_mix_g[l], w_in[l], gmlp_norm_g[l], w_spatial[l], b_spatial[l],
                    w_branch_attn[l], w_branch_gmlp[l], w_out[l], ln_ffn_g[l], peer_w_q[l],
                    peer_sub_keys[l], peer_u[l], peer_v[l], ln_final_g, l + 1 == depth)
    return x2.reshape(b, s, d)
```

```python
import functools
import math

import jax
import jax.numpy as jnp
from jax import lax
from jax.experimental import pallas as pl
from jax.experimental.pallas import tpu as pltpu

F32 = jnp.float32
BF16 = jnp.bfloat16
I32 = jnp.int32

ATTN_HEADS = 8
HEAD_DIM = 128
ATTN_WIDTH = ATTN_HEADS * HEAD_DIM
IDX_HEADS = 8
IDX_DIM = 64
DSA_TOPK = 256
GMLP_GROUPS = 8
CHUNK = 128
GMLP_WIDTH = GMLP_GROUPS * CHUNK
N_KEYS = 128
PEER_HEADS = 8
PEER_TOPK = 16
EPS = 1e-6

LANES = 128
MXU_COLS = 256
VMEM_LIMIT_BYTES = 56 * 1024 * 1024

INT_MIN = -(2 ** 31)
NEG_BIG = -1e30

TQ = 128
TK = 512
SCAN_KEYS = 1024
ATQ = 512
ATK = 512
PROJ_TM = 1024
ATTN_TN = 896
GATE_TN = 1024
NORM_TT = 512
MIX_TT = 256
RES_TT = 512
PEER_TT = 512
PEER_ROWS = 8
SEL_TL = 512


def _cparams(sem):
    return pltpu.CompilerParams(dimension_semantics=sem, vmem_limit_bytes=VMEM_LIMIT_BYTES)


def _norm_proj_kernel(x_ref, g_ref, w_ref, o_ref, h_scr):
    @pl.when(pl.program_id(1) == 0)
    def _():
        x = x_ref[...]
        ms = jnp.mean(x * x, axis=-1, keepdims=True)
        h_scr[...] = ((x * lax.rsqrt(ms + EPS)) * g_ref[...]).astype(BF16)

    o_ref[...] = jnp.dot(h_scr[...], w_ref[...], preferred_element_type=F32).astype(o_ref.dtype)


def _norm_proj(x2, g, w_bf, out_dtype, tn):
    t, d = x2.shape
    n = w_bf.shape[1]
    tm = min(PROJ_TM, t)
    return pl.pallas_call(
        _norm_proj_kernel,
        out_shape=jax.ShapeDtypeStruct((t, n), out_dtype),
        grid=(t // tm, n // tn),
        in_specs=[
            pl.BlockSpec((tm, d), lambda i, j: (i, 0)),
            pl.BlockSpec((1, d), lambda i, j: (0, 0)),
            pl.BlockSpec((d, tn), lambda i, j: (0, j)),
        ],
        out_specs=pl.BlockSpec((tm, tn), lambda i, j: (i, j)),
        scratch_shapes=[pltpu.VMEM((tm, d), BF16)],
        compiler_params=_cparams(("parallel", "arbitrary")),
        name="norm_proj",
    )(x2, g.reshape(1, d), w_bf)


def _sort_key(x):
    bits = pltpu.bitcast(x, I32)
    return bits ^ ((bits >> 31) & 0x7FFFFFFF)


def _dsa_select_kernel(qi_ref, wi_ref, kit_ref, bias_ref, keys_scr, wb_scr, *, n_sel, seq):
    i = pl.program_id(1)
    n_kt = (i * TQ + TQ + TK - 1) // TK
    wb_scr[...] = jnp.broadcast_to((wi_ref[...] * (IDX_HEADS ** -0.5)) * (IDX_DIM ** -0.5), wb_scr.shape)
    q_pos = i * TQ + lax.broadcasted_iota(I32, (TQ, LANES), 0)
    lane = lax.broadcasted_iota(I32, (TQ, LANES), 1)

    def chunk_at(off):
        return pl.ds(pl.multiple_of(off, LANES), LANES)

    def score_body(kt, carry):
        m1, m2 = carry
        off = pl.multiple_of(kt * TK, TK)
        accs = [None] * (TK // LANES)
        for half in range(TK // MXU_COLS):
            kit_tile = kit_ref[:, pl.ds(pl.multiple_of(off + half * MXU_COLS, MXU_COLS), MXU_COLS)]
            for h in range(IDX_HEADS):
                hs = slice(h * TQ, (h + 1) * TQ)
                dots = jnp.maximum(jnp.dot(qi_ref[hs, :], kit_tile, preferred_element_type=F32), 0.0)
                for cc in range(MXU_COLS // LANES):
                    c = half * (MXU_COLS // LANES) + cc
                    term = dots[:, cc * LANES:(cc + 1) * LANES] * wb_scr[hs, :]
                    accs[c] = term if accs[c] is None else accs[c] + term
        for c in range(TK // LANES):
            sc = accs[c] + 0.0
            causal = (off + c * LANES + lane) <= q_pos
            scm = jnp.where(causal, sc, -jnp.inf)
            m2 = jnp.maximum(m2, jnp.minimum(m1, scm))
            m1 = jnp.maximum(m1, scm)
            keys_scr[:, chunk_at(off + c * LANES)] = jnp.where(causal, _sort_key(sc), INT_MIN)
        return m1, m2

    neg_inf = jnp.full((TQ, LANES), -jnp.inf, F32)
    m1, m2 = lax.fori_loop(0, n_kt, score_body, (neg_inf, neg_inf))

    scan = SCAN_KEYS if seq % SCAN_KEYS == 0 else TK
    n_st = (n_kt * TK + scan - 1) // scan
    kt_end = n_st * (scan // TK)

    def pad_body(kt, carry):
        keys_scr[:, pl.ds(pl.multiple_of(kt * TK, TK), TK)] = jnp.full((TQ, TK), INT_MIN, I32)
        return carry

    lax.fori_loop(n_kt, kt_end, pad_body, 0)

    def count(preds):
        def body(st, accs):
            off = pl.multiple_of(st * scan, scan)
            for c in range(scan // LANES):
                blk = keys_scr[:, chunk_at(off + c * LANES)]
                accs = tuple(a + p(blk, off + c * LANES) for a, p in zip(accs, preds))
            return accs
        accs = lax.fori_loop(0, n_st, body, tuple(jnp.zeros((TQ, LANES), I32) for _ in preds))
        return [jnp.sum(a.astype(F32), axis=1, keepdims=True).astype(I32) for a in accs]

    low_f = jnp.min(m2, axis=1, keepdims=True)
    low_key = _sort_key(jnp.broadcast_to(low_f, (TQ, LANES)))[:, :1]
    lo0 = jnp.where(low_f == -jnp.inf, INT_MIN, low_key)
    top_f = jnp.max(m2 if n_sel > LANES else m1, axis=1, keepdims=True)
    hi0 = _sort_key(jnp.broadcast_to(top_f, (TQ, LANES)))[:, :1] + 1

    def bisect_body(state):
        lo, hi, _ = state
        mid = (lo >> 1) + (hi >> 1) + (lo & hi & 1)
        mb = jnp.broadcast_to(mid, (TQ, LANES))
        c, = count([lambda blk, off: (blk >= mb).astype(I32)])
        ge = c >= n_sel
        lo_n = jnp.where(ge, mid, lo)
        hi_n = jnp.where(c == n_sel, mid + 1, jnp.where(ge, hi, mid))
        return lo_n, hi_n, jnp.max(jnp.where(hi_n != lo_n + 1, 1.0, 0.0)).astype(F32)

    t, _, _ = lax.while_loop(lambda st: st[2] > 0.0, bisect_body, (lo0, hi0, jnp.float32(1.0)))
    tb = jnp.broadcast_to(t, (TQ, LANES))

    n_gt, n_eq = count([lambda blk, off: (blk > tb).astype(I32), lambda blk, off: (blk == tb).astype(I32)])
    need = n_sel - n_gt
    ambiguous = jnp.max(jnp.where((n_eq > need) & (t > INT_MIN), 1.0, 0.0))

    def write_bias(keep_fn):
        def body(st, carry):
            off = pl.multiple_of(st * scan, scan)
            for c in range(scan // LANES):
                blk = keys_scr[:, chunk_at(off + c * LANES)]
                bias_ref[:, chunk_at(off + c * LANES)] = jnp.where(
                    keep_fn(blk, off + c * LANES), 0.0, NEG_BIG).astype(BF16)
            return carry
        lax.fori_loop(0, n_st, body, 0)

    @pl.when(ambiguous <= 0)
    def _():
        thr = jnp.broadcast_to(jnp.maximum(t, INT_MIN + 1), (TQ, LANES))
        write_bias(lambda blk, off: blk >= thr)

    @pl.when(ambiguous > 0)
    def _():
        n_bits = max(1, (seq - 1).bit_length())

        def jbit_body(k, x):
            cand = x + lax.shift_left(jnp.int32(1), jnp.int32(n_bits - 1) - k)
            xb = jnp.broadcast_to(cand, (TQ, LANES))
            below, = count([lambda blk, off: jnp.where(blk == tb, ((off + lane) < xb).astype(I32), 0)])
            return jnp.where(below < need, cand, x)

        x = lax.fori_loop(0, n_bits, jbit_body, jnp.zeros((TQ, 1), I32))
        cut = jnp.where(t == INT_MIN, -1, jnp.where(n_eq > need, x, seq))
        jb = jnp.broadcast_to(cut, (TQ, LANES))
        write_bias(lambda blk, off: jnp.where(blk > tb, 1, jnp.where(blk == tb, ((off + lane) <= jb).astype(I32), 0)) > 0)

    def fill_body(kt, carry):
        bias_ref[:, pl.ds(pl.multiple_of(kt * TK, TK), TK)] = jnp.full((TQ, TK), NEG_BIG, BF16)
        return carry

    lax.fori_loop(kt_end, seq // TK, fill_body, 0)


def _dsa_select(qi_r, wi_r, kit, n_sel):
    b, nq = qi_r.shape[0], qi_r.shape[1]
    s = kit.shape[2]
    assert n_sel <= 2 * LANES and s % TK == 0
    return pl.pallas_call(
        functools.partial(_dsa_select_kernel, n_sel=n_sel, seq=s),
        out_shape=jax.ShapeDtypeStruct((b, s, s), BF16),
        grid=(b, nq),
        in_specs=[
            pl.BlockSpec((None, None, IDX_HEADS * TQ, IDX_DIM), lambda bb, i: (bb, i, 0, 0)),
            pl.BlockSpec((None, None, IDX_HEADS * TQ, 1), lambda bb, i: (bb, i, 0, 0)),
            pl.BlockSpec((None, IDX_DIM, s), lambda bb, i: (bb, 0, 0)),
        ],
        out_specs=pl.BlockSpec((None, TQ, s), lambda bb, i: (bb, i, 0)),
        scratch_shapes=[
            pltpu.VMEM((TQ, s + LANES), I32),
            pltpu.VMEM((IDX_HEADS * TQ, LANES), F32),
        ],
        compiler_params=_cparams(("parallel", "arbitrary")),
        name="dsa_select",
    )(qi_r, wi_r, kit)


def _dsa_attend_kernel(qtab_ref, ktab_ref, q_ref, k_ref, v_ref, bias_ref, o_ref, m_scr, acc_scr):
    step = pl.program_id(1)
    i = qtab_ref[step]
    kt = ktab_ref[step]
    last_kt = (i * ATQ + ATQ - 1) // ATK

    @pl.when(kt == 0)
    def _():
        m_scr[...] = jnp.full(m_scr.shape, NEG_BIG, F32)
        acc_scr[...] = jnp.zeros(acc_scr.shape, F32)

    bias = bias_ref[...].astype(F32)
    log2e_scale = (HEAD_DIM ** -0.5) * math.log2(math.e)
    ones = jnp.ones((ATK, HEAD_DIM), BF16)
    for h in range(ATTN_HEADS):
        hs = slice(h * HEAD_DIM, (h + 1) * HEAD_DIM)
        logits = lax.dot_general(q_ref[:, hs], k_ref[:, hs], (((1,), (1,)), ((), ())),
                                 preferred_element_type=F32)
        s = logits * log2e_scale + bias
        m_old = m_scr[h]
        m_new = jnp.maximum(m_old, jnp.max(s, axis=1, keepdims=True))
        alpha = jnp.exp2(m_old - m_new)
        p = jnp.concatenate([jnp.exp2(s[:, c * LANES:(c + 1) * LANES] - m_new) for c in range(ATK // LANES)],
                            axis=1).astype(BF16)
        v_ext = jnp.concatenate([v_ref[:, hs], ones], axis=1)
        pv = jnp.dot(p, v_ext, preferred_element_type=F32)
        acc_scr[h] = jnp.concatenate([alpha, alpha], axis=1) * acc_scr[h] + pv
        m_scr[h] = m_new

    @pl.when(kt == last_kt)
    def _():
        for h in range(ATTN_HEADS):
            acc = acc_scr[h]
            o_ref[:, h * HEAD_DIM:(h + 1) * HEAD_DIM] = (acc[:, :HEAD_DIM] / acc[:, HEAD_DIM:]).astype(o_ref.dtype)


def _dsa_attend(q, k, v, bias):
    b, s, _ = q.shape
    atq = min(ATQ, s)
    assert atq == ATQ and s % ATQ == 0
    qtab, ktab = [], []
    for i in range(s // ATQ):
        for kt in range((i * ATQ + ATQ - 1) // ATK + 1):
            qtab.append(i)
            ktab.append(kt)
    grid_spec = pltpu.PrefetchScalarGridSpec(
        num_scalar_prefetch=2,
        grid=(b, len(qtab)),
        in_specs=[
            pl.BlockSpec((None, ATQ, ATTN_WIDTH), lambda bb, st, qt, kk: (bb, qt[st], 0)),
            pl.BlockSpec((None, ATK, ATTN_WIDTH), lambda bb, st, qt, kk: (bb, kk[st], 0)),
            pl.BlockSpec((None, ATK, ATTN_WIDTH), lambda bb, st, qt, kk: (bb, kk[st], 0)),
            pl.BlockSpec((None, ATQ, ATK), lambda bb, st, qt, kk: (bb, qt[st], kk[st])),
        ],
        out_specs=pl.BlockSpec((None, ATQ, ATTN_WIDTH), lambda bb, st, qt, kk: (bb, qt[st], 0)),
        scratch_shapes=[
            pltpu.VMEM((ATTN_HEADS, ATQ, LANES), F32),
            pltpu.VMEM((ATTN_HEADS, ATQ, 2 * HEAD_DIM), F32),
        ],
    )
    return pl.pallas_call(
        _dsa_attend_kernel,
        out_shape=jax.ShapeDtypeStruct((b, s, ATTN_WIDTH), BF16),
        grid_spec=grid_spec,
        compiler_params=_cparams(("parallel", "arbitrary")),
        name="dsa_attend",
    )(jnp.asarray(qtab, I32), jnp.asarray(ktab, I32), q, k, v, bias)


def _gmlp_mix_kernel(gu_ref, gv_ref, ga_ref, gb_ref, ya_ref, ng_ref, ws_ref, bst_ref, wa_ref, wb_ref,
                     o_ref, yb_scr):
    tt = gu_ref.shape[0]
    row = lax.broadcasted_iota(I32, (CHUNK, CHUNK), 0)
    col = lax.broadcasted_iota(I32, (CHUNK, CHUNK), 1)
    tril = col <= row
    for c in range(tt // CHUNK):
        rs = slice(c * CHUNK, (c + 1) * CHUNK)
        u = jax.nn.gelu(gu_ref[rs, :])
        v = jax.nn.gelu(gv_ref[rs, :])
        v = (v * lax.rsqrt(jnp.mean(v * v, axis=-1, keepdims=True) + EPS)) * ng_ref[...]
        vb = v.astype(BF16)
        for g in range(GMLP_GROUPS):
            gs = slice(g * CHUNK, (g + 1) * CHUNK)
            wm = jnp.where(tril, ws_ref[g], 0.0).astype(BF16)
            z = jnp.dot(wm, vb[:, gs], preferred_element_type=F32) + bst_ref[:, g:g + 1]
            yb_scr[rs, gs] = (u[:, gs] * z).astype(BF16)
    ma = jnp.dot(ya_ref[...], wa_ref[...], preferred_element_type=F32)
    mb = jnp.dot(yb_scr[...], wb_ref[...], preferred_element_type=F32)
    o_ref[...] = (jax.nn.sigmoid(ga_ref[...]) * ma + jax.nn.sigmoid(gb_ref[...]) * mb).astype(o_ref.dtype)


def _gmlp_mix(gates, ya, norm_g, w_s, b_s_t, wa_bf, wb_bf, d_model):
    t = ya.shape[0]
    tt = min(MIX_TT, t)
    assert d_model == 2 * GMLP_WIDTH
    return pl.pallas_call(
        _gmlp_mix_kernel,
        out_shape=jax.ShapeDtypeStruct((t, d_model), BF16),
        grid=(t // tt,),
        in_specs=[
            pl.BlockSpec((tt, GMLP_WIDTH), lambda i: (i, 0)),
            pl.BlockSpec((tt, GMLP_WIDTH), lambda i: (i, 1)),
            pl.BlockSpec((tt, d_model), lambda i: (i, 1)),
            pl.BlockSpec((tt, d_model), lambda i: (i, 2)),
            pl.BlockSpec((tt, ATTN_WIDTH), lambda i: (i, 0)),
            pl.BlockSpec((1, GMLP_WIDTH), lambda i: (0, 0)),
            pl.BlockSpec((GMLP_GROUPS, CHUNK, CHUNK), lambda i: (0, 0, 0)),
            pl.BlockSpec((CHUNK, GMLP_GROUPS), lambda i: (0, 0)),
            pl.BlockSpec((ATTN_WIDTH, d_model), lambda i: (0, 0)),
            pl.BlockSpec((GMLP_WIDTH, d_model), lambda i: (0, 0)),
        ],
        out_specs=pl.BlockSpec((tt, d_model), lambda i: (i, 0)),
        scratch_shapes=[pltpu.VMEM((tt, GMLP_WIDTH), BF16)],
        compiler_params=_cparams(("parallel",)),
        name="gmlp_mix",
    )(gates, gates, gates, gates, ya, norm_g.reshape(1, GMLP_WIDTH), w_s, b_s_t, wa_bf, wb_bf)


def _out_proj_kernel(x_ref, m_ref, wo_ref, g_ref, x1_ref, h2_ref):
    x1 = x_ref[...] + jnp.dot(m_ref[...], wo_ref[...], preferred_element_type=F32)
    x1_ref[...] = x1
    ms = jnp.mean(x1 * x1, axis=-1, keepdims=True)
    h2_ref[...] = ((x1 * lax.rsqrt(ms + EPS)) * g_ref[...]).astype(BF16)


def _out_proj(x2, mixed, wo_bf, g):
    t, d = x2.shape
    tt = min(RES_TT, t)
    return pl.pallas_call(
        _out_proj_kernel,
        out_shape=(jax.ShapeDtypeStruct((t, d), F32), jax.ShapeDtypeStruct((t, d), BF16)),
        grid=(t // tt,),
        in_specs=[
            pl.BlockSpec((tt, d), lambda i: (i, 0)),
            pl.BlockSpec((tt, d), lambda i: (i, 0)),
            pl.BlockSpec((d, d), lambda i: (0, 0)),
            pl.BlockSpec((1, d), lambda i: (0, 0)),
        ],
        out_specs=(pl.BlockSpec((tt, d), lambda i: (i, 0)), pl.BlockSpec((tt, d), lambda i: (i, 0))),
        compiler_params=_cparams(("parallel",)),
        name="out_proj",
    )(x2, mixed, wo_bf, g.reshape(1, d))


def _peer_scores_kernel(h2_ref, wq_ref, sk_ref, st_ref):
    qp = jnp.dot(h2_ref[...], wq_ref[...], preferred_element_type=F32).astype(BF16)
    half = sk_ref.shape[2]
    for hp in range(2 * PEER_HEADS):
        st_ref[hp] = lax.dot_general(sk_ref[hp], qp[:, hp * half:(hp + 1) * half], (((1,), (1,)), ((), ())),
                                     preferred_element_type=F32)


def _peer_scores(h2, wq_bf, sk_bf):
    t, d = h2.shape
    tt = min(RES_TT, t)
    nq = wq_bf.shape[1]
    half = sk_bf.shape[2]
    return pl.pallas_call(
        _peer_scores_kernel,
        out_shape=jax.ShapeDtypeStruct((2 * PEER_HEADS, N_KEYS, t), F32),
        grid=(t // tt,),
        in_specs=[
            pl.BlockSpec((tt, d), lambda i: (i, 0)),
            pl.BlockSpec((d, nq), lambda i: (0, 0)),
            pl.BlockSpec((2 * PEER_HEADS, N_KEYS, half), lambda i: (0, 0, 0)),
        ],
        out_specs=pl.BlockSpec((2 * PEER_HEADS, N_KEYS, tt), lambda i: (0, 0, i)),
        compiler_params=_cparams(("parallel",)),
        name="peer_scores",
    )(h2, wq_bf, sk_bf)


def _top_values(cur, k):
    n = cur.shape[0]
    idx = lax.broadcasted_iota(I32, cur.shape, 0).astype(F32)
    vals = []
    for _ in range(k):
        mx = jnp.max(cur, axis=0, keepdims=True)
        first = jnp.min(jnp.where(cur == mx, idx, float(n)), axis=0, keepdims=True)
        vals.append(mx)
        cur = jnp.where(idx == first, -jnp.inf, cur)
    return vals


def _staircase_sums(t1, t2):
    a1 = jnp.concatenate(t1, axis=0)
    a2 = jnp.concatenate(t2, axis=0)
    r16 = lax.broadcasted_iota(I32, (PEER_TOPK, LANES), 0)
    r8 = r16[:8]
    ninf = -jnp.inf
    return jnp.concatenate([
        t1[0] + a2,
        t1[1] + a2[:8],
        jnp.where(r16 >= 2, a1 + t2[0], ninf),
        jnp.where(r8 >= 2, a1[:8] + t2[1], ninf),
        jnp.where((r8 >= 2) & (r8 <= 4), t1[2] + a2[:8], ninf),
        jnp.where((r8 >= 2) & (r8 <= 3), t1[3] + a2[:8], ninf),
        jnp.where(r8 == 2, t1[4] + a2[:8], ninf),
    ], axis=0)


def _top_ranked(cur, k):
    n = cur.shape[0]
    idx = lax.broadcasted_iota(I32, cur.shape, 0).astype(F32)
    rank = jnp.full(cur.shape, float(k), F32)
    vals, firsts = [], []
    for r in range(k):
        mx = jnp.max(cur, axis=0, keepdims=True)
        first = jnp.min(jnp.where(cur == mx, idx, float(n)), axis=0, keepdims=True)
        hit = idx == first
        rank = jnp.where(hit, float(r), rank)
        cur = jnp.where(hit, -jnp.inf, cur)
        vals.append(mx)
        firsts.append(first)
    return vals, firsts, rank


def _peer_select_kernel(st_ref, quota_ref, e1_ref, rank_ref, e2_ref):
    tl = st_ref.shape[2]
    idx = lax.broadcasted_iota(I32, (N_KEYS, LANES), 0).astype(F32)

    def lane_group(g, carry):
        ls = pl.ds(pl.multiple_of(g * LANES, LANES), LANES)
        for h in range(PEER_HEADS):
            s1 = st_ref[2 * h, :, ls]
            s2 = st_ref[2 * h + 1, :, ls]
            t1, first1, _ = _top_ranked(s1, PEER_TOPK)
            t2, _, rank2 = _top_ranked(s2, PEER_TOPK)
            best = _top_values(_staircase_sums(t1, t2), PEER_TOPK)
            m = best[0]
            theta = best[PEER_TOPK - 1]
            z = functools.reduce(lambda a, b: a + b, [jnp.exp(bs - m) for bs in best])
            a2 = jnp.concatenate(t2, axis=0)
            quota = jnp.zeros((N_KEYS, LANES), F32)
            for a in range(PEER_TOPK):
                n_ok = jnp.sum(jnp.where(t1[a] + a2 >= theta, 1.0, 0.0), axis=0, keepdims=True)
                quota = jnp.where(idx == first1[a], n_ok, quota)
            quota_ref[h, :, ls] = quota
            e1_ref[h, :, ls] = jnp.exp(s1 - t1[0]) / z
            rank_ref[h, :, ls] = rank2.astype(BF16)
            e2_ref[h, :, ls] = jnp.exp(s2 - t2[0]).astype(BF16)
        return carry

    lax.fori_loop(0, tl // LANES, lane_group, 0)


def _peer_select(st):
    t = st.shape[2]
    tl = min(SEL_TL, t)
    f_shape = jax.ShapeDtypeStruct((PEER_HEADS, N_KEYS, t), F32)
    b_shape = jax.ShapeDtypeStruct((PEER_HEADS, N_KEYS, t), BF16)
    spec = pl.BlockSpec((PEER_HEADS, N_KEYS, tl), lambda i: (0, 0, i))
    return pl.pallas_call(
        _peer_select_kernel,
        out_shape=(f_shape, f_shape, b_shape, b_shape),
        grid=(t // tl,),
        in_specs=[pl.BlockSpec((2 * PEER_HEADS, N_KEYS, tl), lambda i: (0, 0, i))],
        out_specs=(spec, spec, spec, spec),
        compiler_params=_cparams(("parallel",)),
        name="peer_select",
    )(st)


def _peer_dense_kernel(h2_ref, u_ref, vt_ref, quota_ref, e1row_ref, rank_ref, e2_ref, o_ref,
                       rank_scr, e2_scr, act_scr, p_scr, acc_scr):
    ei = pl.program_id(1)
    tt = h2_ref.shape[0]
    pack = 16

    @pl.when(ei == 0)
    def _():
        acc_scr[...] = jnp.zeros(acc_scr.shape, F32)
        rank_scr[:, :, :tt] = rank_ref[...]
        e2_scr[:, :, :tt] = e2_ref[...]

    act_scr[:, :tt] = lax.dot_general(u_ref[...], h2_ref[...], (((1,), (1,)), ((), ())),
                                      preferred_element_type=F32)
    for ii in range(PEER_ROWS):
        for lc in range(tt // LANES):
            ls = slice(lc * LANES, (lc + 1) * LANES)
            coef = [None] * (N_KEYS // pack)
            for h in range(PEER_HEADS):
                quota = jnp.broadcast_to(quota_ref[h, ii:ii + 1, ls], (pack, LANES)).astype(BF16)
                e1row = jnp.broadcast_to(e1row_ref[h, ii:ii + 1, ls], (pack, LANES)).astype(BF16)
                for k in range(N_KEYS // pack):
                    ks = slice(k * pack, (k + 1) * pack)
                    gate = jnp.where(rank_scr[h, ks, ls] < quota, e2_scr[h, ks, ls] * e1row, 0.0)
                    coef[k] = gate if coef[k] is None else coef[k] + gate
            for k in range(N_KEYS // pack):
                rs = slice(ii * N_KEYS + k * pack, ii * N_KEYS + (k + 1) * pack)
                p_scr[rs, ls] = coef[k] * jax.nn.gelu(act_scr[rs, ls].astype(BF16))
    acc_scr[...] += jnp.dot(vt_ref[...], p_scr[:, :tt], preferred_element_type=F32)

    @pl.when(ei == pl.num_programs(1) - 1)
    def _():
        o_ref[...] = acc_scr[...].T


def _peer_dense(h2, u_bf, vt_bf, quota, e1, rank2, e2):
    t, d = h2.shape
    n_exp = u_bf.shape[0]
    tt = min(PEER_TT, t)
    te = PEER_ROWS * N_KEYS
    row_spec = pl.BlockSpec((PEER_HEADS, PEER_ROWS, tt), lambda i, e: (0, e, i))
    tok_spec = pl.BlockSpec((PEER_HEADS, N_KEYS, tt), lambda i, e: (0, 0, i))
    return pl.pallas_call(
        _peer_dense_kernel,
        out_shape=jax.ShapeDtypeStruct((t, d), F32),
        grid=(t // tt, n_exp // te),
        in_specs=[
            pl.BlockSpec((tt, d), lambda i, e: (i, 0)),
            pl.BlockSpec((te, d), lambda i, e: (e, 0)),
            pl.BlockSpec((d, te), lambda i, e: (0, e)),
            row_spec,
            row_spec,
            tok_spec,
            tok_spec,
        ],
        out_specs=pl.BlockSpec((tt, d), lambda i, e: (i, 0)),
        scratch_shapes=[
            pltpu.VMEM((PEER_HEADS, N_KEYS, tt + LANES), BF16),
            pltpu.VMEM((PEER_HEADS, N_KEYS, tt + LANES), BF16),
            pltpu.VMEM((te, tt + LANES), F32),
            pltpu.VMEM((te, tt + LANES), BF16),
            pltpu.VMEM((d, tt), F32),
        ],
        compiler_params=_cparams(("parallel", "arbitrary")),
        name="peer_dense",
    )(h2, u_bf, vt_bf, quota, e1, rank2, e2)


def _residual_norm_kernel(x1_ref, p_ref, g_ref, o_ref, *, final):
    y = x1_ref[...] + p_ref[...]
    if final:
        ms = jnp.mean(y * y, axis=-1, keepdims=True)
        y = (y * lax.rsqrt(ms + EPS)) * g_ref[...]
    o_ref[...] = y


def _residual_norm(x1, peer_out, g, final):
    t, d = x1.shape
    tt = min(NORM_TT, t)
    return pl.pallas_call(
        functools.partial(_residual_norm_kernel, final=final),
        out_shape=jax.ShapeDtypeStruct((t, d), F32),
        grid=(t // tt,),
        in_specs=[
            pl.BlockSpec((tt, d), lambda i: (i, 0)),
            pl.BlockSpec((tt, d), lambda i: (i, 0)),
            pl.BlockSpec((1, d), lambda i: (0, 0)),
        ],
        out_specs=pl.BlockSpec((tt, d), lambda i: (i, 0)),
        compiler_params=_cparams(("parallel",)),
        name="residual_norm",
    )(x1, peer_out, g.reshape(1, d))


def _layer(x2, b, s, ln_mix_g, w_in, gmlp_norm_g, w_spatial, b_spatial, w_branch_attn, w_branch_gmlp,
           w_out, ln_ffn_g, peer_w_q, peer_sub_keys, peer_u, peer_v, out_g, final):
    t, d = x2.shape
    n_qi = IDX_HEADS * IDX_DIM
    o_qi = 3 * ATTN_WIDTH
    o_ki = o_qi + n_qi
    o_wi = o_ki + IDX_DIM
    o_gate = o_wi + IDX_HEADS

    w_attn = w_in[:, :o_ki].astype(BF16)
    w_idx = jnp.pad(w_in[:, o_ki:o_gate], ((0, 0), (0, LANES - IDX_DIM - IDX_HEADS))).astype(BF16)
    w_gate = w_in[:, o_gate:].astype(BF16)

    attn_in = _norm_proj(x2, ln_mix_g, w_attn, BF16, ATTN_TN)
    idx_in = _norm_proj(x2, ln_mix_g, w_idx, F32, LANES)
    gates = _norm_proj(x2, ln_mix_g, w_gate, F32, GATE_TN)

    q = attn_in[:, :ATTN_WIDTH].reshape(b, s, ATTN_WIDTH)
    k = attn_in[:, ATTN_WIDTH:2 * ATTN_WIDTH].reshape(b, s, ATTN_WIDTH)
    v = attn_in[:, 2 * ATTN_WIDTH:o_qi].reshape(b, s, ATTN_WIDTH)
    nq = s // TQ
    qi_r = attn_in[:, o_qi:o_ki].reshape(b, nq, TQ, IDX_HEADS, IDX_DIM)
    qi_r = qi_r.transpose(0, 1, 3, 2, 4).reshape(b, nq, IDX_HEADS * TQ, IDX_DIM)
    kit = idx_in[:, :IDX_DIM].astype(BF16).reshape(b, s, IDX_DIM).transpose(0, 2, 1)
    wi_r = idx_in[:, IDX_DIM:IDX_DIM + IDX_HEADS].reshape(b, nq, TQ, IDX_HEADS)
    wi_r = wi_r.transpose(0, 1, 3, 2).reshape(b, nq, IDX_HEADS * TQ, 1)

    n_sel = min(DSA_TOPK, s // 4)
    bias = _dsa_select(qi_r, wi_r, kit, n_sel)
    ya = _dsa_attend(q, k, v, bias).reshape(t, ATTN_WIDTH)

    mixed = _gmlp_mix(gates, ya, gmlp_norm_g, w_spatial, b_spatial.T, w_branch_attn.astype(BF16),
                      w_branch_gmlp.astype(BF16), d)
    x1, h2 = _out_proj(x2, mixed, w_out.astype(BF16), ln_ffn_g)

    half = peer_sub_keys.shape[-1]
    sk = peer_sub_keys.reshape(2 * PEER_HEADS, N_KEYS, half).astype(BF16)
    st = _peer_scores(h2, peer_w_q.astype(BF16), sk)
    quota, e1, rank2, e2 = _peer_select(st)
    peer_out = _peer_dense(h2, peer_u.astype(BF16), peer_v.astype(BF16).T, quota, e1, rank2, e2)
    return _residual_norm(x1, peer_out, out_g, final)


def kernel(x, ln_mix_g, w_in, gmlp_norm_g, w_spatial, b_spatial, w_branch_attn, w_branch_gmlp, w_out, ln_ffn_g, peer_w_q, peer_sub_keys, peer_u, peer_v, ln_final_g):
    b, s, d = x.shape
    depth = w_in.shape[0]
    x2 = x.reshape(b * s, d)
    for l in range(depth):
        x2 = _layer(x2, b, s, ln_mix_g[l], w_in[l], gmlp_norm_g[l], w_spatial[l], b_spatial[l],
                    w_branch_attn[l], w_branch_gmlp[l], w_out[l], ln_ffn_g[l], peer_w_q[l],
                    peer_sub_keys[l], peer_u[l], peer_v[l], ln_final_g, l + 1 == depth)
    return x2.reshape(b, s, d)
```

```python
import functools
import math

import jax
import jax.numpy as jnp
from jax import lax
from jax.experimental import pallas as pl
from jax.experimental.pallas import tpu as pltpu

F32 = jnp.float32
BF16 = jnp.bfloat16
I32 = jnp.int32

ATTN_HEADS = 8
HEAD_DIM = 128
ATTN_WIDTH = ATTN_HEADS * HEAD_DIM
IDX_HEADS = 8
IDX_DIM = 64
DSA_TOPK = 256
GMLP_GROUPS = 8
CHUNK = 128
GMLP_WIDTH = GMLP_GROUPS * CHUNK
N_KEYS = 128
PEER_HEADS = 8
PEER_TOPK = 16
EPS = 1e-6

LANES = 128
MXU_COLS = 256
VMEM_LIMIT_BYTES = 56 * 1024 * 1024

INT_MIN = -(2 ** 31)
NEG_BIG = -1e30

TQ = 128
TK = 512
SCAN_KEYS = 512
ATQ = 512
ATK = 512
PROJ_TM = 1024
ATTN_TN = 896
GATE_TN = 1024
NORM_TT = 512
MIX_TT = 256
RES_TT = 512
PEER_TT = 512
PEER_ROWS = 8
SEL_TL = 512


def _cparams(sem):
    return pltpu.CompilerParams(dimension_semantics=sem, vmem_limit_bytes=VMEM_LIMIT_BYTES)


def _norm_proj_kernel(x_ref, g_ref, w_ref, o_ref, h_scr):
    @pl.when(pl.program_id(1) == 0)
    def _():
        x = x_ref[...]
        ms = jnp.mean(x * x, axis=-1, keepdims=True)
        h_scr[...] = ((x * lax.rsqrt(ms + EPS)) * g_ref[...]).astype(BF16)

    o_ref[...] = jnp.dot(h_scr[...], w_ref[...], preferred_element_type=F32).astype(o_ref.dtype)


def _norm_proj(x2, g, w_bf, out_dtype, tn):
    t, d = x2.shape
    n = w_bf.shape[1]
    tm = min(PROJ_TM, t)
    return pl.pallas_call(
        _norm_proj_kernel,
        out_shape=jax.ShapeDtypeStruct((t, n), out_dtype),
        grid=(t // tm, n // tn),
        in_specs=[
            pl.BlockSpec((tm, d), lambda i, j: (i, 0)),
            pl.BlockSpec((1, d), lambda i, j: (0, 0)),
            pl.BlockSpec((d, tn), lambda i, j: (0, j)),
        ],
        out_specs=pl.BlockSpec((tm, tn), lambda i, j: (i, j)),
        scratch_shapes=[pltpu.VMEM((tm, d), BF16)],
        compiler_params=_cparams(("parallel", "arbitrary")),
        name="norm_proj",
    )(x2, g.reshape(1, d), w_bf)


def _sort_key(x):
    bits = pltpu.bitcast(x, I32)
    return bits ^ ((bits >> 31) & 0x7FFFFFFF)


def _dsa_select_kernel(qi_ref, wi_ref, kit_ref, bias_ref, keys_scr, wb_scr, *, n_sel, seq):
    i = pl.program_id(1)
    n_kt = (i * TQ + TQ + TK - 1) // TK
    wb_scr[...] = jnp.broadcast_to((wi_ref[...] * (IDX_HEADS ** -0.5)) * (IDX_DIM ** -0.5), wb_scr.shape)
    q_pos = i * TQ + lax.broadcasted_iota(I32, (TQ, LANES), 0)
    lane = lax.broadcasted_iota(I32, (TQ, LANES), 1)

    def chunk_at(off):
        return pl.ds(pl.multiple_of(off, LANES), LANES)

    def score_body(kt, carry):
        m1, m2 = carry
        off = pl.multiple_of(kt * TK, TK)
        accs = [None] * (TK // LANES)
        for half in range(TK // MXU_COLS):
            kit_tile = kit_ref[:, pl.ds(pl.multiple_of(off + half * MXU_COLS, MXU_COLS), MXU_COLS)]
            for h in range(IDX_HEADS):
                hs = slice(h * TQ, (h + 1) * TQ)
                dots = jnp.maximum(jnp.dot(qi_ref[hs, :], kit_tile, preferred_element_type=F32), 0.0)
                for cc in range(MXU_COLS // LANES):
                    c = half * (MXU_COLS // LANES) + cc
                    term = dots[:, cc * LANES:(cc + 1) * LANES] * wb_scr[hs, :]
                    accs[c] = term if accs[c] is None else accs[c] + term
        for c in range(TK // LANES):
            sc = accs[c] + 0.0
            causal = (off + c * LANES + lane) <= q_pos
            scm = jnp.where(causal, sc, -jnp.inf)
            m2 = jnp.maximum(m2, jnp.minimum(m1, scm))
            m1 = jnp.maximum(m1, scm)
            keys_scr[:, chunk_at(off + c * LANES)] = jnp.where(causal, _sort_key(sc), INT_MIN)
        return m1, m2

    neg_inf = jnp.full((TQ, LANES), -jnp.inf, F32)
    m1, m2 = lax.fori_loop(0, n_kt, score_body, (neg_inf, neg_inf))

    scan = SCAN_KEYS if seq % SCAN_KEYS == 0 else TK
    n_st = (n_kt * TK + scan - 1) // scan
    kt_end = n_st * (scan // TK)

    def pad_body(kt, carry):
        keys_scr[:, pl.ds(pl.multiple_of(kt * TK, TK), TK)] = jnp.full((TQ, TK), INT_MIN, I32)
        return carry

    lax.fori_loop(n_kt, kt_end, pad_body, 0)

    def count(preds):
        def body(st, accs):
            off = pl.multiple_of(st * scan, scan)
            for c in range(scan // LANES):
                blk = keys_scr[:, chunk_at(off + c * LANES)]
                accs = tuple(a + p(blk, off + c * LANES) for a, p in zip(accs, preds))
            return accs
        accs = lax.fori_loop(0, n_st, body, tuple(jnp.zeros((TQ, LANES), I32) for _ in preds))
        return [jnp.sum(a.astype(F32), axis=1, keepdims=True).astype(I32) for a in accs]

    low_f = jnp.min(m2, axis=1, keepdims=True)
    low_key = _sort_key(jnp.broadcast_to(low_f, (TQ, LANES)))[:, :1]
    lo0 = jnp.where(low_f == -jnp.inf, INT_MIN, low_key)
    top_f = jnp.max(m2 if n_sel > LANES else m1, axis=1, keepdims=True)
    hi0 = _sort_key(jnp.broadcast_to(top_f, (TQ, LANES)))[:, :1] + 1

    def bisect_body(state):
        lo, hi, _ = state
        mid = (lo >> 1) + (hi >> 1) + (lo & hi & 1)
        mb = jnp.broadcast_to(mid, (TQ, LANES))
        c, = count([lambda blk, off: (blk >= mb).astype(I32)])
        ge = c >= n_sel
        lo_n = jnp.where(ge, mid, lo)
        hi_n = jnp.where(c == n_sel, mid + 1, jnp.where(ge, hi, mid))
        return lo_n, hi_n, jnp.max(jnp.where(hi_n != lo_n + 1, 1.0, 0.0)).astype(F32)

    t, _, _ = lax.while_loop(lambda st: st[2] > 0.0, bisect_body, (lo0, hi0, jnp.float32(1.0)))
    tb = jnp.broadcast_to(t, (TQ, LANES))

    n_gt, n_eq = count([lambda blk, off: (blk > tb).astype(I32), lambda blk, off: (blk == tb).astype(I32)])
    need = n_sel - n_gt
    ambiguous = jnp.max(jnp.where((n_eq > need) & (t > INT_MIN), 1.0, 0.0))

    def write_bias(keep_fn):
        def body(st, carry):
            off = pl.multiple_of(st * scan, scan)
            for c in range(scan // LANES):
                blk = keys_scr[:, chunk_at(off + c * LANES)]
                bias_ref[:, chunk_at(off + c * LANES)] = jnp.where(
                    keep_fn(blk, off + c * LANES), 0.0, NEG_BIG).astype(BF16)
            return carry
        lax.fori_loop(0, n_st, body, 0)

    @pl.when(ambiguous <= 0)
    def _():
        thr = jnp.broadcast_to(jnp.maximum(t, INT_MIN + 1), (TQ, LANES))
        write_bias(lambda blk, off: blk >= thr)

    @pl.when(ambiguous > 0)
    def _():
        n_bits = max(1, (seq - 1).bit_length())

        def jbit_body(k, x):
            cand = x + lax.shift_left(jnp.int32(1), jnp.int32(n_bits - 1) - k)
            xb = jnp.broadcast_to(cand, (TQ, LANES))
            below, = count([lambda blk, off: jnp.where(blk == tb, ((off + lane) < xb).astype(I32), 0)])
            return jnp.where(below < need, cand, x)

        x = lax.fori_loop(0, n_bits, jbit_body, jnp.zeros((TQ, 1), I32))
        cut = jnp.where(t == INT_MIN, -1, jnp.where(n_eq > need, x, seq))
        jb = jnp.broadcast_to(cut, (TQ, LANES))
        write_bias(lambda blk, off: jnp.where(blk > tb, 1, jnp.where(blk == tb, ((off + lane) <= jb).astype(I32), 0)) > 0)

    def fill_body(kt, carry):
        bias_ref[:, pl.ds(pl.multiple_of(kt * TK, TK), TK)] = jnp.full((TQ, TK), NEG_BIG, BF16)
        return carry

    lax.fori_loop(kt_end, seq // TK, fill_body, 0)


def _dsa_select(qi_r, wi_r, kit, n_sel):
    b, nq = qi_r.shape[0], qi_r.shape[1]
    s = kit.shape[2]
    assert n_sel <= 2 * LANES and s % TK == 0
    return pl.pallas_call(
        functools.partial(_dsa_select_kernel, n_sel=n_sel, seq=s),
        out_shape=jax.ShapeDtypeStruct((b, s, s), BF16),
        grid=(b, nq),
        in_specs=[
            pl.BlockSpec((None, None, IDX_HEADS * TQ, IDX_DIM), lambda bb, i: (bb, i, 0, 0)),
            pl.BlockSpec((None, None, IDX_HEADS * TQ, 1), lambda bb, i: (bb, i, 0, 0)),
            pl.BlockSpec((None, IDX_DIM, s), lambda bb, i: (bb, 0, 0)),
        ],
        out_specs=pl.BlockSpec((None, TQ, s), lambda bb, i: (bb, i, 0)),
        scratch_shapes=[
            pltpu.VMEM((TQ, s + LANES), I32),
            pltpu.VMEM((IDX_HEADS * TQ, LANES), F32),
        ],
        compiler_params=_cparams(("parallel", "arbitrary")),
        name="dsa_select",
    )(qi_r, wi_r, kit)


def _dsa_attend_kernel(qtab_ref, ktab_ref, q_ref, k_ref, v_ref, bias_ref, o_ref, m_scr, acc_scr):
    step = pl.program_id(1)
    i = qtab_ref[step]
    kt = ktab_ref[step]
    last_kt = (i * ATQ + ATQ - 1) // ATK

    @pl.when(kt == 0)
    def _():
        m_scr[...] = jnp.full(m_scr.shape, NEG_BIG, F32)
        acc_scr[...] = jnp.zeros(acc_scr.shape, F32)

    bias = bias_ref[...].astype(F32)
    log2e_scale = (HEAD_DIM ** -0.5) * math.log2(math.e)
    ones = jnp.ones((ATK, HEAD_DIM), BF16)
    for h in range(ATTN_HEADS):
        hs = slice(h * HEAD_DIM, (h + 1) * HEAD_DIM)
        logits = lax.dot_general(q_ref[:, hs], k_ref[:, hs], (((1,), (1,)), ((), ())),
                                 preferred_element_type=F32)
        s = logits * log2e_scale + bias
        m_old = m_scr[h]
        m_new = jnp.maximum(m_old, jnp.max(s, axis=1, keepdims=True))
        alpha = jnp.exp2(m_old - m_new)
        p = jnp.concatenate([jnp.exp2(s[:, c * LANES:(c + 1) * LANES] - m_new) for c in range(ATK // LANES)],
                            axis=1).astype(BF16)
        v_ext = jnp.concatenate([v_ref[:, hs], ones], axis=1)
        pv = jnp.dot(p, v_ext, preferred_element_type=F32)
        acc_scr[h] = jnp.concatenate([alpha, alpha], axis=1) * acc_scr[h] + pv
        m_scr[h] = m_new

    @pl.when(kt == last_kt)
    def _():
        for h in range(ATTN_HEADS):
            acc = acc_scr[h]
            o_ref[:, h * HEAD_DIM:(h + 1) * HEAD_DIM] = (acc[:, :HEAD_DIM] / acc[:, HEAD_DIM:]).astype(o_ref.dtype)


def _dsa_attend(q, k, v, bias):
    b, s, _ = q.shape
    atq = min(ATQ, s)
    assert atq == ATQ and s % ATQ == 0
    qtab, ktab = [], []
    for i in range(s // ATQ):
        for kt in range((i * ATQ + ATQ - 1) // ATK + 1):
            qtab.append(i)
            ktab.append(kt)
    grid_spec = pltpu.PrefetchScalarGridSpec(
        num_scalar_prefetch=2,
        grid=(b, len(qtab)),
        in_specs=[
            pl.BlockSpec((None, ATQ, ATTN_WIDTH), lambda bb, st, qt, kk: (bb, qt[st], 0)),
            pl.BlockSpec((None, ATK, ATTN_WIDTH), lambda bb, st, qt, kk: (bb, kk[st], 0)),
            pl.BlockSpec((None, ATK, ATTN_WIDTH), lambda bb, st, qt, kk: (bb, kk[st], 0)),
            pl.BlockSpec((None, ATQ, ATK), lambda bb, st, qt, kk: (bb, qt[st], kk[st])),
        ],
        out_specs=pl.BlockSpec((None, ATQ, ATTN_WIDTH), lambda bb, st, qt, kk: (bb, qt[st], 0)),
        scratch_shapes=[
            pltpu.VMEM((ATTN_HEADS, ATQ, LANES), F32),
            pltpu.VMEM((ATTN_HEADS, ATQ, 2 * HEAD_DIM), F32),
        ],
    )
    return pl.pallas_call(
        _dsa_attend_kernel,
        out_shape=jax.ShapeDtypeStruct((b, s, ATTN_WIDTH), BF16),
        grid_spec=grid_spec,
        compiler_params=_cparams(("parallel", "arbitrary")),
        name="dsa_attend",
    )(jnp.asarray(qtab, I32), jnp.asarray(ktab, I32), q, k, v, bias)


def _gmlp_mix_kernel(gu_ref, gv_ref, ga_ref, gb_ref, ya_ref, ng_ref, ws_ref, bst_ref, wa_ref, wb_ref,
                     o_ref, yb_scr):
    tt = gu_ref.shape[0]
    row = lax.broadcasted_iota(I32, (CHUNK, CHUNK), 0)
    col = lax.broadcasted_iota(I32, (CHUNK, CHUNK), 1)
    tril = col <= row
    for c in range(tt // CHUNK):
        rs = slice(c * CHUNK, (c + 1) * CHUNK)
        u = jax.nn.gelu(gu_ref[rs, :])
        v = jax.nn.gelu(gv_ref[rs, :])
        v = (v * lax.rsqrt(jnp.mean(v * v, axis=-1, keepdims=True) + EPS)) * ng_ref[...]
        vb = v.astype(BF16)
        for g in range(GMLP_GROUPS):
            gs = slice(g * CHUNK, (g + 1) * CHUNK)
            wm = jnp.where(tril, ws_ref[g], 0.0).astype(BF16)
            z = jnp.dot(wm, vb[:, gs], preferred_element_type=F32) + bst_ref[:, g:g + 1]
            yb_scr[rs, gs] = (u[:, gs] * z).astype(BF16)
    ma = jnp.dot(ya_ref[...], wa_ref[...], preferred_element_type=F32)
    mb = jnp.dot(yb_scr[...], wb_ref[...], preferred_element_type=F32)
    o_ref[...] = (jax.nn.sigmoid(ga_ref[...]) * ma + jax.nn.sigmoid(gb_ref[...]) * mb).astype(o_ref.dtype)


def _gmlp_mix(gates, ya, norm_g, w_s, b_s_t, wa_bf, wb_bf, d_model):
    t = ya.shape[0]
    tt = min(MIX_TT, t)
    assert d_model == 2 * GMLP_WIDTH
    return pl.pallas_call(
        _gmlp_mix_kernel,
        out_shape=jax.ShapeDtypeStruct((t, d_model), BF16),
        grid=(t // tt,),
        in_specs=[
            pl.BlockSpec((tt, GMLP_WIDTH), lambda i: (i, 0)),
            pl.BlockSpec((tt, GMLP_WIDTH), lambda i: (i, 1)),
            pl.BlockSpec((tt, d_model), lambda i: (i, 1)),
            pl.BlockSpec((tt, d_model), lambda i: (i, 2)),
            pl.BlockSpec((tt, ATTN_WIDTH), lambda i: (i, 0)),
            pl.BlockSpec((1, GMLP_WIDTH), lambda i: (0, 0)),
            pl.BlockSpec((GMLP_GROUPS, CHUNK, CHUNK), lambda i: (0, 0, 0)),
            pl.BlockSpec((CHUNK, GMLP_GROUPS), lambda i: (0, 0)),
            pl.BlockSpec((ATTN_WIDTH, d_model), lambda i: (0, 0)),
            pl.BlockSpec((GMLP_WIDTH, d_model), lambda i: (0, 0)),
        ],
        out_specs=pl.BlockSpec((tt, d_model), lambda i: (i, 0)),
        scratch_shapes=[pltpu.VMEM((tt, GMLP_WIDTH), BF16)],
        compiler_params=_cparams(("parallel",)),
        name="gmlp_mix",
    )(gates, gates, gates, gates, ya, norm_g.reshape(1, GMLP_WIDTH), w_s, b_s_t, wa_bf, wb_bf)


def _out_proj_kernel(x_ref, m_ref, wo_ref, g_ref, x1_ref, h2_ref):
    x1 = x_ref[...] + jnp.dot(m_ref[...], wo_ref[...], preferred_element_type=F32)
    x1_ref[...] = x1
    ms = jnp.mean(x1 * x1, axis=-1, keepdims=True)
    h2_ref[...] = ((x1 * lax.rsqrt(ms + EPS)) * g_ref[...]).astype(BF16)


def _out_proj(x2, mixed, wo_bf, g):
    t, d = x2.shape
    tt = min(RES_TT, t)
    return pl.pallas_call(
        _out_proj_kernel,
        out_shape=(jax.ShapeDtypeStruct((t, d), F32), jax.ShapeDtypeStruct((t, d), BF16)),
        grid=(t // tt,),
        in_specs=[
            pl.BlockSpec((tt, d), lambda i: (i, 0)),
            pl.BlockSpec((tt, d), lambda i: (i, 0)),
            pl.BlockSpec((d, d), lambda i: (0, 0)),
            pl.BlockSpec((1, d), lambda i: (0, 0)),
        ],
        out_specs=(pl.BlockSpec((tt, d), lambda i: (i, 0)), pl.BlockSpec((tt, d), lambda i: (i, 0))),
        compiler_params=_cparams(("parallel",)),
        name="out_proj",
    )(x2, mixed, wo_bf, g.reshape(1, d))


def _peer_scores_kernel(h2_ref, wq_ref, sk_ref, st_ref):
    qp = jnp.dot(h2_ref[...], wq_ref[...], preferred_element_type=F32).astype(BF16)
    half = sk_ref.shape[2]
    for hp in range(2 * PEER_HEADS):
        st_ref[hp] = lax.dot_general(sk_ref[hp], qp[:, hp * half:(hp + 1) * half], (((1,), (1,)), ((), ())),
                                     preferred_element_type=F32)


def _peer_scores(h2, wq_bf, sk_bf):
    t, d = h2.shape
    tt = min(RES_TT, t)
    nq = wq_bf.shape[1]
    half = sk_bf.shape[2]
    return pl.pallas_call(
        _peer_scores_kernel,
        out_shape=jax.ShapeDtypeStruct((2 * PEER_HEADS, N_KEYS, t), F32),
        grid=(t // tt,),
        in_specs=[
            pl.BlockSpec((tt, d), lambda i: (i, 0)),
            pl.BlockSpec((d, nq), lambda i: (0, 0)),
            pl.BlockSpec((2 * PEER_HEADS, N_KEYS, half), lambda i: (0, 0, 0)),
        ],
        out_specs=pl.BlockSpec((2 * PEER_HEADS, N_KEYS, tt), lambda i: (0, 0, i)),
        compiler_params=_cparams(("parallel",)),
        name="peer_scores",
    )(h2, wq_bf, sk_bf)


def _top_values(cur, k):
    n = cur.shape[0]
    idx = lax.broadcasted_iota(I32, cur.shape, 0).astype(F32)
    vals = []
    for _ in range(k):
        mx = jnp.max(cur, axis=0, keepdims=True)
        first = jnp.min(jnp.where(cur == mx, idx, float(n)), axis=0, keepdims=True)
        vals.append(mx)
        cur = jnp.where(idx == first, -jnp.inf, cur)
    return vals


def _staircase_sums(t1, t2):
    a1 = jnp.concatenate(t1, axis=0)
    a2 = jnp.concatenate(t2, axis=0)
    r16 = lax.broadcasted_iota(I32, (PEER_TOPK, LANES), 0)
    r8 = r16[:8]
    ninf = -jnp.inf
    return jnp.concatenate([
        t1[0] + a2,
        t1[1] + a2[:8],
        jnp.where(r16 >= 2, a1 + t2[0], ninf),
        jnp.where(r8 >= 2, a1[:8] + t2[1], ninf),
        jnp.where((r8 >= 2) & (r8 <= 4), t1[2] + a2[:8], ninf),
        jnp.where((r8 >= 2) & (r8 <= 3), t1[3] + a2[:8], ninf),
        jnp.where(r8 == 2, t1[4] + a2[:8], ninf),
    ], axis=0)


def _top_ranked(cur, k):
    n = cur.shape[0]
    idx = lax.broadcasted_iota(I32, cur.shape, 0).astype(F32)
    rank = jnp.full(cur.shape, float(k), F32)
    vals, firsts = [], []
    for r in range(k):
        mx = jnp.max(cur, axis=0, keepdims=True)
        first = jnp.min(jnp.where(cur == mx, idx, float(n)), axis=0, keepdims=True)
        hit = idx == first
        rank = jnp.where(hit, float(r), rank)
        cur = jnp.where(hit, -jnp.inf, cur)
        vals.append(mx)
        firsts.append(first)
    return vals, firsts, rank


def _peer_select_kernel(st_ref, quota_ref, e1_ref, rank_ref, e2_ref):
    tl = st_ref.shape[2]
    idx = lax.broadcasted_iota(I32, (N_KEYS, LANES), 0).astype(F32)

    def lane_group(g, carry):
        ls = pl.ds(pl.multiple_of(g * LANES, LANES), LANES)
        for h in range(PEER_HEADS):
            s1 = st_ref[2 * h, :, ls]
            s2 = st_ref[2 * h + 1, :, ls]
            t1, first1, _ = _top_ranked(s1, PEER_TOPK)
            t2, _, rank2 = _top_ranked(s2, PEER_TOPK)
            best = _top_values(_staircase_sums(t1, t2), PEER_TOPK)
            m = best[0]
            theta = best[PEER_TOPK - 1]
            z = functools.reduce(lambda a, b: a + b, [jnp.exp(bs - m) for bs in best])
            a2 = jnp.concatenate(t2, axis=0)
            quota = jnp.zeros((N_KEYS, LANES), F32)
            for a in range(PEER_TOPK):
                n_ok = jnp.sum(jnp.where(t1[a] + a2 >= theta, 1.0, 0.0), axis=0, keepdims=True)
                quota = jnp.where(idx == first1[a], n_ok, quota)
            quota_ref[h, :, ls] = quota
            e1_ref[h, :, ls] = jnp.exp(s1 - t1[0]) / z
            rank_ref[h, :, ls] = rank2.astype(BF16)
            e2_ref[h, :, ls] = jnp.exp(s2 - t2[0]).astype(BF16)
        return carry

    lax.fori_loop(0, tl // LANES, lane_group, 0)


def _peer_select(st):
    t = st.shape[2]
    tl = min(SEL_TL, t)
    f_shape = jax.ShapeDtypeStruct((PEER_HEADS, N_KEYS, t), F32)
    b_shape = jax.ShapeDtypeStruct((PEER_HEADS, N_KEYS, t), BF16)
    spec = pl.BlockSpec((PEER_HEADS, N_KEYS, tl), lambda i: (0, 0, i))
    return pl.pallas_call(
        _peer_select_kernel,
        out_shape=(f_shape, f_shape, b_shape, b_shape),
        grid=(t // tl,),
        in_specs=[pl.BlockSpec((2 * PEER_HEADS, N_KEYS, tl), lambda i: (0, 0, i))],
        out_specs=(spec, spec, spec, spec),
        compiler_params=_cparams(("parallel",)),
        name="peer_select",
    )(st)


def _peer_dense_kernel(h2_ref, u_ref, vt_ref, quota_ref, e1row_ref, rank_ref, e2_ref, o_ref,
                       rank_scr, e2_scr, act_scr, p_scr, acc_scr):
    ei = pl.program_id(1)
    tt = h2_ref.shape[0]
    pack = 16

    @pl.when(ei == 0)
    def _():
        acc_scr[...] = jnp.zeros(acc_scr.shape, F32)
        rank_scr[:, :, :tt] = rank_ref[...]
        e2_scr[:, :, :tt] = e2_ref[...]

    act_scr[:, :tt] = lax.dot_general(u_ref[...], h2_ref[...], (((1,), (1,)), ((), ())),
                                      preferred_element_type=F32)
    for ii in range(PEER_ROWS):
        for lc in range(tt // LANES):
            ls = slice(lc * LANES, (lc + 1) * LANES)
            coef = [None] * (N_KEYS // pack)
            for h in range(PEER_HEADS):
                quota = jnp.broadcast_to(quota_ref[h, ii:ii + 1, ls], (pack, LANES)).astype(BF16)
                e1row = jnp.broadcast_to(e1row_ref[h, ii:ii + 1, ls], (pack, LANES)).astype(BF16)
                for k in range(N_KEYS // pack):
                    ks = slice(k * pack, (k + 1) * pack)
                    gate = jnp.where(rank_scr[h, ks, ls] < quota, e2_scr[h, ks, ls] * e1row, 0.0)
                    coef[k] = gate if coef[k] is None else coef[k] + gate
            for k in range(N_KEYS // pack):
                rs = slice(ii * N_KEYS + k * pack, ii * N_KEYS + (k + 1) * pack)
                p_scr[rs, ls] = coef[k] * jax.nn.gelu(act_scr[rs, ls].astype(BF16))
    acc_scr[...] += jnp.dot(vt_ref[...], p_scr[:, :tt], preferred_element_type=F32)

    @pl.when(ei == pl.num_programs(1) - 1)
    def _():
        o_ref[...] = acc_scr[...].T


def _peer_dense(h2, u_bf, vt_bf, quota, e1, rank2, e2):
    t, d = h2.shape
    n_exp = u_bf.shape[0]
    tt = min(PEER_TT, t)
    te = PEER_ROWS * N_KEYS
    row_spec = pl.BlockSpec((PEER_HEADS, PEER_ROWS, tt), lambda i, e: (0, e, i))
    tok_spec = pl.BlockSpec((PEER_HEADS, N_KEYS, tt), lambda i, e: (0, 0, i))
    return pl.pallas_call(
        _peer_dense_kernel,
        out_shape=jax.ShapeDtypeStruct((t, d), F32),
        grid=(t // tt, n_exp // te),
        in_specs=[
            pl.BlockSpec((tt, d), lambda i, e: (i, 0)),
            pl.BlockSpec((te, d), lambda i, e: (e, 0)),
            pl.BlockSpec((d, te), lambda i, e: (0, e)),
            row_spec,
            row_spec,
            tok_spec,
            tok_spec,
        ],
        out_specs=pl.BlockSpec((tt, d), lambda i, e: (i, 0)),
        scratch_shapes=[
            pltpu.VMEM((PEER_HEADS, N_KEYS, tt + LANES), BF16),
            pltpu.VMEM((PEER_HEADS, N_KEYS, tt + LANES), BF16),
            pltpu.VMEM((te, tt + LANES), F32),
            pltpu.VMEM((te, tt + LANES), BF16),
            pltpu.VMEM((d, tt), F32),
        ],
        compiler_params=_cparams(("parallel", "arbitrary")),
        name="peer_dense",
    )(h2, u_bf, vt_bf, quota, e1, rank2, e2)


def _residual_norm_kernel(x1_ref, p_ref, g_ref, o_ref, *, final):
    y = x1_ref[...] + p_ref[...]
    if final:
        ms = jnp.mean(y * y, axis=-1, keepdims=True)
        y = (y * lax.rsqrt(ms + EPS)) * g_ref[...]
    o_ref[...] = y


def _residual_norm(x1, peer_out, g, final):
    t, d = x1.shape
    tt = min(NORM_TT, t)
    return pl.pallas_call(
        functools.partial(_residual_norm_kernel, final=final),
        out_shape=jax.ShapeDtypeStruct((t, d), F32),
        grid=(t // tt,),
        in_specs=[
            pl.BlockSpec((tt, d), lambda i: (i, 0)),
            pl.BlockSpec((tt, d), lambda i: (i, 0)),
            pl.BlockSpec((1, d), lambda i: (0, 0)),
        ],
        out_specs=pl.BlockSpec((tt, d), lambda i: (i, 0)),
        compiler_params=_cparams(("parallel",)),
        name="residual_norm",
    )(x1, peer_out, g.reshape(1, d))


def _layer(x2, b, s, ln_mix_g, w_in, gmlp_norm_g, w_spatial, b_spatial, w_branch_attn, w_branch_gmlp,
           w_out, ln_ffn_g, peer_w_q, peer_sub_keys, peer_u, peer_v, out_g, final):
    t, d = x2.shape
    n_qi = IDX_HEADS * IDX_DIM
    o_qi = 3 * ATTN_WIDTH
    o_ki = o_qi + n_qi
    o_wi = o_ki + IDX_DIM
    o_gate = o_wi + IDX_HEADS

    w_attn = w_in[:, :o_ki].astype(BF16)
    w_idx = jnp.pad(w_in[:, o_ki:o_gate], ((0, 0), (0, LANES - IDX_DIM - IDX_HEADS))).astype(BF16)
    w_gate = w_in[:, o_gate:].astype(BF16)

    attn_in = _norm_proj(x2, ln_mix_g, w_attn, BF16, ATTN_TN)
    idx_in = _norm_proj(x2, ln_mix_g, w_idx, F32, LANES)
    gates = _norm_proj(x2, ln_mix_g, w_gate, F32, GATE_TN)

    q = attn_in[:, :ATTN_WIDTH].reshape(b, s, ATTN_WIDTH)
    k = attn_in[:, ATTN_WIDTH:2 * ATTN_WIDTH].reshape(b, s, ATTN_WIDTH)
    v = attn_in[:, 2 * ATTN_WIDTH:o_qi].reshape(b, s, ATTN_WIDTH)
    nq = s // TQ
    qi_r = attn_in[:, o_qi:o_ki].reshape(b, nq, TQ, IDX_HEADS, IDX_DIM)
    qi_r = qi_r.transpose(0, 1, 3, 2, 4).reshape(b, nq, IDX_HEADS * TQ, IDX_DIM)
    kit = idx_in[:, :IDX_DIM].astype(BF16).reshape(b, s, IDX_DIM).transpose(0, 2, 1)
    wi_r = idx_in[:, IDX_DIM:IDX_DIM + IDX_HEADS].reshape(b, nq, TQ, IDX_HEADS)
    wi_r = wi_r.transpose(0, 1, 3, 2).reshape(b, nq, IDX_HEADS * TQ, 1)

    n_sel = min(DSA_TOPK, s // 4)
    bias = _dsa_select(qi_r, wi_r, kit, n_sel)
    ya = _dsa_attend(q, k, v, bias).reshape(t, ATTN_WIDTH)

    mixed = _gmlp_mix(gates, ya, gmlp_norm_g, w_spatial, b_spatial.T, w_branch_attn.astype(BF16),
                      w_branch_gmlp.astype(BF16), d)
    x1, h2 = _out_proj(x2, mixed, w_out.astype(BF16), ln_ffn_g)

    half = peer_sub_keys.shape[-1]
    sk = peer_sub_keys.reshape(2 * PEER_HEADS, N_KEYS, half).astype(BF16)
    st = _peer_scores(h2, peer_w_q.astype(BF16), sk)
    quota, e1, rank2, e2 = _peer_select(st)
    peer_out = _peer_dense(h2, peer_u.astype(BF16), peer_v.astype(BF16).T, quota, e1, rank2, e2)
    return _residual_norm(x1, peer_out, out_g, final)


def kernel(x, ln_mix_g, w_in, gmlp_norm_g, w_spatial, b_spatial, w_branch_attn, w_branch_gmlp, w_out, ln_ffn_g, peer_w_q, peer_sub_keys, peer_u, peer_v, ln_final_g):
    b, s, d = x.shape
    depth = w_in.shape[0]
    x2 = x.reshape(b * s, d)
    for l in range(depth):
        x2 = _layer(x2, b, s, ln_mix_g[l], w_in[l], gmlp_norm_g[l], w_spatial[l], b_spatial[l],
                    w_branch_attn[l], w_branch_gmlp[l], w_out[l], ln_ffn_g[l], peer_w_q[l],
                    peer_sub_keys[l], peer_u[l], peer_v[l], ln_final_g, l + 1 == depth)
    return x2.reshape(b, s, d)
```

```python
import functools
import math

import jax
import jax.numpy as jnp
from jax import lax
from jax.experimental import pallas as pl
from jax.experimental.pallas import tpu as pltpu

F32 = jnp.float32
BF16 = jnp.bfloat16
I32 = jnp.int32

ATTN_HEADS = 8
HEAD_DIM = 128
ATTN_WIDTH = ATTN_HEADS * HEAD_DIM
IDX_HEADS = 8
IDX_DIM = 64
DSA_TOPK = 256
GMLP_GROUPS = 8
CHUNK = 128
GMLP_WIDTH = GMLP_GROUPS * CHUNK
N_KEYS = 128
PEER_HEADS = 8
PEER_TOPK = 16
EPS = 1e-6

LANES = 128
MXU_COLS = 256
VMEM_LIMIT_BYTES = 56 * 1024 * 1024

INT_MIN = -(2 ** 31)
NEG_BIG = -1e30

TQ = 128
TK = 512
SCAN_KEYS = 1024
ATQ = 512
ATK = 512
PROJ_TM = 1024
ATTN_TN = 896
GATE_TN = 1024
NORM_TT = 512
MIX_TT = 256
RES_TT = 512
PEER_TT = 512
PEER_ROWS = 8
SEL_TL = 512


def _cparams(sem):
    return pltpu.CompilerParams(dimension_semantics=sem, vmem_limit_bytes=VMEM_LIMIT_BYTES)


def _norm_proj_kernel(x_ref, g_ref, w_ref, o_ref, h_scr):
    @pl.when(pl.program_id(1) == 0)
    def _():
        x = x_ref[...]
        ms = jnp.mean(x * x, axis=-1, keepdims=True)
        h_scr[...] = ((x * lax.rsqrt(ms + EPS)) * g_ref[...]).astype(BF16)

    o_ref[...] = jnp.dot(h_scr[...], w_ref[...], preferred_element_type=F32).astype(o_ref.dtype)


def _norm_proj(x2, g, w_bf, out_dtype, tn):
    t, d = x2.shape
    n = w_bf.shape[1]
    tm = min(PROJ_TM, t)
    return pl.pallas_call(
        _norm_proj_kernel,
        out_shape=jax.ShapeDtypeStruct((t, n), out_dtype),
        grid=(t // tm, n // tn),
        in_specs=[
            pl.BlockSpec((tm, d), lambda i, j: (i, 0)),
            pl.BlockSpec((1, d), lambda i, j: (0, 0)),
            pl.BlockSpec((d, tn), lambda i, j: (0, j)),
        ],
        out_specs=pl.BlockSpec((tm, tn), lambda i, j: (i, j)),
        scratch_shapes=[pltpu.VMEM((tm, d), BF16)],
        compiler_params=_cparams(("parallel", "arbitrary")),
        name="norm_proj",
    )(x2, g.reshape(1, d), w_bf)


def _sort_key(x):
    bits = pltpu.bitcast(x, I32)
    return bits ^ ((bits >> 31) & 0x7FFFFFFF)


def _dsa_select_kernel(qi_ref, wi_ref, kit_ref, bias_ref, keys_scr, wb_scr, *, n_sel, seq):
    i = pl.program_id(1)
    n_kt = (i * TQ + TQ + TK - 1) // TK
    wb_scr[...] = jnp.broadcast_to((wi_ref[...] * (IDX_HEADS ** -0.5)) * (IDX_DIM ** -0.5), wb_scr.shape)
    q_pos = i * TQ + lax.broadcasted_iota(I32, (TQ, LANES), 0)
    lane = lax.broadcasted_iota(I32, (TQ, LANES), 1)

    def chunk_at(off):
        return pl.ds(pl.multiple_of(off, LANES), LANES)

    def score_body(kt, carry):
        m1, m2 = carry
        off = pl.multiple_of(kt * TK, TK)
        accs = [None] * (TK // LANES)
        for half in range(TK // MXU_COLS):
            kit_tile = kit_ref[:, pl.ds(pl.multiple_of(off + half * MXU_COLS, MXU_COLS), MXU_COLS)]
            for h in range(IDX_HEADS):
                hs = slice(h * TQ, (h + 1) * TQ)
                dots = jnp.maximum(jnp.dot(qi_ref[hs, :], kit_tile, preferred_element_type=F32), 0.0)
                for cc in range(MXU_COLS // LANES):
                    c = half * (MXU_COLS // LANES) + cc
                    term = dots[:, cc * LANES:(cc + 1) * LANES] * wb_scr[hs, :]
                    accs[c] = term if accs[c] is None else accs[c] + term
        for c in range(TK // LANES):
            sc = accs[c] + 0.0
            causal = (off + c * LANES + lane) <= q_pos
            scm = jnp.where(causal, sc, -jnp.inf)
            m2 = jnp.maximum(m2, jnp.minimum(m1, scm))
            m1 = jnp.maximum(m1, scm)
            keys_scr[:, chunk_at(off + c * LANES)] = jnp.where(causal, _sort_key(sc), INT_MIN)
        return m1, m2

    neg_inf = jnp.full((TQ, LANES), -jnp.inf, F32)
    m1, m2 = lax.fori_loop(0, n_kt, score_body, (neg_inf, neg_inf))

    scan = SCAN_KEYS if seq % SCAN_KEYS == 0 else TK
    n_st = (n_kt * TK + scan - 1) // scan
    kt_end = n_st * (scan // TK)

    def pad_body(kt, carry):
        keys_scr[:, pl.ds(pl.multiple_of(kt * TK, TK), TK)] = jnp.full((TQ, TK), INT_MIN, I32)
        return carry

    lax.fori_loop(n_kt, kt_end, pad_body, 0)

    def count(preds):
        def body(st, accs):
            off = pl.multiple_of(st * scan, scan)
            for c in range(scan // LANES):
                blk = keys_scr[:, chunk_at(off + c * LANES)]
                accs = tuple(a + p(blk, off + c * LANES) for a, p in zip(accs, preds))
            return accs
        accs = lax.fori_loop(0, n_st, body, tuple(jnp.zeros((TQ, LANES), I32) for _ in preds))
        return [jnp.sum(a.astype(F32), axis=1, keepdims=True).astype(I32) for a in accs]

    low_f = jnp.min(m2, axis=1, keepdims=True)
    low_key = _sort_key(jnp.broadcast_to(low_f, (TQ, LANES)))[:, :1]
    lo0 = jnp.where(low_f == -jnp.inf, INT_MIN, low_key)
    top_f = jnp.max(m2 if n_sel > LANES else m1, axis=1, keepdims=True)
    hi0 = _sort_key(jnp.broadcast_to(top_f, (TQ, LANES)))[:, :1] + 1

    def bisect_body(state):
        lo, hi, _ = state
        mid = (lo >> 1) + (hi >> 1) + (lo & hi & 1)
        mb = jnp.broadcast_to(mid, (TQ, LANES))
        c, = count([lambda blk, off: (blk >= mb).astype(I32)])
        ge = c >= n_sel
        lo_n = jnp.where(ge, mid, lo)
        hi_n = jnp.where(c == n_sel, mid + 1, jnp.where(ge, hi, mid))
        return lo_n, hi_n, jnp.max(jnp.where(hi_n != lo_n + 1, 1.0, 0.0)).astype(F32)

    t, _, _ = lax.while_loop(lambda st: st[2] > 0.0, bisect_body, (lo0, hi0, jnp.float32(1.0)))
    tb = jnp.broadcast_to(t, (TQ, LANES))

    n_gt, n_eq = count([lambda blk, off: (blk > tb).astype(I32), lambda blk, off: (blk == tb).astype(I32)])
    need = n_sel - n_gt
    ambiguous = jnp.max(jnp.where((n_eq > need) & (t > INT_MIN), 1.0, 0.0))

    def write_bias(keep_fn):
        def body(st, carry):
            off = pl.multiple_of(st * scan, scan)
            for c in range(scan // LANES):
                blk = keys_scr[:, chunk_at(off + c * LANES)]
                bias_ref[:, chunk_at(off + c * LANES)] = jnp.where(
                    keep_fn(blk, off + c * LANES), 0.0, NEG_BIG).astype(BF16)
            return carry
        lax.fori_loop(0, n_st, body, 0)

    @pl.when(ambiguous <= 0)
    def _():
        thr = jnp.broadcast_to(jnp.maximum(t, INT_MIN + 1), (TQ, LANES))
        write_bias(lambda blk, off: blk >= thr)

    @pl.when(ambiguous > 0)
    def _():
        n_bits = max(1, (seq - 1).bit_length())

        def jbit_body(k, x):
            cand = x + lax.shift_left(jnp.int32(1), jnp.int32(n_bits - 1) - k)
            xb = jnp.broadcast_to(cand, (TQ, LANES))
            below, = count([lambda blk, off: jnp.where(blk == tb, ((off + lane) < xb).astype(I32), 0)])
            return jnp.where(below < need, cand, x)

        x = lax.fori_loop(0, n_bits, jbit_body, jnp.zeros((TQ, 1), I32))
        cut = jnp.where(t == INT_MIN, -1, jnp.where(n_eq > need, x, seq))
        jb = jnp.broadcast_to(cut, (TQ, LANES))
        write_bias(lambda blk, off: jnp.where(blk > tb, 1, jnp.where(blk == tb, ((off + lane) <= jb).astype(I32), 0)) > 0)

    def fill_body(kt, carry):
        bias_ref[:, pl.ds(pl.multiple_of(kt * TK, TK), TK)] = jnp.full((TQ, TK), NEG_BIG, BF16)
        return carry

    lax.fori_loop(kt_end, seq // TK, fill_body, 0)


def _dsa_select(qi_r, wi_r, kit, n_sel):
    b, nq = qi_r.shape[0], qi_r.shape[1]
    s = kit.shape[2]
    assert n_sel <= 2 * LANES and s % TK == 0
    return pl.pallas_call(
        functools.partial(_dsa_select_kernel, n_sel=n_sel, seq=s),
        out_shape=jax.ShapeDtypeStruct((b, s, s), BF16),
        grid=(b, nq),
        in_specs=[
            pl.BlockSpec((None, None, IDX_HEADS * TQ, IDX_DIM), lambda bb, i: (bb, i, 0, 0)),
            pl.BlockSpec((None, None, IDX_HEADS * TQ, 1), lambda bb, i: (bb, i, 0, 0)),
            pl.BlockSpec((None, IDX_DIM, s), lambda bb, i: (bb, 0, 0)),
        ],
        out_specs=pl.BlockSpec((None, TQ, s), lambda bb, i: (bb, i, 0)),
        scratch_shapes=[
            pltpu.VMEM((TQ, s + LANES), I32),
            pltpu.VMEM((IDX_HEADS * TQ, LANES), F32),
        ],
        compiler_params=_cparams(("parallel", "arbitrary")),
        name="dsa_select",
    )(qi_r, wi_r, kit)


def _dsa_attend_kernel(qtab_ref, ktab_ref, q_ref, k_ref, v_ref, bias_ref, o_ref, m_scr, acc_scr):
    step = pl.program_id(1)
    i = qtab_ref[step]
    kt = ktab_ref[step]
    last_kt = (i * ATQ + ATQ - 1) // ATK

    @pl.when(kt == 0)
    def _():
        m_scr[...] = jnp.full(m_scr.shape, NEG_BIG, F32)
        acc_scr[...] = jnp.zeros(acc_scr.shape, F32)

    log2e_scale = (HEAD_DIM ** -0.5) * math.log2(math.e)
    ones = jnp.ones((ATK, HEAD_DIM), BF16)
    for h in range(ATTN_HEADS):
        hs = slice(h * HEAD_DIM, (h + 1) * HEAD_DIM)
        logits = lax.dot_general(q_ref[:, hs], k_ref[:, hs], (((1,), (1,)), ((), ())),
                                 preferred_element_type=F32)
        s = logits * log2e_scale + bias_ref[...].astype(F32)
        m_old = m_scr[h]
        m_new = jnp.maximum(m_old, jnp.max(s, axis=1, keepdims=True))
        alpha = jnp.exp2(m_old - m_new)
        p = jnp.concatenate([jnp.exp2(s[:, c * LANES:(c + 1) * LANES] - m_new) for c in range(ATK // LANES)],
                            axis=1).astype(BF16)
        v_ext = jnp.concatenate([v_ref[:, hs], ones], axis=1)
        pv = jnp.dot(p, v_ext, preferred_element_type=F32)
        acc_scr[h] = jnp.concatenate([alpha, alpha], axis=1) * acc_scr[h] + pv
        m_scr[h] = m_new

    @pl.when(kt == last_kt)
    def _():
        for h in range(ATTN_HEADS):
            acc = acc_scr[h]
            o_ref[:, h * HEAD_DIM:(h + 1) * HEAD_DIM] = (acc[:, :HEAD_DIM] / acc[:, HEAD_DIM:]).astype(o_ref.dtype)


def _dsa_attend(q, k, v, bias):
    b, s, _ = q.shape
    atq = min(ATQ, s)
    assert atq == ATQ and s % ATQ == 0
    qtab, ktab = [], []
    for i in range(s // ATQ):
        for kt in range((i * ATQ + ATQ - 1) // ATK + 1):
            qtab.append(i)
            ktab.append(kt)
    grid_spec = pltpu.PrefetchScalarGridSpec(
        num_scalar_prefetch=2,
        grid=(b, len(qtab)),
        in_specs=[
            pl.BlockSpec((None, ATQ, ATTN_WIDTH), lambda bb, st, qt, kk: (bb, qt[st], 0)),
            pl.BlockSpec((None, ATK, ATTN_WIDTH), lambda bb, st, qt, kk: (bb, kk[st], 0)),
            pl.BlockSpec((None, ATK, ATTN_WIDTH), lambda bb, st, qt, kk: (bb, kk[st], 0)),
            pl.BlockSpec((None, ATQ, ATK), lambda bb, st, qt, kk: (bb, qt[st], kk[st])),
        ],
        out_specs=pl.BlockSpec((None, ATQ, ATTN_WIDTH), lambda bb, st, qt, kk: (bb, qt[st], 0)),
        scratch_shapes=[
            pltpu.VMEM((ATTN_HEADS, ATQ, LANES), F32),
            pltpu.VMEM((ATTN_HEADS, ATQ, 2 * HEAD_DIM), F32),
        ],
    )
    return pl.pallas_call(
        _dsa_attend_kernel,
        out_shape=jax.ShapeDtypeStruct((b, s, ATTN_WIDTH), BF16),
        grid_spec=grid_spec,
        compiler_params=_cparams(("parallel", "arbitrary")),
        name="dsa_attend",
    )(jnp.asarray(qtab, I32), jnp.asarray(ktab, I32), q, k, v, bias)


def _gmlp_mix_kernel(gu_ref, gv_ref, ga_ref, gb_ref, ya_ref, ng_ref, ws_ref, bst_ref, wa_ref, wb_ref,
                     o_ref, yb_scr):
    tt = gu_ref.shape[0]
    row = lax.broadcasted_iota(I32, (CHUNK, CHUNK), 0)
    col = lax.broadcasted_iota(I32, (CHUNK, CHUNK), 1)
    tril = col <= row
    for c in range(tt // CHUNK):
        rs = slice(c * CHUNK, (c + 1) * CHUNK)
        u = jax.nn.gelu(gu_ref[rs, :])
        v = jax.nn.gelu(gv_ref[rs, :])
        v = (v * lax.rsqrt(jnp.mean(v * v, axis=-1, keepdims=True) + EPS)) * ng_ref[...]
        vb = v.astype(BF16)
        for g in range(GMLP_GROUPS):
            gs = slice(g * CHUNK, (g + 1) * CHUNK)
            wm = jnp.where(tril, ws_ref[g], 0.0).astype(BF16)
            z = jnp.dot(wm, vb[:, gs], preferred_element_type=F32) + bst_ref[:, g:g + 1]
            yb_scr[rs, gs] = (u[:, gs] * z).astype(BF16)
    ma = jnp.dot(ya_ref[...], wa_ref[...], preferred_element_type=F32)
    mb = jnp.dot(yb_scr[...], wb_ref[...], preferred_element_type=F32)
    o_ref[...] = (jax.nn.sigmoid(ga_ref[...]) * ma + jax.nn.sigmoid(gb_ref[...]) * mb).astype(o_ref.dtype)


def _gmlp_mix(gates, ya, norm_g, w_s, b_s_t, wa_bf, wb_bf, d_model):
    t = ya.shape[0]
    tt = min(MIX_TT, t)
    assert d_model == 2 * GMLP_WIDTH
    return pl.pallas_call(
        _gmlp_mix_kernel,
        out_shape=jax.ShapeDtypeStruct((t, d_model), BF16),
        grid=(t // tt,),
        in_specs=[
            pl.BlockSpec((tt, GMLP_WIDTH), lambda i: (i, 0)),
            pl.BlockSpec((tt, GMLP_WIDTH), lambda i: (i, 1)),
            pl.BlockSpec((tt, d_model), lambda i: (i, 1)),
            pl.BlockSpec((tt, d_model), lambda i: (i, 2)),
            pl.BlockSpec((tt, ATTN_WIDTH), lambda i: (i, 0)),
            pl.BlockSpec((1, GMLP_WIDTH), lambda i: (0, 0)),
            pl.BlockSpec((GMLP_GROUPS, CHUNK, CHUNK), lambda i: (0, 0, 0)),
            pl.BlockSpec((CHUNK, GMLP_GROUPS), lambda i: (0, 0)),
            pl.BlockSpec((ATTN_WIDTH, d_model), lambda i: (0, 0)),
            pl.BlockSpec((GMLP_WIDTH, d_model), lambda i: (0, 0)),
        ],
        out_specs=pl.BlockSpec((tt, d_model), lambda i: (i, 0)),
        scratch_shapes=[pltpu.VMEM((tt, GMLP_WIDTH), BF16)],
        compiler_params=_cparams(("parallel",)),
        name="gmlp_mix",
    )(gates, gates, gates, gates, ya, norm_g.reshape(1, GMLP_WIDTH), w_s, b_s_t, wa_bf, wb_bf)


def _out_proj_kernel(x_ref, m_ref, wo_ref, g_ref, x1_ref, h2_ref):
    x1 = x_ref[...] + jnp.dot(m_ref[...], wo_ref[...], preferred_element_type=F32)
    x1_ref[...] = x1
    ms = jnp.mean(x1 * x1, axis=-1, keepdims=True)
    h2_ref[...] = ((x1 * lax.rsqrt(ms + EPS)) * g_ref[...]).astype(BF16)


def _out_proj(x2, mixed, wo_bf, g):
    t, d = x2.shape
    tt = min(RES_TT, t)
    return pl.pallas_call(
        _out_proj_kernel,
        out_shape=(jax.ShapeDtypeStruct((t, d), F32), jax.ShapeDtypeStruct((t, d), BF16)),
        grid=(t // tt,),
        in_specs=[
            pl.BlockSpec((tt, d), lambda i: (i, 0)),
            pl.BlockSpec((tt, d), lambda i: (i, 0)),
            pl.BlockSpec((d, d), lambda i: (0, 0)),
            pl.BlockSpec((1, d), lambda i: (0, 0)),
        ],
        out_specs=(pl.BlockSpec((tt, d), lambda i: (i, 0)), pl.BlockSpec((tt, d), lambda i: (i, 0))),
        compiler_params=_cparams(("parallel",)),
        name="out_proj",
    )(x2, mixed, wo_bf, g.reshape(1, d))


def _peer_scores_kernel(h2_ref, wq_ref, sk_ref, st_ref):
    qp = jnp.dot(h2_ref[...], wq_ref[...], preferred_element_type=F32).astype(BF16)
    half = sk_ref.shape[2]
    for hp in range(2 * PEER_HEADS):
        st_ref[hp] = lax.dot_general(sk_ref[hp], qp[:, hp * half:(hp + 1) * half], (((1,), (1,)), ((), ())),
                                     preferred_element_type=F32)


def _peer_scores(h2, wq_bf, sk_bf):
    t, d = h2.shape
    tt = min(RES_TT, t)
    nq = wq_bf.shape[1]
    half = sk_bf.shape[2]
    return pl.pallas_call(
        _peer_scores_kernel,
        out_shape=jax.ShapeDtypeStruct((2 * PEER_HEADS, N_KEYS, t), F32),
        grid=(t // tt,),
        in_specs=[
            pl.BlockSpec((tt, d), lambda i: (i, 0)),
            pl.BlockSpec((d, nq), lambda i: (0, 0)),
            pl.BlockSpec((2 * PEER_HEADS, N_KEYS, half), lambda i: (0, 0, 0)),
        ],
        out_specs=pl.BlockSpec((2 * PEER_HEADS, N_KEYS, tt), lambda i: (0, 0, i)),
        compiler_params=_cparams(("parallel",)),
        name="peer_scores",
    )(h2, wq_bf, sk_bf)


def _top_values(cur, k):
    n = cur.shape[0]
    idx = lax.broadcasted_iota(I32, cur.shape, 0).astype(F32)
    vals = []
    for _ in range(k):
        mx = jnp.max(cur, axis=0, keepdims=True)
        first = jnp.min(jnp.where(cur == mx, idx, float(n)), axis=0, keepdims=True)
        vals.append(mx)
        cur = jnp.where(idx == first, -jnp.inf, cur)
    return vals


def _staircase_sums(t1, t2):
    a1 = jnp.concatenate(t1, axis=0)
    a2 = jnp.concatenate(t2, axis=0)
    r16 = lax.broadcasted_iota(I32, (PEER_TOPK, LANES), 0)
    r8 = r16[:8]
    ninf = -jnp.inf
    return jnp.concatenate([
        t1[0] + a2,
        t1[1] + a2[:8],
        jnp.where(r16 >= 2, a1 + t2[0], ninf),
        jnp.where(r8 >= 2, a1[:8] + t2[1], ninf),
        jnp.where((r8 >= 2) & (r8 <= 4), t1[2] + a2[:8], ninf),
        jnp.where((r8 >= 2) & (r8 <= 3), t1[3] + a2[:8], ninf),
        jnp.where(r8 == 2, t1[4] + a2[:8], ninf),
    ], axis=0)


def _top_ranked(cur, k):
    n = cur.shape[0]
    idx = lax.broadcasted_iota(I32, cur.shape, 0).astype(F32)
    rank = jnp.full(cur.shape, float(k), F32)
    vals, firsts = [], []
    for r in range(k):
        mx = jnp.max(cur, axis=0, keepdims=True)
        first = jnp.min(jnp.where(cur == mx, idx, float(n)), axis=0, keepdims=True)
        hit = idx == first
        rank = jnp.where(hit, float(r), rank)
        cur = jnp.where(hit, -jnp.inf, cur)
        vals.append(mx)
        firsts.append(first)
    return vals, firsts, rank


def _peer_select_kernel(st_ref, quota_ref, e1_ref, rank_ref, e2_ref):
    tl = st_ref.shape[2]
    idx = lax.broadcasted_iota(I32, (N_KEYS, LANES), 0).astype(F32)

    def lane_group(g, carry):
        ls = pl.ds(pl.multiple_of(g * LANES, LANES), LANES)
        for h in range(PEER_HEADS):
            s1 = st_ref[2 * h, :, ls]
            s2 = st_ref[2 * h + 1, :, ls]
            t1, first1, _ = _top_ranked(s1, PEER_TOPK)
            t2, _, rank2 = _top_ranked(s2, PEER_TOPK)
            best = _top_values(_staircase_sums(t1, t2), PEER_TOPK)
            m = best[0]
            theta = best[PEER_TOPK - 1]
            z = functools.reduce(lambda a, b: a + b, [jnp.exp(bs - m) for bs in best])
            a2 = jnp.concatenate(t2, axis=0)
            quota = jnp.zeros((N_KEYS, LANES), F32)
            for a in range(PEER_TOPK):
                n_ok = jnp.sum(jnp.where(t1[a] + a2 >= theta, 1.0, 0.0), axis=0, keepdims=True)
                quota = jnp.where(idx == first1[a], n_ok, quota)
            quota_ref[h, :, ls] = quota
            e1_ref[h, :, ls] = jnp.exp(s1 - t1[0]) / z
            rank_ref[h, :, ls] = rank2.astype(BF16)
            e2_ref[h, :, ls] = jnp.exp(s2 - t2[0]).astype(BF16)
        return carry

    lax.fori_loop(0, tl // LANES, lane_group, 0)


def _peer_select(st):
    t = st.shape[2]
    tl = min(SEL_TL, t)
    f_shape = jax.ShapeDtypeStruct((PEER_HEADS, N_KEYS, t), F32)
    b_shape = jax.ShapeDtypeStruct((PEER_HEADS, N_KEYS, t), BF16)
    spec = pl.BlockSpec((PEER_HEADS, N_KEYS, tl), lambda i: (0, 0, i))
    return pl.pallas_call(
        _peer_select_kernel,
        out_shape=(f_shape, f_shape, b_shape, b_shape),
        grid=(t // tl,),
        in_specs=[pl.BlockSpec((2 * PEER_HEADS, N_KEYS, tl), lambda i: (0, 0, i))],
        out_specs=(spec, spec, spec, spec),
        compiler_params=_cparams(("parallel",)),
        name="peer_select",
    )(st)


def _peer_dense_kernel(h2_ref, u_ref, vt_ref, quota_ref, e1row_ref, rank_ref, e2_ref, o_ref,
                       rank_scr, e2_scr, act_scr, p_scr, acc_scr):
    ei = pl.program_id(1)
    tt = h2_ref.shape[0]
    pack = 16

    @pl.when(ei == 0)
    def _():
        acc_scr[...] = jnp.zeros(acc_scr.shape, F32)
        rank_scr[:, :, :tt] = rank_ref[...]
        e2_scr[:, :, :tt] = e2_ref[...]

    act_scr[:, :tt] = lax.dot_general(u_ref[...], h2_ref[...], (((1,), (1,)), ((), ())),
                                      preferred_element_type=F32)
    for ii in range(PEER_ROWS):
        for lc in range(tt // LANES):
            ls = slice(lc * LANES, (lc + 1) * LANES)
            coef = [None] * (N_KEYS // pack)
            for h in range(PEER_HEADS):
                quota = jnp.broadcast_to(quota_ref[h, ii:ii + 1, ls], (pack, LANES)).astype(BF16)
                e1row = jnp.broadcast_to(e1row_ref[h, ii:ii + 1, ls], (pack, LANES)).astype(BF16)
                for k in range(N_KEYS // pack):
                    ks = slice(k * pack, (k + 1) * pack)
                    gate = jnp.where(rank_scr[h, ks, ls] < quota, e2_scr[h, ks, ls] * e1row, 0.0)
                    coef[k] = gate if coef[k] is None else coef[k] + gate
            for k in range(N_KEYS // pack):
                rs = slice(ii * N_KEYS + k * pack, ii * N_KEYS + (k + 1) * pack)
                p_scr[rs, ls] = coef[k] * jax.nn.gelu(act_scr[rs, ls].astype(BF16))
    acc_scr[...] += jnp.dot(vt_ref[...], p_scr[:, :tt], preferred_element_type=F32)

    @pl.when(ei == pl.num_programs(1) - 1)
    def _():
        o_ref[...] = acc_scr[...].T


def _peer_dense(h2, u_bf, vt_bf, quota, e1, rank2, e2):
    t, d = h2.shape
    n_exp = u_bf.shape[0]
    tt = min(PEER_TT, t)
    te = PEER_ROWS * N_KEYS
    row_spec = pl.BlockSpec((PEER_HEADS, PEER_ROWS, tt), lambda i, e: (0, e, i))
    tok_spec = pl.BlockSpec((PEER_HEADS, N_KEYS, tt), lambda i, e: (0, 0, i))
    return pl.pallas_call(
        _peer_dense_kernel,
        out_shape=jax.ShapeDtypeStruct((t, d), F32),
        grid=(t // tt, n_exp // te),
        in_specs=[
            pl.BlockSpec((tt, d), lambda i, e: (i, 0)),
            pl.BlockSpec((te, d), lambda i, e: (e, 0)),
            pl.BlockSpec((d, te), lambda i, e: (0, e)),
            row_spec,
            row_spec,
            tok_spec,
            tok_spec,
        ],
        out_specs=pl.BlockSpec((tt, d), lambda i, e: (i, 0)),
        scratch_shapes=[
            pltpu.VMEM((PEER_HEADS, N_KEYS, tt + LANES), BF16),
            pltpu.VMEM((PEER_HEADS, N_KEYS, tt + LANES), BF16),
            pltpu.VMEM((te, tt + LANES), F32),
            pltpu.VMEM((te, tt + LANES), BF16),
            pltpu.VMEM((d, tt), F32),
        ],
        compiler_params=_cparams(("parallel", "arbitrary")),
        name="peer_dense",
    )(h2, u_bf, vt_bf, quota, e1, rank2, e2)


def _residual_norm_kernel(x1_ref, p_ref, g_ref, o_ref, *, final):
    y = x1_ref[...] + p_ref[...]
    if final:
        ms = jnp.mean(y * y, axis=-1, keepdims=True)
        y = (y * lax.rsqrt(ms + EPS)) * g_ref[...]
    o_ref[...] = y


def _residual_norm(x1, peer_out, g, final):
    t, d = x1.shape
    tt = min(NORM_TT, t)
    return pl.pallas_call(
        functools.partial(_residual_norm_kernel, final=final),
        out_shape=jax.ShapeDtypeStruct((t, d), F32),
        grid=(t // tt,),
        in_specs=[
            pl.BlockSpec((tt, d), lambda i: (i, 0)),
            pl.BlockSpec((tt, d), lambda i: (i, 0)),
            pl.BlockSpec((1, d), lambda i: (0, 0)),
        ],
        out_specs=pl.BlockSpec((tt, d), lambda i: (i, 0)),
        compiler_params=_cparams(("parallel",)),
        name="residual_norm",
    )(x1, peer_out, g.reshape(1, d))


def _layer(x2, b, s, ln_mix_g, w_in, gmlp_norm_g, w_spatial, b_spatial, w_branch_attn, w_branch_gmlp,
           w_out, ln_ffn_g, peer_w_q, peer_sub_keys, peer_u, peer_v, out_g, final):
    t, d = x2.shape
    n_qi = IDX_HEADS * IDX_DIM
    o_qi = 3 * ATTN_WIDTH
    o_ki = o_qi + n_qi
    o_wi = o_ki + IDX_DIM
    o_gate = o_wi + IDX_HEADS

    w_attn = w_in[:, :o_ki].astype(BF16)
    w_idx = jnp.pad(w_in[:, o_ki:o_gate], ((0, 0), (0, LANES - IDX_DIM - IDX_HEADS))).astype(BF16)
    w_gate = w_in[:, o_gate:].astype(BF16)

    attn_in = _norm_proj(x2, ln_mix_g, w_attn, BF16, ATTN_TN)
    idx_in = _norm_proj(x2, ln_mix_g, w_idx, F32, LANES)
    gates = _norm_proj(x2, ln_mix_g, w_gate, F32, GATE_TN)

    q = attn_in[:, :ATTN_WIDTH].reshape(b, s, ATTN_WIDTH)
    k = attn_in[:, ATTN_WIDTH:2 * ATTN_WIDTH].reshape(b, s, ATTN_WIDTH)
    v = attn_in[:, 2 * ATTN_WIDTH:o_qi].reshape(b, s, ATTN_WIDTH)
    nq = s // TQ
    qi_r = attn_in[:, o_qi:o_ki].reshape(b, nq, TQ, IDX_HEADS, IDX_DIM)
    qi_r = qi_r.transpose(0, 1, 3, 2, 4).reshape(b, nq, IDX_HEADS * TQ, IDX_DIM)
    kit = idx_in[:, :IDX_DIM].astype(BF16).reshape(b, s, IDX_DIM).transpose(0, 2, 1)
    wi_r = idx_in[:, IDX_DIM:IDX_DIM + IDX_HEADS].reshape(b, nq, TQ, IDX_HEADS)
    wi_r = wi_r.transpose(0, 1, 3, 2).reshape(b, nq, IDX_HEADS * TQ, 1)

    n_sel = min(DSA_TOPK, s // 4)
    bias = _dsa_select(qi_r, wi_r, kit, n_sel)
    ya = _dsa_attend(q, k, v, bias).reshape(t, ATTN_WIDTH)

    mixed = _gmlp_mix(gates, ya, gmlp_norm_g, w_spatial, b_spatial.T, w_branch_attn.astype(BF16),
                      w_branch_gmlp.astype(BF16), d)
    x1, h2 = _out_proj(x2, mixed, w_out.astype(BF16), ln_ffn_g)

    half = peer_sub_keys.shape[-1]
    sk = peer_sub_keys.reshape(2 * PEER_HEADS, N_KEYS, half).astype(BF16)
    st = _peer_scores(h2, peer_w_q.astype(BF16), sk)
    quota, e1, rank2, e2 = _peer_select(st)
    peer_out = _peer_dense(h2, peer_u.astype(BF16), peer_v.astype(BF16).T, quota, e1, rank2, e2)
    return _residual_norm(x1, peer_out, out_g, final)


def kernel(x, ln_mix_g, w_in, gmlp_norm_g, w_spatial, b_spatial, w_branch_attn, w_branch_gmlp, w_out, ln_ffn_g, peer_w_q, peer_sub_keys, peer_u, peer_v, ln_final_g):
    b, s, d = x.shape
    depth = w_in.shape[0]
    x2 = x.reshape(b * s, d)
    for l in range(depth):
        x2 = _layer(x2, b, s, ln_mix_g[l], w_in[l], gmlp_norm_g[l], w_spatial[l], b_spatial[l],
                    w_branch_attn[l], w_branch_gmlp[l], w_out[l], ln_ffn_g[l], peer_w_q[l],
                    peer_sub_keys[l], peer_u[l], peer_v[l], ln_final_g, l + 1 == depth)
    return x2.reshape(b, s, d)
```
